```python
import math
import jax, jax.numpy as jnp
from jax import lax
import numpy as np

D_MODEL = 1024
BATCH = 8
SEQ = 4096
DEPTH = 1

D_MIX = D_MODEL
N_DN_HEADS = 4
DN_HEAD_DIM = 128
D_DN = N_DN_HEADS * DN_HEAD_DIM
CONV_K = 4
CHUNK = 64
POOL_WINDOWS = (2, 4, 8, 16)
N_POOL_GROUPS = len(POOL_WINDOWS)
D_POOL = D_MIX - D_DN
POOL_GROUP_DIM = D_POOL // N_POOL_GROUPS
D_IN_PROJ = 4 * D_DN + 2 * N_DN_HEADS + D_POOL
SPLITS = (3 * D_DN, 4 * D_DN, 4 * D_DN + N_DN_HEADS, 4 * D_DN + 2 * N_DN_HEADS)
N_EXPERTS = 32
TOP_K = 4
D_EXPERT = D_MODEL
SWIGLU_LIMIT = 7.0
SWIGLU_ALPHA = 1.702
MOE_BLOCK = 256
RMS_EPS = 1e-6
L2_EPS = 1e-6

kernel_name = 'hybrid_deltanet_pool_moe_block'


def _rmsnorm(x, w):
    xf = x.astype(jnp.float32)
    y = xf * lax.rsqrt(jnp.mean(xf * xf, axis=-1, keepdims=True) + RMS_EPS)
    return (y * w.astype(jnp.float32)).astype(x.dtype)


def _l2norm(x):
    return x * lax.rsqrt(jnp.sum(x * x, axis=-1, keepdims=True) + L2_EPS)


def _causal_depthwise_conv(x, w):
    c = x.shape[-1]
    return lax.conv_general_dilated(
        x, w.astype(x.dtype)[:, None, :], window_strides=(1,),
        padding=[(CONV_K - 1, 0)], dimension_numbers=('NWC', 'WIO', 'NWC'),
        feature_group_count=c)


def _chunk_gated_delta_rule(q, k, v, g, beta):
    b, t, h, dk = q.shape
    dv = v.shape[-1]
    n = t // CHUNK

    def heads_chunks(u):
        u = jnp.moveaxis(u, 2, 1)
        return u.reshape(b, h, n, CHUNK, *u.shape[3:])

    q = heads_chunks(q) * (dk ** -0.5)
    k = heads_chunks(k)
    v = heads_chunks(v)
    beta = heads_chunks(beta)
    g = jnp.cumsum(heads_chunks(g), axis=-1)
    pos = jnp.arange(CHUNK)
    causal = pos[:, None] >= pos[None, :]
    strict = pos[:, None] > pos[None, :]
    decay = jnp.exp(jnp.where(causal, g[..., :, None] - g[..., None, :], -jnp.inf))
    k_beta = k * beta[..., None]
    a_mat = jnp.where(strict, jnp.einsum('bhnid,bhnjd->bhnij', k_beta, k) * decay, 0.0)
    eye = jnp.broadcast_to(jnp.eye(CHUNK, dtype=q.dtype), a_mat.shape)
    t_mat = lax.linalg.triangular_solve(a_mat, eye, left_side=True, lower=True, unit_diagonal=True)
    u = jnp.einsum('bhnij,bhnjd->bhnid', t_mat, v * beta[..., None])
    w = jnp.einsum('bhnij,bhnjd->bhnid', t_mat, k_beta * jnp.exp(g)[..., None])
    qk = jnp.where(causal, jnp.einsum('bhnid,bhnjd->bhnij', q, k) * decay, 0.0)
    q_dec = q * jnp.exp(g)[..., None]
    k_dec = k * jnp.exp(g[..., -1:] - g)[..., None]
    g_tot = jnp.exp(g[..., -1])

    def step(state, inp):
        qd, kd, u_c, w_c, qk_c, gt = inp
        v_new = u_c - jnp.einsum('bhid,bhde->bhie', w_c, state)
        o = jnp.einsum('bhid,bhde->bhie', qd, state) + jnp.einsum('bhij,bhje->bhie', qk_c, v_new)
        state = state * gt[..., None, None] + jnp.einsum('bhid,bhie->bhde', kd, v_new)
        return state, o

    xs = tuple(jnp.moveaxis(a, 2, 0) for a in (q_dec, k_dec, u, w, qk, g_tot))
    s0 = jnp.zeros((b, h, dk, dv), q.dtype)
    _, o = lax.scan(step, s0, xs)
    o = jnp.moveaxis(o, 0, 2).reshape(b, h, t, dv)
    return jnp.moveaxis(o, 1, 2)


def _multiscale_pool(p, pool_w, pool_scale):
    b, t, _ = p.shape
    pf = p.astype(jnp.float32).reshape(b, t, N_POOL_GROUPS, POOL_GROUP_DIM)
    cs = jnp.cumsum(pf, axis=1)
    pos = jnp.arange(1, t + 1, dtype=jnp.float32)
    means = []
    for gi, win in enumerate(POOL_WINDOWS):
        c = cs[:, :, gi]
        lag = jnp.pad(c, ((0, 0), (win, 0), (0, 0)))[:, :t]
        cnt = jnp.minimum(pos, float(win))[None, :, None]
        means.append((c - lag) / cnt)
    mixed = jnp.stack(means, axis=2) - pf
    y = jnp.einsum('btgc,gcd->btgd', mixed, pool_w.astype(jnp.float32))
    return (y.reshape(b, t, D_POOL) * pool_scale.astype(jnp.float32)).astype(p.dtype)


def _moe(h, w_router, b_router, w_gate_up, b_gate_up, w_down, b_down):
    d = h.shape[-1]
    xt = h.reshape(-1, d)
    n_tok = xt.shape[0]
    logits = (xt @ w_router + b_router).astype(jnp.float32)
    top_val, top_idx = lax.top_k(logits, TOP_K)
    gates = jax.nn.softmax(top_val, axis=-1)
    n_assign = n_tok * TOP_K
    flat_e = top_idx.reshape(-1)
    flat_tok = jnp.arange(n_assign, dtype=jnp.int32) // TOP_K
    flat_w = gates.reshape(-1)
    order = jnp.argsort(flat_e)
    sorted_e = flat_e[order]
    counts = jnp.bincount(flat_e, length=N_EXPERTS)
    padded = ((counts + MOE_BLOCK - 1) // MOE_BLOCK) * MOE_BLOCK
    pad_end = jnp.cumsum(padded)
    pad_start = pad_end - padded
    start = jnp.cumsum(counts) - counts
    rank = jnp.arange(n_assign, dtype=jnp.int32) - start[sorted_e]
    dest = pad_start[sorted_e] + rank
    n_blocks = -(-n_assign // MOE_BLOCK) + N_EXPERTS
    n_rows = n_blocks * MOE_BLOCK
    row_tok = jnp.zeros((n_rows,), jnp.int32).at[dest].set(flat_tok[order])
    row_w = jnp.zeros((n_rows,), jnp.float32).at[dest].set(flat_w[order])
    block_start = jnp.arange(n_blocks, dtype=jnp.int32) * MOE_BLOCK
    block_e = jnp.minimum(jnp.searchsorted(pad_end, block_start, side='right'), N_EXPERTS - 1)

    def expert_block(args):
        e, tok, wt = args
        xb = xt[tok]
        gu = xb @ w_gate_up[e] + b_gate_up[e]
        gate = jnp.minimum(gu[:, :D_EXPERT], SWIGLU_LIMIT)
        up = jnp.clip(gu[:, D_EXPERT:], -SWIGLU_LIMIT, SWIGLU_LIMIT)
        act = (up + 1.0) * gate * jax.nn.sigmoid(SWIGLU_ALPHA * gate)
        yb = act @ w_down[e] + b_down[e]
        return yb * wt.astype(yb.dtype)[:, None]

    y = lax.map(expert_block, (block_e, row_tok.reshape(n_blocks, MOE_BLOCK),
                               row_w.reshape(n_blocks, MOE_BLOCK)))
    out = jax.ops.segment_sum(y.reshape(n_rows, d), row_tok, num_segments=n_tok)
    return out.reshape(h.shape).astype(h.dtype)


def setup_inputs(seed: int = 0) -> dict:
    key = jax.random.key(seed)
    ks = jax.random.split(key, 18)
    L = DEPTH
    f32 = jnp.float32

    def nrm(k, shape, scale):
        return jax.random.normal(k, shape, f32) * scale

    x = nrm(ks[0], (BATCH, SEQ, D_MODEL), 1.0)
    attn_norm_w = 1.0 + nrm(ks[1], (L, D_MODEL), 0.02)
    w_in = nrm(ks[2], (L, D_MODEL, D_IN_PROJ), D_MODEL ** -0.5)
    conv_w = nrm(ks[3], (L, CONV_K, 3 * D_DN), CONV_K ** -0.5)
    a_log = jnp.log(jax.random.uniform(ks[4], (L, N_DN_HEADS), f32, 1.0, 16.0))
    dt = jnp.exp(jax.random.uniform(ks[5], (L, N_DN_HEADS), f32, math.log(1e-3), math.log(1e-1)))
    dt_bias = dt + jnp.log(-jnp.expm1(-dt))
    dn_norm_w = 1.0 + nrm(ks[6], (L, DN_HEAD_DIM), 0.02)
    pool_w = nrm(ks[7], (L, N_POOL_GROUPS, POOL_GROUP_DIM, POOL_GROUP_DIM), POOL_GROUP_DIM ** -0.5)
    pool_scale = 1.0 + nrm(ks[8], (L, D_POOL), 0.1)
    w_out = nrm(ks[9], (L, D_MIX, D_MODEL), D_MIX ** -0.5)
    ffn_norm_w = 1.0 + nrm(ks[10], (L, D_MODEL), 0.02)
    w_router = nrm(ks[11], (L, D_MODEL, N_EXPERTS), D_MODEL ** -0.5)
    b_router = nrm(ks[12], (L, N_EXPERTS), 0.01)
    w_gate_up = nrm(ks[13], (L, N_EXPERTS, D_MODEL, 2 * D_EXPERT), D_MODEL ** -0.5)
    b_gate_up = nrm(ks[14], (L, N_EXPERTS, 2 * D_EXPERT), 0.01)
    w_down = nrm(ks[15], (L, N_EXPERTS, D_EXPERT, D_MODEL), D_EXPERT ** -0.5)
    b_down = nrm(ks[16], (L, N_EXPERTS, D_MODEL), 0.01)
    final_norm_w = 1.0 + nrm(ks[17], (D_MODEL,), 0.02)
    return {'x': x, 'attn_norm_w': attn_norm_w, 'w_in': w_in, 'conv_w': conv_w,
            'a_log': a_log, 'dt_bias': dt_bias, 'dn_norm_w': dn_norm_w,
            'pool_w': pool_w, 'pool_scale': pool_scale, 'w_out': w_out,
            'ffn_norm_w': ffn_norm_w, 'w_router': w_router, 'b_router': b_router,
            'w_gate_up': w_gate_up, 'b_gate_up': b_gate_up, 'w_down': w_down,
            'b_down': b_down, 'final_norm_w': final_norm_w}


def reference(x, attn_norm_w, w_in, conv_w, a_log, dt_bias, dn_norm_w, pool_w, pool_scale,
              w_out, ffn_norm_w, w_router, b_router, w_gate_up, b_gate_up, w_down, b_down,
              final_norm_w):
    b, t, _ = x.shape
    f32 = jnp.float32
    for l in range(DEPTH):
        h = _rmsnorm(x, attn_norm_w[l])
        proj = h @ w_in[l]
        qkv, z, beta_logit, a_in, p = jnp.split(proj, list(SPLITS), axis=-1)
        qkv = jax.nn.silu(_causal_depthwise_conv(qkv, conv_w[l])).astype(f32)
        q, k, v = jnp.split(qkv, 3, axis=-1)
        q = _l2norm(q.reshape(b, t, N_DN_HEADS, DN_HEAD_DIM))
        k = _l2norm(k.reshape(b, t, N_DN_HEADS, DN_HEAD_DIM))
        v = v.reshape(b, t, N_DN_HEADS, DN_HEAD_DIM)
        beta = jax.nn.sigmoid(beta_logit.astype(f32))
        g = -jnp.exp(a_log[l].astype(f32)) * jax.nn.softplus(a_in.astype(f32) + dt_bias[l].astype(f32))
        o = _chunk_gated_delta_rule(q, k, v, g, beta)
        o = _rmsnorm(o, dn_norm_w[l]) * jax.nn.silu(z.astype(f32).reshape(b, t, N_DN_HEADS, DN_HEAD_DIM))
        dn_out = o.reshape(b, t, D_DN).astype(x.dtype)
        pool_out = _multiscale_pool(p, pool_w[l], pool_scale[l])
        x = x + jnp.concatenate([dn_out, pool_out], axis=-1) @ w_out[l]
        x = x + _moe(_rmsnorm(x, ffn_norm_w[l]), w_router[l], b_router[l],
                     w_gate_up[l], b_gate_up[l], w_down[l], b_down[l])
    return _rmsnorm(x, final_norm_w)
```

```python
import functools

import jax
import jax.numpy as jnp
from jax import lax
from jax.experimental import pallas as pl
from jax.experimental.pallas import tpu as pltpu

F32 = jnp.float32
BF16 = jnp.bfloat16

D_MODEL = 1024
N_DN_HEADS = 4
DN_HEAD_DIM = 128
D_DN = N_DN_HEADS * DN_HEAD_DIM
CONV_K = 4
CHUNK = 64
POOL_WINDOWS = (2, 4, 8, 16)
N_POOL_GROUPS = len(POOL_WINDOWS)
D_POOL = D_MODEL - D_DN
POOL_GROUP_DIM = D_POOL // N_POOL_GROUPS
N_EXPERTS = 32
TOP_K = 4
D_EXPERT = D_MODEL
SWIGLU_LIMIT = 7.0
SWIGLU_ALPHA = 1.702
RMS_EPS = 1e-6
L2_EPS = 1e-6

LANES = 128
SUBLANES = 8
VMEM_LIMIT_BYTES = 56 * 1024 * 1024

MIX_TILE = 256
PAIR = 2 * CHUNK
CONV_HALO = SUBLANES
POOL_HALO = 2 * SUBLANES
MOE_ROWS = 512
ROW_TILE = 256
FF_CHUNK = 512

COL_QKV = 0
COL_Z = 3 * D_DN
COL_P = 4 * D_DN
COL_BA = 4 * D_DN + D_POOL
D_IN_R = COL_BA + LANES
A_LANE0 = N_DN_HEADS


def _dot(a, b):
    return jnp.dot(a, b, preferred_element_type=F32)


def _dot_nt(a, b):
    return lax.dot_general(a, b, (((1,), (1,)), ((), ())), preferred_element_type=F32)


def _hi_lo(x):
    hi = x.astype(BF16)
    lo = (x - hi.astype(F32)).astype(BF16)
    return hi, lo


def _dot_f32(a, b):
    ah, al = _hi_lo(a)
    bh, bl = _hi_lo(b)
    return _dot(ah, bh) + (_dot(ah, bl) + _dot(al, bh))


def _dot_exact_lhs(m_bf16, x):
    hi = x.astype(BF16)
    r = x - hi.astype(F32)
    mid = r.astype(BF16)
    lo = (r - mid.astype(F32)).astype(BF16)
    return _dot(m_bf16, hi) + (_dot(m_bf16, mid) + _dot(m_bf16, lo))


def _rms_scale(x):
    return lax.rsqrt(jnp.mean(x * x, axis=-1, keepdims=True) + RMS_EPS)


def _silu(x):
    return x * jax.nn.sigmoid(x)


def _softplus(x):
    return jnp.maximum(x, 0.0) + jnp.log1p(jnp.exp(-jnp.abs(x)))


def _mix_kernel(x_ref, anw_ref, win_ref, convw_ref, alog_ref, dtb_ref, dnw_ref, poolw_ref, pscale_ref,
                wout_ref, fnw_ref, wr_ref, br_ref,
                x1_ref, route_ref, gate_ref, counts_ref,
                qkv_ext, p_ext, state, o_buf, cnt):
    b = pl.program_id(0)
    t = pl.program_id(1)
    tt = MIX_TILE

    @pl.when(t == 0)
    def _():
        qkv_ext[0:CONV_HALO, :] = jnp.zeros((CONV_HALO, 3 * D_DN), F32)
        p_ext[0:POOL_HALO, :] = jnp.zeros((POOL_HALO, D_POOL), F32)
        state[...] = jnp.zeros(state.shape, F32)

    @pl.when(jnp.logical_and(b == 0, t == 0))
    def _():
        cnt[...] = jnp.zeros(cnt.shape, F32)

    x = x_ref[...]
    hb = (x * _rms_scale(x) * anw_ref[...]).astype(BF16)
    qkv_ext[CONV_HALO:CONV_HALO + tt, :] = _dot(hb, win_ref[:, COL_QKV:COL_Z])
    z = _dot(hb, win_ref[:, COL_Z:COL_P])
    p_ext[POOL_HALO:POOL_HALO + tt, :] = _dot(hb, win_ref[:, COL_P:COL_BA])
    ba = _dot(hb, win_ref[:, COL_BA:D_IN_R])

    beta = jax.nn.sigmoid(ba)
    g = -jnp.exp(alog_ref[...]) * _softplus(ba + dtb_ref[...])
    ri = lax.broadcasted_iota(jnp.int32, (tt, tt), 0)
    ci = lax.broadcasted_iota(jnp.int32, (tt, tt), 1)
    same_chunk = (ri // CHUNK) == (ci // CHUNK)
    l_incl = jnp.where(jnp.logical_and(same_chunk, ri >= ci), 1.0, 0.0).astype(BF16)
    l_all = jnp.where(same_chunk, 1.0, 0.0).astype(BF16)
    gc = _dot_exact_lhs(l_incl, g)
    gl = _dot_exact_lhs(l_all, g)
    eg = jnp.exp(gc)
    ekd = jnp.exp(gl - gc)
    egl = jnp.exp(gl)
    gct = gc.T

    pr = lax.broadcasted_iota(jnp.int32, (PAIR, PAIR), 0)
    pc = lax.broadcasted_iota(jnp.int32, (PAIR, PAIR), 1)
    pair_same = (pr // CHUNK) == (pc // CHUNK)
    causal = jnp.logical_and(pair_same, pr >= pc)
    strict = jnp.logical_and(pair_same, pr > pc)
    eye = jnp.where(pr == pc, 1.0, 0.0).astype(F32)
    zeros_chunk = jnp.zeros((CHUNK, DN_HEAD_DIM), F32)

    def conv_silu(col):
        acc = None
        for j in range(CONV_K):
            rows = pl.ds(CONV_HALO - (CONV_K - 1) + j, tt)
            term = qkv_ext[rows, col * LANES:(col + 1) * LANES] * convw_ref[j:j + 1, col * LANES:(col + 1) * LANES]
            acc = term if acc is None else acc + term
        return _silu(acc)

    def l2n(v):
        return v * lax.rsqrt(jnp.sum(v * v, axis=-1, keepdims=True) + L2_EPS)

    for h in range(N_DN_HEADS):
        q_all = l2n(conv_silu(h)) * (DN_HEAD_DIM ** -0.5)
        k_all = l2n(conv_silu(N_DN_HEADS + h))
        v_all = conv_silu(2 * N_DN_HEADS + h)
        la = A_LANE0 + h
        for d in range(tt // PAIR):
            r0 = d * PAIR
            q = q_all[r0:r0 + PAIR]
            k = k_all[r0:r0 + PAIR]
            v = v_all[r0:r0 + PAIR]
            bcol = beta[r0:r0 + PAIR, h:h + 1]
            gcol = gc[r0:r0 + PAIR, la:la + 1]
            egcol = eg[r0:r0 + PAIR, la:la + 1]
            ekdcol = ekd[r0:r0 + PAIR, la:la + 1]
            grow = gct[la:la + 1, r0:r0 + PAIR]
            diff = gcol - grow
            decay = jnp.where(causal, jnp.exp(jnp.where(causal, diff, 0.0)), 0.0)
            kb = k * bcol
            k16 = k.astype(BF16)
            a_mat = jnp.where(strict, _dot_nt(kb.astype(BF16), k16) * decay, 0.0)
            t_mat = eye - a_mat
            pw = a_mat
            n_sq = CHUNK.bit_length() - 2
            for _ in range(n_sq):
                pw = _dot_f32(pw, pw)
                t_mat = t_mat + _dot_f32(t_mat, pw)
            t16 = t_mat.astype(BF16)
            u = _dot(t16, (v * bcol).astype(BF16))
            w = _dot(t16, (kb * egcol).astype(BF16))
            qk = jnp.where(causal, _dot_nt(q.astype(BF16), k16) * decay, 0.0).astype(BF16)
            qd = (q * egcol).astype(BF16)
            kdt = (k * ekdcol).T.astype(BF16)
            for c in range(PAIR // CHUNK):
                c0 = c * CHUNK
                s = state[h]
                s16 = s.astype(BF16)
                v_new = u[c0:c0 + CHUNK] - _dot(w[c0:c0 + CHUNK].astype(BF16), s16)
                parts = [zeros_chunk] * (PAIR // CHUNK)
                parts[c] = v_new
                v_pad = jnp.concatenate(parts, axis=0).astype(BF16)
                o = _dot(qd[c0:c0 + CHUNK], s16) + _dot(qk[c0:c0 + CHUNK], v_pad)
                o_buf[r0 + c0:r0 + c0 + CHUNK, h * LANES:(h + 1) * LANES] = o
                gt = egl[r0 + c0:r0 + c0 + 1, la:la + 1]
                state[h] = s * gt + _dot(kdt, v_pad)

    dn_parts = []
    for h in range(N_DN_HEADS):
        o = o_buf[:, h * LANES:(h + 1) * LANES]
        zh = z[:, h * LANES:(h + 1) * LANES]
        dn_parts.append(((o * _rms_scale(o) * dnw_ref[...]) * _silu(zh)).astype(BF16))

    pos = (t * tt + lax.broadcasted_iota(jnp.int32, (tt, 1), 0) + 1).astype(F32)
    pool_parts = []
    for gi, win in enumerate(POOL_WINDOWS):
        cols = slice(gi * POOL_GROUP_DIM, (gi + 1) * POOL_GROUP_DIM)
        cur = p_ext[POOL_HALO:POOL_HALO + tt, cols]
        acc = cur
        for sft in range(1, win):
            acc = acc + p_ext[pl.ds(POOL_HALO - sft, tt), cols]
        mixed = acc / jnp.minimum(pos, float(win)) - cur
        y = _dot(mixed.astype(BF16), poolw_ref[gi]) * pscale_ref[:, cols]
        pool_parts.append(y.astype(BF16))

    mix = None
    for i, part in enumerate(dn_parts + pool_parts):
        term = _dot(part, wout_ref[i * LANES:(i + 1) * LANES, :])
        mix = term if mix is None else mix + term
    x1 = x + mix
    x1_ref[...] = x1

    qkv_ext[0:CONV_HALO, :] = qkv_ext[tt:tt + CONV_HALO, :]
    p_ext[0:POOL_HALO, :] = p_ext[tt:tt + POOL_HALO, :]

    hn = x1 * _rms_scale(x1) * fnw_ref[...]
    lane = lax.broadcasted_iota(jnp.int32, (tt, LANES), 1).astype(F32)
    logits = _dot_f32(hn, wr_ref[...]) + br_ref[...]
    work = jnp.where(lane < N_EXPERTS, logits, -jnp.inf)
    vals, idxs = [], []
    for _ in range(TOP_K):
        m = jnp.max(work, axis=-1, keepdims=True)
        idx = jnp.min(jnp.where(work == m, lane, float(LANES)), axis=-1, keepdims=True)
        vals.append(m)
        idxs.append(idx)
        work = jnp.where(lane == idx, -jnp.inf, work)
    exps = [jnp.exp(v - vals[0]) for v in vals]
    denom = exps[0] + exps[1] + exps[2] + exps[3]
    onehot = jnp.zeros((tt, LANES), F32)
    for idx in idxs:
        onehot = onehot + jnp.where(lane == idx, 1.0, 0.0)
    l_strict = jnp.where(ri > ci, 1.0, 0.0).astype(BF16)
    before = _dot(l_strict, onehot.astype(BF16)) + cnt[0:1, :]
    cnt[...] = cnt[...] + jnp.sum(onehot, axis=0, keepdims=True)
    route = jnp.zeros((tt, LANES), F32)
    gates = jnp.zeros((tt, LANES), F32)
    for kk in range(TOP_K):
        rank = jnp.sum(jnp.where(lane == idxs[kk], before, 0.0), axis=-1, keepdims=True)
        route = jnp.where(lane == float(kk), idxs[kk], route)
        route = jnp.where(lane == float(TOP_K + kk), rank, route)
        gates = jnp.where(lane == float(kk), exps[kk] / denom, gates)
    route_ref[...] = route.astype(jnp.int32)
    gate_ref[...] = gates
    counts_ref[...] = cnt[...]


def _mix_call(x, anw, win_r, convw, alog_p, dtb_p, dnw, poolw, pscale, wout, fnw, wr_p, br_p):
    bsz, seq, d = x.shape
    n_t = seq // MIX_TILE
    n_tok = bsz * seq
    const2 = lambda b, t: (0, 0)
    tok_map = lambda b, t: (b * n_t + t, 0)
    in_specs = [
        pl.BlockSpec((None, MIX_TILE, d), lambda b, t: (b, t, 0)),
        pl.BlockSpec(anw.shape, const2),
        pl.BlockSpec(win_r.shape, const2),
        pl.BlockSpec(convw.shape, const2),
        pl.BlockSpec(alog_p.shape, const2),
        pl.BlockSpec(dtb_p.shape, const2),
        pl.BlockSpec(dnw.shape, const2),
        pl.BlockSpec(poolw.shape, lambda b, t: (0, 0, 0)),
        pl.BlockSpec(pscale.shape, const2),
        pl.BlockSpec(wout.shape, const2),
        pl.BlockSpec(fnw.shape, const2),
        pl.BlockSpec(wr_p.shape, const2),
        pl.BlockSpec(br_p.shape, const2),
    ]
    out_shape = (
        jax.ShapeDtypeStruct((bsz, seq, d), F32),
        jax.ShapeDtypeStruct((n_tok, LANES), jnp.int32),
        jax.ShapeDtypeStruct((n_tok, LANES), F32),
        jax.ShapeDtypeStruct((SUBLANES, LANES), F32),
    )
    out_specs = (
        pl.BlockSpec((None, MIX_TILE, d), lambda b, t: (b, t, 0)),
        pl.BlockSpec((MIX_TILE, LANES), tok_map),
        pl.BlockSpec((MIX_TILE, LANES), tok_map),
        pl.BlockSpec((SUBLANES, LANES), const2),
    )
    scratch = [
        pltpu.VMEM((CONV_HALO + MIX_TILE, 3 * D_DN), F32),
        pltpu.VMEM((POOL_HALO + MIX_TILE, D_POOL), F32),
        pltpu.VMEM((N_DN_HEADS, DN_HEAD_DIM, DN_HEAD_DIM), F32),
        pltpu.VMEM((MIX_TILE, D_DN), F32),
        pltpu.VMEM((SUBLANES, LANES), F32),
    ]
    return pl.pallas_call(
        _mix_kernel,
        grid=(bsz, n_t),
        in_specs=in_specs,
        out_specs=out_specs,
        out_shape=out_shape,
        scratch_shapes=scratch,
        compiler_params=pltpu.CompilerParams(
            dimension_semantics=("arbitrary", "arbitrary"), vmem_limit_bytes=VMEM_LIMIT_BYTES),
        name="mix",
    )(x, anw, win_r, convw, alog_p, dtb_p, dnw, poolw, pscale, wout, fnw, wr_p, br_p)


def _row_copy_wait(src_hbm_or_vmem, dst, sem, n):
    def body(i, carry):
        pltpu.make_async_copy(src_hbm_or_vmem.at[pl.ds(0, 1)], dst.at[pl.ds(0, 1)], sem).wait()
        return carry
    lax.fori_loop(0, n, body, 0)


def _dispatch_kernel(dest_ref, x1_ref, fnw_ref, xs_in_ref, xs_ref, hn_buf, sem):
    del xs_in_ref
    x1 = x1_ref[...]
    hn_buf[...] = x1 * _rms_scale(x1) * fnw_ref[...]

    def body(i, carry):
        for kk in range(TOP_K):
            r = dest_ref[i * TOP_K + kk]
            pltpu.make_async_copy(hn_buf.at[pl.ds(i, 1)], xs_ref.at[pl.ds(r, 1)], sem).start()
        return carry
    lax.fori_loop(0, ROW_TILE, body, 0)
    _row_copy_wait(hn_buf, xs_ref, sem, ROW_TILE * TOP_K)


def _dispatch_call(dest_flat, x1_flat, fnw, xs_zero):
    n_tok, d = x1_flat.shape
    return pl.pallas_call(
        _dispatch_kernel,
        grid=(n_tok // ROW_TILE,),
        in_specs=[
            pl.BlockSpec((ROW_TILE * TOP_K,), lambda i: (i,), memory_space=pltpu.SMEM),
            pl.BlockSpec((ROW_TILE, d), lambda i: (i, 0)),
            pl.BlockSpec(fnw.shape, lambda i: (0, 0)),
            pl.BlockSpec(memory_space=pl.ANY),
        ],
        out_specs=pl.BlockSpec(memory_space=pl.ANY),
        out_shape=jax.ShapeDtypeStruct(xs_zero.shape, F32),
        scratch_shapes=[pltpu.VMEM((ROW_TILE, d), F32), pltpu.SemaphoreType.DMA(())],
        input_output_aliases={3: 0},
        compiler_params=pltpu.CompilerParams(dimension_semantics=("arbitrary",)),
        name="dispatch",
    )(dest_flat, x1_flat, fnw, xs_zero)


def _moe_kernel(be_ref, nu_ref, xs_ref, wgu_ref, bgu_ref, wd_ref, bd_ref, ys_ref, wgu16, wd16):
    i = pl.program_id(0)

    @pl.when(i < nu_ref[0])
    def _():
        prev = be_ref[jnp.maximum(i - 1, 0)]
        new_expert = jnp.logical_or(i == 0, be_ref[i] != prev)

        @pl.when(new_expert)
        def _():
            wgu16[...] = wgu_ref[...].astype(BF16)
            wd16[...] = wd_ref[...].astype(BF16)

        xb = xs_ref[...].astype(BF16)
        acc = None
        for j in range(D_EXPERT // FF_CHUNK):
            c0 = j * FF_CHUNK
            gate = _dot(xb, wgu16[:, c0:c0 + FF_CHUNK]) + bgu_ref[:, c0:c0 + FF_CHUNK]
            up = _dot(xb, wgu16[:, D_EXPERT + c0:D_EXPERT + c0 + FF_CHUNK]) + bgu_ref[:, D_EXPERT + c0:D_EXPERT + c0 + FF_CHUNK]
            gate = jnp.minimum(gate, SWIGLU_LIMIT)
            up = jnp.clip(up, -SWIGLU_LIMIT, SWIGLU_LIMIT)
            act = (up + 1.0) * gate * jax.nn.sigmoid(SWIGLU_ALPHA * gate)
            term = _dot(act.astype(BF16), wd16[c0:c0 + FF_CHUNK, :])
            acc = term if acc is None else acc + term
        ys_ref[...] = acc + bd_ref[...]

    @pl.when(i >= nu_ref[0])
    def _():
        ys_ref[...] = jnp.zeros(ys_ref.shape, F32)


def _moe_call(block_e, n_used, xs, wgu, bgu, wd, bd):
    n_rows, d = xs.shape
    n_blocks = n_rows // MOE_ROWS
    blk = lambda i, be, nu: (jnp.minimum(i, nu[0] - 1), 0)
    exp3 = lambda i, be, nu: (be[jnp.minimum(i, nu[0] - 1)], 0, 0)
    grid_spec = pltpu.PrefetchScalarGridSpec(
        num_scalar_prefetch=2,
        grid=(n_blocks,),
        in_specs=[
            pl.BlockSpec((MOE_ROWS, d), blk),
            pl.BlockSpec((None, d, 2 * D_EXPERT), exp3),
            pl.BlockSpec((None, 1, 2 * D_EXPERT), exp3),
            pl.BlockSpec((None, D_EXPERT, d), exp3),
            pl.BlockSpec((None, 1, d), exp3),
        ],
        out_specs=pl.BlockSpec((MOE_ROWS, d), lambda i, be, nu: (i, 0)),
        scratch_shapes=[pltpu.VMEM((d, 2 * D_EXPERT), BF16), pltpu.VMEM((D_EXPERT, d), BF16)],
    )
    return pl.pallas_call(
        _moe_kernel,
        grid_spec=grid_spec,
        out_shape=jax.ShapeDtypeStruct((n_rows, d), F32),
        compiler_params=pltpu.CompilerParams(
            dimension_semantics=("arbitrary",), vmem_limit_bytes=VMEM_LIMIT_BYTES),
        name="moe",
    )(block_e, n_used, xs, wgu, bgu, wd, bd)


def _combine_kernel(dest_ref, x1_ref, gate_ref, fw_ref, ys_ref, out_ref, ybuf, sem):
    def body(i, carry):
        for kk in range(TOP_K):
            r = dest_ref[i * TOP_K + kk]
            pltpu.make_async_copy(ys_ref.at[pl.ds(r, 1)], ybuf.at[kk, pl.ds(i, 1)], sem).start()
        return carry
    lax.fori_loop(0, ROW_TILE, body, 0)
    _row_copy_wait(ys_ref, ybuf.at[0], sem, ROW_TILE * TOP_K)

    acc = x1_ref[...]
    gates = gate_ref[...]
    for kk in range(TOP_K):
        acc = acc + gates[:, kk:kk + 1] * ybuf[kk]
    out_ref[...] = acc * _rms_scale(acc) * fw_ref[...]


def _combine_call(dest_flat, x1_flat, gates, fw, ys):
    n_tok, d = x1_flat.shape
    return pl.pallas_call(
        _combine_kernel,
        grid=(n_tok // ROW_TILE,),
        in_specs=[
            pl.BlockSpec((ROW_TILE * TOP_K,), lambda i: (i,), memory_space=pltpu.SMEM),
            pl.BlockSpec((ROW_TILE, d), lambda i: (i, 0)),
            pl.BlockSpec((ROW_TILE, LANES), lambda i: (i, 0)),
            pl.BlockSpec(fw.shape, lambda i: (0, 0)),
            pl.BlockSpec(memory_space=pl.ANY),
        ],
        out_specs=pl.BlockSpec((ROW_TILE, d), lambda i: (i, 0)),
        out_shape=jax.ShapeDtypeStruct((n_tok, d), F32),
        scratch_shapes=[pltpu.VMEM((TOP_K, ROW_TILE, d), F32), pltpu.SemaphoreType.DMA(())],
        compiler_params=pltpu.CompilerParams(dimension_semantics=("arbitrary",)),
        name="combine",
    )(dest_flat, x1_flat, gates, fw, ys)


def _pad_lanes(v, lane0):
    out = jnp.zeros((1, LANES), F32)
    return lax.dynamic_update_slice(out, v.reshape(1, -1).astype(F32), (0, lane0))


def _layer(x, attn_norm_w, w_in, conv_w, a_log, dt_bias, dn_norm_w, pool_w, pool_scale, w_out,
           ffn_norm_w, w_router, b_router, w_gate_up, b_gate_up, w_down, b_down):
    bsz, seq, d = x.shape
    n_tok = bsz * seq
    assert d == D_MODEL and seq % MIX_TILE == 0 and n_tok % ROW_TILE == 0

    s_qkv, s_z, s_b, s_a = 3 * D_DN, 4 * D_DN, 4 * D_DN + N_DN_HEADS, 4 * D_DN + 2 * N_DN_HEADS
    ba_cols = jnp.zeros((d, LANES), w_in.dtype).at[:, :2 * N_DN_HEADS].set(w_in[:, s_z:s_a])
    win_r = jnp.concatenate([w_in[:, :s_z], w_in[:, s_a:], ba_cols], axis=1).astype(BF16)
    wr_p = jnp.zeros((d, LANES), F32).at[:, :N_EXPERTS].set(w_router)
    br_p = _pad_lanes(b_router, 0)

    x1, route, gates, counts = _mix_call(
        x, attn_norm_w.reshape(1, d), win_r, conv_w, _pad_lanes(a_log, A_LANE0), _pad_lanes(dt_bias, A_LANE0),
        dn_norm_w.reshape(1, DN_HEAD_DIM), pool_w.astype(BF16), pool_scale.reshape(1, D_POOL),
        w_out.astype(BF16), ffn_norm_w.reshape(1, d), wr_p, br_p)

    cnt = counts[0, :N_EXPERTS].astype(jnp.int32)
    padded = ((cnt + MOE_ROWS - 1) // MOE_ROWS) * MOE_ROWS
    pad_end = jnp.cumsum(padded)
    pad_start = pad_end - padded
    n_blocks = -(-(n_tok * TOP_K) // MOE_ROWS) + N_EXPERTS
    n_rows = n_blocks * MOE_ROWS
    dest = (pad_start[route[:, :TOP_K]] + route[:, TOP_K:2 * TOP_K]).reshape(-1)
    block_start = jnp.arange(n_blocks, dtype=jnp.int32) * MOE_ROWS
    block_e = jnp.minimum(jnp.searchsorted(pad_end, block_start, side='right'), N_EXPERTS - 1).astype(jnp.int32)
    n_used = (pad_end[-1:] // MOE_ROWS).astype(jnp.int32)

    x1_flat = x1.reshape(n_tok, d)
    fnw = ffn_norm_w.reshape(1, d)
    xs = _dispatch_call(dest, x1_flat, fnw, jnp.zeros((n_rows, d), F32))
    ys = _moe_call(block_e, n_used, xs, w_gate_up, b_gate_up.reshape(N_EXPERTS, 1, -1),
                   w_down, b_down.reshape(N_EXPERTS, 1, -1))
    return dest, x1_flat, gates, ys


def kernel(x, attn_norm_w, w_in, conv_w, a_log, dt_bias, dn_norm_w, pool_w, pool_scale, w_out, ffn_norm_w,
           w_router, b_router, w_gate_up, b_gate_up, w_down, b_down, final_norm_w):
    bsz, seq, d = x.shape
    assert w_in.shape[0] == 1, "the combine kernel fuses the final norm, so exactly one layer is supported"
    l = 0
    dest, x1_flat, gates, ys = _layer(
        x, attn_norm_w[l], w_in[l], conv_w[l], a_log[l], dt_bias[l], dn_norm_w[l], pool_w[l], pool_scale[l],
        w_out[l], ffn_norm_w[l], w_router[l], b_router[l], w_gate_up[l], b_gate_up[l], w_down[l], b_down[l])
    return _combine_call(dest, x1_flat, gates, final_norm_w.reshape(1, d), ys).reshape(bsz, seq, d)
```

```python
import jax
import jax.numpy as jnp
from jax import lax
from jax.experimental import pallas as pl
from jax.experimental.pallas import tpu as pltpu

F32 = jnp.float32
BF16 = jnp.bfloat16

D_MODEL = 1024
N_DN_HEADS = 4
DN_HEAD_DIM = 128
D_DN = N_DN_HEADS * DN_HEAD_DIM
CONV_K = 4
CHUNK = 64
POOL_WINDOWS = (2, 4, 8, 16)
N_POOL_GROUPS = len(POOL_WINDOWS)
D_POOL = D_MODEL - D_DN
POOL_GROUP_DIM = D_POOL // N_POOL_GROUPS
N_EXPERTS = 32
TOP_K = 4
D_EXPERT = D_MODEL
SWIGLU_LIMIT = 7.0
SWIGLU_ALPHA = 1.702
RMS_EPS = 1e-6
L2_EPS = 1e-6

LANES = 128
SUBLANES = 8
ROW_SUBLANES = D_MODEL // LANES
VMEM_LIMIT_BYTES = 56 * 1024 * 1024

TILE = 256
PAIR = 2 * CHUNK
CONV_HALO = SUBLANES
POOL_HALO = 2 * SUBLANES
MOE_ROWS = 512
FF_CHUNK = 512
COPY_ROWS = 8
SLOTS = TILE * TOP_K + N_EXPERTS * COPY_ROWS
assert TILE * TOP_K // COPY_ROWS + 1 <= 256

COL_QKV = 0
COL_Z = 3 * D_DN
COL_P = 4 * D_DN
COL_BA = 4 * D_DN + D_POOL
D_IN_R = COL_BA + LANES
A_LANE0 = N_DN_HEADS


def _dot(a, b):
    return jnp.dot(a, b, preferred_element_type=F32)


def _dot_nt(a, b):
    return lax.dot_general(a, b, (((1,), (1,)), ((), ())), preferred_element_type=F32)


def _hi_lo(x):
    hi = x.astype(BF16)
    lo = (x - hi.astype(F32)).astype(BF16)
    return hi, lo


def _dot_f32(a, b):
    ah, al = _hi_lo(a)
    bh, bl = _hi_lo(b)
    return _dot(ah, bh) + (_dot(ah, bl) + _dot(al, bh))


def _dot_exact_lhs(m_bf16, x):
    hi = x.astype(BF16)
    r = x - hi.astype(F32)
    mid = r.astype(BF16)
    lo = (r - mid.astype(F32)).astype(BF16)
    return _dot(m_bf16, hi) + (_dot(m_bf16, mid) + _dot(m_bf16, lo))


def _rms_scale(x):
    return lax.rsqrt(jnp.mean(x * x, axis=-1, keepdims=True) + RMS_EPS)


def _silu(x):
    return x * jax.nn.sigmoid(x)


def _softplus(x):
    return jnp.maximum(x, 0.0) + jnp.log1p(jnp.exp(-jnp.abs(x)))


def _mix_kernel(x_ref, anw_ref, win_ref, convw_ref, alog_ref, dtb_ref, dnw_ref, poolw_ref, pscale_ref,
                wout_ref, fnw_ref, wr_ref, br_ref,
                x1_ref, route_ref, gate_ref, meta_ref,
                qkv_ext, p_ext, state, cnt):
    b = pl.program_id(0)
    t = pl.program_id(1)
    tt = TILE

    @pl.when(t == 0)
    def _():
        qkv_ext[0:CONV_HALO, :] = jnp.zeros((CONV_HALO, 3 * D_DN), F32)
        p_ext[0:POOL_HALO, :] = jnp.zeros((POOL_HALO, D_POOL), F32)
        state[...] = jnp.zeros(state.shape, F32)

    @pl.when(jnp.logical_and(b == 0, t == 0))
    def _():
        cnt[...] = jnp.zeros(cnt.shape, F32)

    x = x_ref[...]
    hb = (x * _rms_scale(x) * anw_ref[...]).astype(BF16)
    qkv_ext[CONV_HALO:CONV_HALO + tt, :] = _dot(hb, win_ref[:, COL_QKV:COL_Z])
    z = _dot(hb, win_ref[:, COL_Z:COL_P])
    p_ext[POOL_HALO:POOL_HALO + tt, :] = _dot(hb, win_ref[:, COL_P:COL_BA])
    ba = _dot(hb, win_ref[:, COL_BA:D_IN_R])

    beta = jax.nn.sigmoid(ba)
    g = -jnp.exp(alog_ref[...]) * _softplus(ba + dtb_ref[...])
    ri = lax.broadcasted_iota(jnp.int32, (tt, tt), 0)
    ci = lax.broadcasted_iota(jnp.int32, (tt, tt), 1)
    same_chunk = (ri // CHUNK) == (ci // CHUNK)
    l_incl = jnp.where(jnp.logical_and(same_chunk, ri >= ci), 1.0, 0.0).astype(BF16)
    l_all = jnp.where(same_chunk, 1.0, 0.0).astype(BF16)
    gc = _dot_exact_lhs(l_incl, g)
    gl = _dot_exact_lhs(l_all, g)
    eg = jnp.exp(gc)
    ekd = jnp.exp(gl - gc)
    egl = jnp.exp(gl)
    gct = gc.T

    pr = lax.broadcasted_iota(jnp.int32, (PAIR, PAIR), 0)
    pc = lax.broadcasted_iota(jnp.int32, (PAIR, PAIR), 1)
    pair_same = (pr // CHUNK) == (pc // CHUNK)
    causal = jnp.logical_and(pair_same, pr >= pc)
    strict = jnp.logical_and(pair_same, pr > pc)
    eye = jnp.where(pr == pc, 1.0, 0.0).astype(F32)
    zeros_chunk = jnp.zeros((CHUNK, DN_HEAD_DIM), F32)

    def conv_silu(col):
        acc = None
        for j in range(CONV_K):
            rows = pl.ds(CONV_HALO - (CONV_K - 1) + j, tt)
            term = qkv_ext[rows, col * LANES:(col + 1) * LANES] * convw_ref[j:j + 1, col * LANES:(col + 1) * LANES]
            acc = term if acc is None else acc + term
        return _silu(acc)

    def l2n(v):
        return v * lax.rsqrt(jnp.sum(v * v, axis=-1, keepdims=True) + L2_EPS)

    n_pairs = tt // PAIR
    insts = []
    for h in range(N_DN_HEADS):
        q_all = l2n(conv_silu(h)) * (DN_HEAD_DIM ** -0.5)
        k_all = l2n(conv_silu(N_DN_HEADS + h))
        v_all = conv_silu(2 * N_DN_HEADS + h)
        la = A_LANE0 + h
        for d in range(n_pairs):
            r0 = d * PAIR
            q = q_all[r0:r0 + PAIR]
            k = k_all[r0:r0 + PAIR]
            v = v_all[r0:r0 + PAIR]
            bcol = beta[r0:r0 + PAIR, h:h + 1]
            gcol = gc[r0:r0 + PAIR, la:la + 1]
            egcol = eg[r0:r0 + PAIR, la:la + 1]
            ekdcol = ekd[r0:r0 + PAIR, la:la + 1]
            grow = gct[la:la + 1, r0:r0 + PAIR]
            diff = gcol - grow
            decay = jnp.where(causal, jnp.exp(jnp.where(causal, diff, 0.0)), 0.0)
            kb = k * bcol
            k16 = k.astype(BF16)
            a_mat = jnp.where(strict, _dot_nt(kb.astype(BF16), k16) * decay, 0.0)
            qk = jnp.where(causal, _dot_nt(q.astype(BF16), k16) * decay, 0.0).astype(BF16)
            insts.append(dict(
                h=h, d=d, a=a_mat, qk=qk,
                vb=(v * bcol).astype(BF16), kbg=(kb * egcol).astype(BF16),
                qd=(q * egcol).astype(BF16), kdt=(k * ekdcol).T.astype(BF16)))

    t_mats = [eye - it["a"] for it in insts]
    pws = [it["a"] for it in insts]
    for _ in range(CHUNK.bit_length() - 2):
        pws = [_dot_f32(pw, pw) for pw in pws]
        t_mats = [tm + _dot_f32(tm, pw) for tm, pw in zip(t_mats, pws)]
    for it, tm in zip(insts, t_mats):
        t16 = tm.astype(BF16)
        it["u"] = _dot(t16, it["vb"])
        it["w"] = _dot(t16, it["kbg"]).astype(BF16)

    s_cur = [state[h] for h in range(N_DN_HEADS)]
    o_rows = [[None] * (tt // CHUNK) for _ in range(N_DN_HEADS)]
    for d in range(n_pairs):
        for c in range(PAIR // CHUNK):
            c0 = c * CHUNK
            for h in range(N_DN_HEADS):
                it = insts[h * n_pairs + d]
                s = s_cur[h]
                s16 = s.astype(BF16)
                v_new = it["u"][c0:c0 + CHUNK] - _dot(it["w"][c0:c0 + CHUNK], s16)
                parts = [zeros_chunk] * (PAIR // CHUNK)
                parts[c] = v_new
                v_pad = jnp.concatenate(parts, axis=0).astype(BF16)
                o_rows[h][d * (PAIR // CHUNK) + c] = (
                    _dot(it["qd"][c0:c0 + CHUNK], s16) + _dot(it["qk"][c0:c0 + CHUNK], v_pad))
                row = d * PAIR + c0
                gt = egl[row:row + 1, A_LANE0 + h:A_LANE0 + h + 1]
                s_cur[h] = s * gt + _dot(it["kdt"], v_pad)
    for h in range(N_DN_HEADS):
        state[h] = s_cur[h]

    dn_parts = []
    for h in range(N_DN_HEADS):
        o = jnp.concatenate(o_rows[h], axis=0)
        zh = z[:, h * LANES:(h + 1) * LANES]
        dn_parts.append(((o * _rms_scale(o) * dnw_ref[...]) * _silu(zh)).astype(BF16))

    pos = (t * tt + lax.broadcasted_iota(jnp.int32, (tt, 1), 0) + 1).astype(F32)
    pool_parts = []
    for gi, win in enumerate(POOL_WINDOWS):
        cols = slice(gi * POOL_GROUP_DIM, (gi + 1) * POOL_GROUP_DIM)
        cur = p_ext[POOL_HALO:POOL_HALO + tt, cols]
        acc = cur
        for sft in range(1, win):
            acc = acc + p_ext[pl.ds(POOL_HALO - sft, tt), cols]
        mixed = acc / jnp.minimum(pos, float(win)) - cur
        y = _dot(mixed.astype(BF16), poolw_ref[gi]) * pscale_ref[:, cols]
        pool_parts.append(y.astype(BF16))

    mix = None
    for i, part in enumerate(dn_parts + pool_parts):
        term = _dot(part, wout_ref[i * LANES:(i + 1) * LANES, :])
        mix = term if mix is None else mix + term
    x1 = x + mix
    x1_ref[...] = x1

    qkv_ext[0:CONV_HALO, :] = qkv_ext[tt:tt + CONV_HALO, :]
    p_ext[0:POOL_HALO, :] = p_ext[tt:tt + POOL_HALO, :]

    hn = x1 * _rms_scale(x1) * fnw_ref[...]
    lane = lax.broadcasted_iota(jnp.int32, (tt, LANES), 1).astype(F32)
    logits = _dot_f32(hn, wr_ref[...]) + br_ref[...]
    work = jnp.where(lane < N_EXPERTS, logits, -jnp.inf)
    vals, idxs = [], []
    for _ in range(TOP_K):
        m = jnp.max(work, axis=-1, keepdims=True)
        idx = jnp.min(jnp.where(work == m, lane, float(LANES)), axis=-1, keepdims=True)
        vals.append(m)
        idxs.append(idx)
        work = jnp.where(lane == idx, -jnp.inf, work)
    exps = [jnp.exp(v - vals[0]) for v in vals]
    denom = exps[0] + exps[1] + exps[2] + exps[3]

    onehot = jnp.zeros((tt, LANES), F32)
    for idx in idxs:
        onehot = onehot + jnp.where(lane == idx, 1.0, 0.0)
    l_strict = jnp.where(ri > ci, 1.0, 0.0).astype(BF16)
    within = _dot(l_strict, onehot.astype(BF16))
    tile_cnt = jnp.sum(onehot, axis=0, keepdims=True)
    n_copies = jnp.floor((tile_cnt + float(COPY_ROWS - 1)) * (1.0 / COPY_ROWS))
    er = lax.broadcasted_iota(jnp.int32, (LANES, LANES), 0)
    ec = lax.broadcasted_iota(jnp.int32, (LANES, LANES), 1)
    before_lane = jnp.where(er < ec, 1.0, 0.0).astype(BF16)
    run_start = float(COPY_ROWS) * _dot(jnp.broadcast_to(n_copies, (SUBLANES, LANES)).astype(BF16), before_lane)[0:1]
    slot_of = run_start + within
    route = jnp.zeros((tt, LANES), F32)
    gates = jnp.zeros((tt, LANES), F32)
    for kk in range(TOP_K):
        slot = jnp.sum(jnp.where(lane == idxs[kk], slot_of, 0.0), axis=-1, keepdims=True)
        route = jnp.where(lane == float(kk), slot, route)
        gates = jnp.where(lane == float(kk), exps[kk] / denom, gates)
    route_ref[...] = route.astype(jnp.int32)
    gate_ref[...] = gates
    srow = lax.broadcasted_iota(jnp.int32, (SUBLANES, LANES), 0)
    meta_ref[...] = jnp.where(srow == 0, tile_cnt, jnp.where(srow == 1, cnt[0:1, :], 0.0))
    cnt[...] = cnt[...] + tile_cnt


def _mix_call(x, anw, win_r, convw, alog_p, dtb_p, dnw, poolw, pscale, wout, fnw, wr_p, br_p):
    bsz, seq, d = x.shape
    n_t = seq // TILE
    n_tok = bsz * seq
    const2 = lambda b, t: (0, 0)
    tok_map = lambda b, t: (b * n_t + t, 0)
    in_specs = [
        pl.BlockSpec((None, TILE, d), lambda b, t: (b, t, 0)),
        pl.BlockSpec(anw.shape, const2),
        pl.BlockSpec(win_r.shape, const2),
        pl.BlockSpec(convw.shape, const2),
        pl.BlockSpec(alog_p.shape, const2),
        pl.BlockSpec(dtb_p.shape, const2),
        pl.BlockSpec(dnw.shape, const2),
        pl.BlockSpec(poolw.shape, lambda b, t: (0, 0, 0)),
        pl.BlockSpec(pscale.shape, const2),
        pl.BlockSpec(wout.shape, const2),
        pl.BlockSpec(fnw.shape, const2),
        pl.BlockSpec(wr_p.shape, const2),
        pl.BlockSpec(br_p.shape, const2),
    ]
    out_shape = (
        jax.ShapeDtypeStruct((bsz, seq, d), F32),
        jax.ShapeDtypeStruct((n_tok, LANES), jnp.int32),
        jax.ShapeDtypeStruct((n_tok, LANES), F32),
        jax.ShapeDtypeStruct((bsz * n_t * SUBLANES, LANES), F32),
    )
    out_specs = (
        pl.BlockSpec((None, TILE, d), lambda b, t: (b, t, 0)),
        pl.BlockSpec((TILE, LANES), tok_map),
        pl.BlockSpec((TILE, LANES), tok_map),
        pl.BlockSpec((SUBLANES, LANES), tok_map),
    )
    scratch = [
        pltpu.VMEM((CONV_HALO + TILE, 3 * D_DN), F32),
        pltpu.VMEM((POOL_HALO + TILE, D_POOL), F32),
        pltpu.VMEM((N_DN_HEADS, DN_HEAD_DIM, DN_HEAD_DIM), F32),
        pltpu.VMEM((SUBLANES, LANES), F32),
    ]
    return pl.pallas_call(
        _mix_kernel,
        grid=(bsz, n_t),
        in_specs=in_specs,
        out_specs=out_specs,
        out_shape=out_shape,
        scratch_shapes=scratch,
        compiler_params=pltpu.CompilerParams(
            dimension_semantics=("arbitrary", "arbitrary"), vmem_limit_bytes=VMEM_LIMIT_BYTES),
        name="mix",
    )(x, anw, win_r, convw, alog_p, dtb_p, dnw, poolw, pscale, wout, fnw, wr_p, br_p)


COPY_SUBLANES = COPY_ROWS * ROW_SUBLANES


def _run_copies(ncopy_ref, base_ref, local_buf, global_ref, sem, to_global):
    def expert_body(e, first):
        n = ncopy_ref[e]
        base = base_ref[e] * ROW_SUBLANES

        def copy_body(j, carry):
            loc = local_buf.at[pl.ds(pl.multiple_of((first + j) * COPY_SUBLANES, COPY_SUBLANES), COPY_SUBLANES)]
            glo = global_ref.at[pl.ds(pl.multiple_of(base + j * COPY_SUBLANES, SUBLANES), COPY_SUBLANES)]
            if to_global:
                pltpu.make_async_copy(loc, glo, sem).start()
            else:
                pltpu.make_async_copy(glo, loc, sem).start()
            return carry
        lax.fori_loop(0, n, copy_body, 0)
        return first + n
    return lax.fori_loop(0, N_EXPERTS, expert_body, 0)


def _wait_copies(n, local_buf, global_ref, sem):
    def body(i, carry):
        pltpu.make_async_copy(local_buf.at[pl.ds(0, COPY_SUBLANES)], global_ref.at[pl.ds(0, COPY_SUBLANES)], sem).wait()
        return carry
    lax.fori_loop(0, n, body, 0)


def _dispatch_kernel(slot_ref, ncopy_ref, base_ref, x1_ref, fnw_ref, xs_in_ref, xs_ref, tok_tiles, sorted_buf, sem):
    del xs_in_ref
    x1 = x1_ref[...]
    hn = x1 * _rms_scale(x1) * fnw_ref[...]
    for s in range(ROW_SUBLANES):
        tok_tiles[pl.ds(s, TILE, stride=ROW_SUBLANES), :] = hn[:, s * LANES:(s + 1) * LANES]
    sorted_buf[...] = jnp.zeros(sorted_buf.shape, F32)

    def token_body(i, carry):
        tile = tok_tiles[pl.ds(pl.multiple_of(i * ROW_SUBLANES, ROW_SUBLANES), ROW_SUBLANES), :]
        for kk in range(TOP_K):
            slot = slot_ref[i * TOP_K + kk]
            sorted_buf[pl.ds(pl.multiple_of(slot * ROW_SUBLANES, ROW_SUBLANES), ROW_SUBLANES), :] = tile
        return carry
    lax.fori_loop(0, TILE, token_body, 0, unroll=8)

    n = _run_copies(ncopy_ref, base_ref, sorted_buf, xs_ref, sem, to_global=True)
    _wait_copies(n, sorted_buf, xs_ref, sem)


def _smem_spec(n):
    return pl.BlockSpec((n,), lambda i: (i,), memory_space=pltpu.SMEM)


def _dispatch_call(slots, ncopy, base, x1_flat, fnw, xs_zero):
    n_tok, d = x1_flat.shape
    return pl.pallas_call(
        _dispatch_kernel,
        grid=(n_tok // TILE,),
        in_specs=[
            _smem_spec(TILE * TOP_K), _smem_spec(LANES), _smem_spec(LANES),
            pl.BlockSpec((TILE, d), lambda i: (i, 0)),
            pl.BlockSpec(fnw.shape, lambda i: (0, 0)),
            pl.BlockSpec(memory_space=pl.ANY),
        ],
        out_specs=pl.BlockSpec(memory_space=pl.ANY),
        out_shape=jax.ShapeDtypeStruct(xs_zero.shape, F32),
        scratch_shapes=[pltpu.VMEM((TILE * ROW_SUBLANES, LANES), F32),
                        pltpu.VMEM((SLOTS * ROW_SUBLANES, LANES), F32),
                        pltpu.SemaphoreType.DMA(())],
        input_output_aliases={5: 0},
        compiler_params=pltpu.CompilerParams(dimension_semantics=("arbitrary",)),
        name="dispatch",
    )(slots, ncopy, base, x1_flat, fnw, xs_zero)


def _combine_kernel(slot_ref, gatev_ref, ncopy_ref, base_ref, x1_ref, fw_ref, ys_ref, out_ref,
                    sorted_buf, tok_tiles, sem):
    n = _run_copies(ncopy_ref, base_ref, sorted_buf, ys_ref, sem, to_global=False)
    _wait_copies(n, sorted_buf, ys_ref, sem)

    def token_body(i, carry):
        acc = None
        for kk in range(TOP_K):
            slot = slot_ref[i * TOP_K + kk]
            row = sorted_buf[pl.ds(pl.multiple_of(slot * ROW_SUBLANES, ROW_SUBLANES), ROW_SUBLANES), :]
            term = row * gatev_ref[i * TOP_K + kk]
            acc = term if acc is None else acc + term
        tok_tiles[pl.ds(pl.multiple_of(i * ROW_SUBLANES, ROW_SUBLANES), ROW_SUBLANES), :] = acc
        return carry
    lax.fori_loop(0, TILE, token_body, 0, unroll=8)

    x1 = x1_ref[...]
    cols = []
    ssq = None
    for s in range(ROW_SUBLANES):
        c = x1[:, s * LANES:(s + 1) * LANES] + tok_tiles[pl.ds(s, TILE, stride=ROW_SUBLANES), :]
        cols.append(c)
        sq = jnp.sum(c * c, axis=-1, keepdims=True)
        ssq = sq if ssq is None else ssq + sq
    scale = lax.rsqrt(ssq * (1.0 / D_MODEL) + RMS_EPS)
    for s in range(ROW_SUBLANES):
        out_ref[:, s * LANES:(s + 1) * LANES] = cols[s] * scale * fw_ref[:, s * LANES:(s + 1) * LANES]


def _combine_call(slots, gatev, ncopy, base, x1_flat, fw, ys):
    n_tok, d = x1_flat.shape
    return pl.pallas_call(
        _combine_kernel,
        grid=(n_tok // TILE,),
        in_specs=[
            _smem_spec(TILE * TOP_K), _smem_spec(TILE * TOP_K), _smem_spec(LANES), _smem_spec(LANES),
            pl.BlockSpec((TILE, d), lambda i: (i, 0)),
            pl.BlockSpec(fw.shape, lambda i: (0, 0)),
            pl.BlockSpec(memory_space=pl.ANY),
        ],
        out_specs=pl.BlockSpec((TILE, d), lambda i: (i, 0)),
        out_shape=jax.ShapeDtypeStruct((n_tok, d), F32),
        scratch_shapes=[pltpu.VMEM((SLOTS * ROW_SUBLANES, LANES), F32),
                        pltpu.VMEM((TILE * ROW_SUBLANES, LANES), F32),
                        pltpu.SemaphoreType.DMA(())],
        compiler_params=pltpu.CompilerParams(dimension_semantics=("arbitrary",)),
        name="combine",
    )(slots, gatev, ncopy, base, x1_flat, fw, ys)


def _moe_kernel(be_ref, nu_ref, xs_ref, wgu_ref, bgu_ref, wd_ref, bd_ref, ys_ref, wgu16, wd16):
    i = pl.program_id(0)

    @pl.when(i < nu_ref[0])
    def _():
        prev = be_ref[jnp.maximum(i - 1, 0)]
        new_expert = jnp.logical_or(i == 0, be_ref[i] != prev)

        @pl.when(new_expert)
        def _():
            wgu16[...] = wgu_ref[...].astype(BF16)
            wd16[...] = wd_ref[...].astype(BF16)

        xb = jnp.concatenate(
            [xs_ref[pl.ds(s, MOE_ROWS, stride=ROW_SUBLANES), :].astype(BF16) for s in range(ROW_SUBLANES)], axis=1)
        acc = None
        for j in range(D_EXPERT // FF_CHUNK):
            c0 = j * FF_CHUNK
            gate = _dot(xb, wgu16[:, c0:c0 + FF_CHUNK]) + bgu_ref[:, c0:c0 + FF_CHUNK]
            up = _dot(xb, wgu16[:, D_EXPERT + c0:D_EXPERT + c0 + FF_CHUNK]) + bgu_ref[:, D_EXPERT + c0:D_EXPERT + c0 + FF_CHUNK]
            gate = jnp.minimum(gate, SWIGLU_LIMIT)
            up = jnp.clip(up, -SWIGLU_LIMIT, SWIGLU_LIMIT)
            act = (up + 1.0) * gate * jax.nn.sigmoid(SWIGLU_ALPHA * gate)
            term = _dot(act.astype(BF16), wd16[c0:c0 + FF_CHUNK, :])
            acc = term if acc is None else acc + term
        y = acc + bd_ref[...]
        for s in range(ROW_SUBLANES):
            ys_ref[pl.ds(s, MOE_ROWS, stride=ROW_SUBLANES), :] = y[:, s * LANES:(s + 1) * LANES]

    @pl.when(i >= nu_ref[0])
    def _():
        ys_ref[...] = jnp.zeros(ys_ref.shape, F32)


def _moe_call(block_e, n_used, xs, wgu, bgu, wd, bd):
    n_rows = xs.shape[0] // ROW_SUBLANES
    d = D_MODEL
    n_blocks = n_rows // MOE_ROWS
    blk = lambda i, be, nu: (jnp.minimum(i, nu[0] - 1), 0)
    exp3 = lambda i, be, nu: (be[jnp.minimum(i, nu[0] - 1)], 0, 0)
    grid_spec = pltpu.PrefetchScalarGridSpec(
        num_scalar_prefetch=2,
        grid=(n_blocks,),
        in_specs=[
            pl.BlockSpec((MOE_ROWS * ROW_SUBLANES, LANES), blk),
            pl.BlockSpec((None, d, 2 * D_EXPERT), exp3),
            pl.BlockSpec((None, 1, 2 * D_EXPERT), exp3),
            pl.BlockSpec((None, D_EXPERT, d), exp3),
            pl.BlockSpec((None, 1, d), exp3),
        ],
        out_specs=pl.BlockSpec((MOE_ROWS * ROW_SUBLANES, LANES), lambda i, be, nu: (i, 0)),
        scratch_shapes=[pltpu.VMEM((d, 2 * D_EXPERT), BF16), pltpu.VMEM((D_EXPERT, d), BF16)],
    )
    return pl.pallas_call(
        _moe_kernel,
        grid_spec=grid_spec,
        out_shape=jax.ShapeDtypeStruct(xs.shape, F32),
        compiler_params=pltpu.CompilerParams(
            dimension_semantics=("arbitrary",), vmem_limit_bytes=VMEM_LIMIT_BYTES),
        name="moe",
    )(block_e, n_used, xs, wgu, bgu, wd, bd)


def _pad_lanes(v, lane0):
    out = jnp.zeros((1, LANES), F32)
    return lax.dynamic_update_slice(out, v.reshape(1, -1).astype(F32), (0, lane0))


def kernel(x, attn_norm_w, w_in, conv_w, a_log, dt_bias, dn_norm_w, pool_w, pool_scale, w_out, ffn_norm_w,
           w_router, b_router, w_gate_up, b_gate_up, w_down, b_down, final_norm_w):
    bsz, seq, d = x.shape
    n_tok = bsz * seq
    assert w_in.shape[0] == 1, "the combine kernel fuses the final norm, so exactly one layer is supported"
    assert d == D_MODEL and seq % TILE == 0
    l = 0

    s_z, s_a = 4 * D_DN, 4 * D_DN + 2 * N_DN_HEADS
    ba_cols = jnp.pad(w_in[l][:, s_z:s_a], ((0, 0), (0, LANES - 2 * N_DN_HEADS)))
    win_r = jnp.concatenate([w_in[l][:, :s_z], w_in[l][:, s_a:], ba_cols], axis=1).astype(BF16)
    wr_p = jnp.pad(w_router[l], ((0, 0), (0, LANES - N_EXPERTS)))
    br_p = _pad_lanes(b_router[l], 0)
    fnw = ffn_norm_w[l].reshape(1, d)

    x1, route, gates, meta = _mix_call(
        x, attn_norm_w[l].reshape(1, d), win_r, conv_w[l], _pad_lanes(a_log[l], A_LANE0),
        _pad_lanes(dt_bias[l], A_LANE0), dn_norm_w[l].reshape(1, DN_HEAD_DIM), pool_w[l].astype(BF16),
        pool_scale[l].reshape(1, D_POOL), w_out[l].astype(BF16), fnw, wr_p, br_p)

    n_tiles = n_tok // TILE
    meta = meta.reshape(n_tiles, SUBLANES, LANES)[:, :2, :N_EXPERTS].astype(jnp.int32)
    tile_cnt, tile_before = meta[:, 0], meta[:, 1]
    total = tile_cnt[-1] + tile_before[-1]
    padded = jnp.where(total > 0, ((total + COPY_ROWS - 1 + MOE_ROWS - 1) // MOE_ROWS) * MOE_ROWS, 0)
    pad_end = jnp.cumsum(padded)
    pad_start = pad_end - padded
    n_blocks = (n_tok * TOP_K + N_EXPERTS * (MOE_ROWS + COPY_ROWS - 2)) // MOE_ROWS + 1
    n_rows = n_blocks * MOE_ROWS
    lane_pad = ((0, 0), (0, LANES - N_EXPERTS))
    base = jnp.pad(pad_start[None, :] + tile_before, lane_pad).reshape(-1)
    ncopy = jnp.pad((tile_cnt + COPY_ROWS - 1) // COPY_ROWS, lane_pad).reshape(-1)
    block_start = jnp.arange(n_blocks, dtype=jnp.int32) * MOE_ROWS
    block_e = jnp.minimum(jnp.sum(block_start[:, None] >= pad_end[None, :], axis=1), N_EXPERTS - 1).astype(jnp.int32)
    n_used = (pad_end[-1:] // MOE_ROWS).astype(jnp.int32)
    slots = route[:, :TOP_K].reshape(-1)
    gatev = gates[:, :TOP_K].reshape(-1)

    x1_flat = x1.reshape(n_tok, d)
    xs = _dispatch_call(slots, ncopy, base, x1_flat, fnw, jnp.zeros((n_rows * ROW_SUBLANES, LANES), F32))
    ys = _moe_call(block_e, n_used, xs, w_gate_up[l], b_gate_up[l].reshape(N_EXPERTS, 1, -1),
                   w_down[l], b_down[l].reshape(N_EXPERTS, 1, -1))
    out = _combine_call(slots, gatev, ncopy, base, x1_flat, final_norm_w.reshape(1, d), ys)
    return out.reshape(bsz, seq, d)
```

```python
import jax
import jax.numpy as jnp
from jax import lax
from jax.experimental import pallas as pl
from jax.experimental.pallas import tpu as pltpu

F32 = jnp.float32
BF16 = jnp.bfloat16

D_MODEL = 1024
N_DN_HEADS = 4
DN_HEAD_DIM = 128
D_DN = N_DN_HEADS * DN_HEAD_DIM
CONV_K = 4
CHUNK = 64
POOL_WINDOWS = (2, 4, 8, 16)
N_POOL_GROUPS = len(POOL_WINDOWS)
D_POOL = D_MODEL - D_DN
POOL_GROUP_DIM = D_POOL // N_POOL_GROUPS
N_EXPERTS = 32
TOP_K = 4
D_EXPERT = D_MODEL
SWIGLU_LIMIT = 7.0
SWIGLU_ALPHA = 1.702
RMS_EPS = 1e-6
L2_EPS = 1e-6

LANES = 128
SUBLANES = 8
ROW_SUBLANES = D_MODEL // LANES
VMEM_LIMIT_BYTES = 56 * 1024 * 1024

TILE = 256
PAIR = 2 * CHUNK
CONV_HALO = SUBLANES
POOL_HALO = 2 * SUBLANES
MOE_ROWS = 512
FF_CHUNK = 512
COPY_ROWS = 8
ZERO_COPIES = 4
SLOTS = TILE * TOP_K + N_EXPERTS * COPY_ROWS
assert TILE * TOP_K // COPY_ROWS + 1 <= 256

COL_QKV = 0
COL_Z = 3 * D_DN
COL_P = 4 * D_DN
COL_BA = 4 * D_DN + D_POOL
D_IN_R = COL_BA + LANES
A_LANE0 = N_DN_HEADS


def _dot(a, b):
    return jnp.dot(a, b, preferred_element_type=F32)


def _dot_nt(a, b):
    return lax.dot_general(a, b, (((1,), (1,)), ((), ())), preferred_element_type=F32)


def _hi_lo(x):
    hi = x.astype(BF16)
    lo = (x - hi.astype(F32)).astype(BF16)
    return hi, lo


def _dot_f32(a, b):
    ah, al = _hi_lo(a)
    bh, bl = _hi_lo(b)
    return _dot(ah, bh) + (_dot(ah, bl) + _dot(al, bh))


def _dot_exact_lhs(m_bf16, x):
    hi = x.astype(BF16)
    r = x - hi.astype(F32)
    mid = r.astype(BF16)
    lo = (r - mid.astype(F32)).astype(BF16)
    return _dot(m_bf16, hi) + (_dot(m_bf16, mid) + _dot(m_bf16, lo))


def _rms_scale(x):
    return lax.rsqrt(jnp.mean(x * x, axis=-1, keepdims=True) + RMS_EPS)


def _silu(x):
    return x * jax.nn.sigmoid(x)


def _softplus(x):
    return jnp.maximum(x, 0.0) + jnp.log1p(jnp.exp(-jnp.abs(x)))


def _mix_kernel(x_ref, anw_ref, win_ref, convw_ref, alog_ref, dtb_ref, dnw_ref, poolw_ref, pscale_ref,
                wout_ref, fnw_ref, wr_ref, br_ref,
                x1_ref, route_ref, gate_ref, meta_ref, xs_ref,
                qkv_ext, p_ext, state, cnt, zero_buf, zero_sem):
    b = pl.program_id(0)
    t = pl.program_id(1)
    tt = TILE

    @pl.when(jnp.logical_and(b == 0, t == 0))
    def _():
        zero_buf[...] = jnp.zeros(zero_buf.shape, F32)
    fill_rows = zero_buf.shape[0]
    last_start = xs_ref.shape[0] - fill_rows
    step = b * pl.num_programs(1) + t

    def zero_copy(j):
        start = jnp.minimum((step * ZERO_COPIES + j) * fill_rows, last_start)
        return pltpu.make_async_copy(zero_buf, xs_ref.at[pl.ds(pl.multiple_of(start, SUBLANES), fill_rows)], zero_sem)
    for j in range(ZERO_COPIES):
        zero_copy(j).start()

    @pl.when(t == 0)
    def _():
        qkv_ext[0:CONV_HALO, :] = jnp.zeros((CONV_HALO, 3 * D_DN), F32)
        p_ext[0:POOL_HALO, :] = jnp.zeros((POOL_HALO, D_POOL), F32)
        state[...] = jnp.zeros(state.shape, F32)

    @pl.when(jnp.logical_and(b == 0, t == 0))
    def _():
        cnt[...] = jnp.zeros(cnt.shape, F32)

    x = x_ref[...]
    hb = (x * _rms_scale(x) * anw_ref[...]).astype(BF16)
    qkv_ext[CONV_HALO:CONV_HALO + tt, :] = _dot(hb, win_ref[:, COL_QKV:COL_Z])
    z = _dot(hb, win_ref[:, COL_Z:COL_P])
    p_ext[POOL_HALO:POOL_HALO + tt, :] = _dot(hb, win_ref[:, COL_P:COL_BA])
    ba = _dot(hb, win_ref[:, COL_BA:D_IN_R])

    beta = jax.nn.sigmoid(ba)
    g = -jnp.exp(alog_ref[...]) * _softplus(ba + dtb_ref[...])
    ri = lax.broadcasted_iota(jnp.int32, (tt, tt), 0)
    ci = lax.broadcasted_iota(jnp.int32, (tt, tt), 1)
    same_chunk = (ri // CHUNK) == (ci // CHUNK)
    l_incl = jnp.where(jnp.logical_and(same_chunk, ri >= ci), 1.0, 0.0).astype(BF16)
    l_all = jnp.where(same_chunk, 1.0, 0.0).astype(BF16)
    gc = _dot_exact_lhs(l_incl, g)
    gl = _dot_exact_lhs(l_all, g)
    eg = jnp.exp(gc)
    ekd = jnp.exp(gl - gc)
    egl = jnp.exp(gl)
    gct = gc.T

    pr = lax.broadcasted_iota(jnp.int32, (PAIR, PAIR), 0)
    pc = lax.broadcasted_iota(jnp.int32, (PAIR, PAIR), 1)
    pair_same = (pr // CHUNK) == (pc // CHUNK)
    causal = jnp.logical_and(pair_same, pr >= pc)
    strict = jnp.logical_and(pair_same, pr > pc)
    eye = jnp.where(pr == pc, 1.0, 0.0).astype(F32)
    zeros_chunk = jnp.zeros((CHUNK, DN_HEAD_DIM), F32)

    def conv_silu(col):
        acc = None
        for j in range(CONV_K):
            rows = pl.ds(CONV_HALO - (CONV_K - 1) + j, tt)
            term = qkv_ext[rows, col * LANES:(col + 1) * LANES] * convw_ref[j:j + 1, col * LANES:(col + 1) * LANES]
            acc = term if acc is None else acc + term
        return _silu(acc)

    def l2n(v):
        return v * lax.rsqrt(jnp.sum(v * v, axis=-1, keepdims=True) + L2_EPS)

    n_pairs = tt // PAIR
    insts = []
    for h in range(N_DN_HEADS):
        q_all = l2n(conv_silu(h)) * (DN_HEAD_DIM ** -0.5)
        k_all = l2n(conv_silu(N_DN_HEADS + h))
        v_all = conv_silu(2 * N_DN_HEADS + h)
        la = A_LANE0 + h
        for d in range(n_pairs):
            r0 = d * PAIR
            q = q_all[r0:r0 + PAIR]
            k = k_all[r0:r0 + PAIR]
            v = v_all[r0:r0 + PAIR]
            bcol = beta[r0:r0 + PAIR, h:h + 1]
            gcol = gc[r0:r0 + PAIR, la:la + 1]
            egcol = eg[r0:r0 + PAIR, la:la + 1]
            ekdcol = ekd[r0:r0 + PAIR, la:la + 1]
            grow = gct[la:la + 1, r0:r0 + PAIR]
            diff = gcol - grow
            decay = jnp.where(causal, jnp.exp(jnp.where(causal, diff, 0.0)), 0.0)
            kb = k * bcol
            k16 = k.astype(BF16)
            a_mat = jnp.where(strict, _dot_nt(kb.astype(BF16), k16) * decay, 0.0)
            qk = jnp.where(causal, _dot_nt(q.astype(BF16), k16) * decay, 0.0).astype(BF16)
            insts.append(dict(
                h=h, d=d, a=a_mat, qk=qk,
                rhs=jnp.concatenate([(v * bcol).astype(BF16), (kb * egcol).astype(BF16)], axis=1),
                qd=(q * egcol).astype(BF16), kdt=(k * ekdcol).T.astype(BF16)))

    rs = [-it["a"] for it in insts]
    p16 = [it["a"].astype(BF16) for it in insts]
    for _ in range(CHUNK.bit_length() - 2):
        pws = [_dot(p, p) for p in p16]
        p16 = [pw.astype(BF16) for pw in pws]
        rs = [r + pw + _dot(r.astype(BF16), p) for r, pw, p in zip(rs, pws, p16)]
    for it, r in zip(insts, rs):
        uw = _dot((eye + r).astype(BF16), it["rhs"])
        it["u"] = uw[:, :DN_HEAD_DIM]
        it["w"] = uw[:, DN_HEAD_DIM:].astype(BF16)

    s_cur = [state[h] for h in range(N_DN_HEADS)]
    o_rows = [[None] * (tt // CHUNK) for _ in range(N_DN_HEADS)]
    for d in range(n_pairs):
        for c in range(PAIR // CHUNK):
            c0 = c * CHUNK
            for h in range(N_DN_HEADS):
                it = insts[h * n_pairs + d]
                s = s_cur[h]
                s16 = s.astype(BF16)
                v_new = it["u"][c0:c0 + CHUNK] - _dot(it["w"][c0:c0 + CHUNK], s16)
                parts = [zeros_chunk] * (PAIR // CHUNK)
                parts[c] = v_new
                v_pad = jnp.concatenate(parts, axis=0).astype(BF16)
                o_rows[h][d * (PAIR // CHUNK) + c] = (
                    _dot(it["qd"][c0:c0 + CHUNK], s16) + _dot(it["qk"][c0:c0 + CHUNK], v_pad))
                row = d * PAIR + c0
                gt = egl[row:row + 1, A_LANE0 + h:A_LANE0 + h + 1]
                s_cur[h] = s * gt + _dot(it["kdt"], v_pad)
    for h in range(N_DN_HEADS):
        state[h] = s_cur[h]

    dn_parts = []
    for h in range(N_DN_HEADS):
        o = jnp.concatenate(o_rows[h], axis=0)
        zh = z[:, h * LANES:(h + 1) * LANES]
        dn_parts.append(((o * _rms_scale(o) * dnw_ref[...]) * _silu(zh)).astype(BF16))

    pos = (t * tt + lax.broadcasted_iota(jnp.int32, (tt, 1), 0) + 1).astype(F32)
    pool_parts = []
    for gi, win in enumerate(POOL_WINDOWS):
        cols = slice(gi * POOL_GROUP_DIM, (gi + 1) * POOL_GROUP_DIM)
        cur = p_ext[POOL_HALO:POOL_HALO + tt, cols]
        acc = cur
        for sft in range(1, win):
            acc = acc + p_ext[pl.ds(POOL_HALO - sft, tt), cols]
        mixed = acc / jnp.minimum(pos, float(win)) - cur
        y = _dot(mixed.astype(BF16), poolw_ref[gi]) * pscale_ref[:, cols]
        pool_parts.append(y.astype(BF16))

    mix = None
    for i, part in enumerate(dn_parts + pool_parts):
        term = _dot(part, wout_ref[i * LANES:(i + 1) * LANES, :])
        mix = term if mix is None else mix + term
    x1 = x + mix
    x1_ref[...] = x1

    qkv_ext[0:CONV_HALO, :] = qkv_ext[tt:tt + CONV_HALO, :]
    p_ext[0:POOL_HALO, :] = p_ext[tt:tt + POOL_HALO, :]

    hn = x1 * _rms_scale(x1) * fnw_ref[...]
    lane = lax.broadcasted_iota(jnp.int32, (tt, LANES), 1).astype(F32)
    logits = _dot_f32(hn, wr_ref[...]) + br_ref[...]
    work = jnp.where(lane < N_EXPERTS, logits, -jnp.inf)
    vals, idxs = [], []
    for _ in range(TOP_K):
        m = jnp.max(work, axis=-1, keepdims=True)
        idx = jnp.min(jnp.where(work == m, lane, float(LANES)), axis=-1, keepdims=True)
        vals.append(m)
        idxs.append(idx)
        work = jnp.where(lane == idx, -jnp.inf, work)
    exps = [jnp.exp(v - vals[0]) for v in vals]
    denom = exps[0] + exps[1] + exps[2] + exps[3]

    onehot = jnp.zeros((tt, LANES), F32)
    for idx in idxs:
        onehot = onehot + jnp.where(lane == idx, 1.0, 0.0)
    l_strict = jnp.where(ri > ci, 1.0, 0.0).astype(BF16)
    within = _dot(l_strict, onehot.astype(BF16))
    tile_cnt = jnp.sum(onehot, axis=0, keepdims=True)
    n_copies = jnp.floor((tile_cnt + float(COPY_ROWS - 1)) * (1.0 / COPY_ROWS))
    er = lax.broadcasted_iota(jnp.int32, (LANES, LANES), 0)
    ec = lax.broadcasted_iota(jnp.int32, (LANES, LANES), 1)
    before_lane = jnp.where(er < ec, 1.0, 0.0).astype(BF16)
    run_start = float(COPY_ROWS) * _dot(jnp.broadcast_to(n_copies, (SUBLANES, LANES)).astype(BF16), before_lane)[0:1]
    slot_of = run_start + within
    route = jnp.zeros((tt, LANES), F32)
    gates = jnp.zeros((tt, LANES), F32)
    for kk in range(TOP_K):
        slot = jnp.sum(jnp.where(lane == idxs[kk], slot_of, 0.0), axis=-1, keepdims=True)
        route = jnp.where(lane == float(kk), slot, route)
        gates = jnp.where(lane == float(kk), exps[kk] / denom, gates)
    route_ref[...] = route.astype(jnp.int32)
    gate_ref[...] = gates
    srow = lax.broadcasted_iota(jnp.int32, (SUBLANES, LANES), 0)
    meta_ref[...] = jnp.where(srow == 0, tile_cnt, jnp.where(srow == 1, cnt[0:1, :], 0.0))
    cnt[...] = cnt[...] + tile_cnt
    for j in range(ZERO_COPIES):
        zero_copy(j).wait()


def _mix_call(x, anw, win_r, convw, alog_p, dtb_p, dnw, poolw, pscale, wout, fnw, wr_p, br_p, xs_sublanes):
    bsz, seq, d = x.shape
    n_t = seq // TILE
    n_tok = bsz * seq
    fill_rows = -(-xs_sublanes // (bsz * n_t * ZERO_COPIES * SUBLANES)) * SUBLANES
    const2 = lambda b, t: (0, 0)
    tok_map = lambda b, t: (b * n_t + t, 0)
    in_specs = [
        pl.BlockSpec((None, TILE, d), lambda b, t: (b, t, 0)),
        pl.BlockSpec(anw.shape, const2),
        pl.BlockSpec(win_r.shape, const2),
        pl.BlockSpec(convw.shape, const2),
        pl.BlockSpec(alog_p.shape, const2),
        pl.BlockSpec(dtb_p.shape, const2),
        pl.BlockSpec(dnw.shape, const2),
        pl.BlockSpec(poolw.shape, lambda b, t: (0, 0, 0)),
        pl.BlockSpec(pscale.shape, const2),
        pl.BlockSpec(wout.shape, const2),
        pl.BlockSpec(fnw.shape, const2),
        pl.BlockSpec(wr_p.shape, const2),
        pl.BlockSpec(br_p.shape, const2),
    ]
    out_shape = (
        jax.ShapeDtypeStruct((bsz, seq, d), F32),
        jax.ShapeDtypeStruct((n_tok, LANES), jnp.int32),
        jax.ShapeDtypeStruct((n_tok, LANES), F32),
        jax.ShapeDtypeStruct((bsz * n_t * SUBLANES, LANES), F32),
        jax.ShapeDtypeStruct((xs_sublanes, LANES), F32),
    )
    out_specs = (
        pl.BlockSpec((None, TILE, d), lambda b, t: (b, t, 0)),
        pl.BlockSpec((TILE, LANES), tok_map),
        pl.BlockSpec((TILE, LANES), tok_map),
        pl.BlockSpec((SUBLANES, LANES), tok_map),
        pl.BlockSpec(memory_space=pl.ANY),
    )
    scratch = [
        pltpu.VMEM((CONV_HALO + TILE, 3 * D_DN), F32),
        pltpu.VMEM((POOL_HALO + TILE, D_POOL), F32),
        pltpu.VMEM((N_DN_HEADS, DN_HEAD_DIM, DN_HEAD_DIM), F32),
        pltpu.VMEM((SUBLANES, LANES), F32),
        pltpu.VMEM((fill_rows, LANES), F32),
        pltpu.SemaphoreType.DMA(()),
    ]
    return pl.pallas_call(
        _mix_kernel,
        grid=(bsz, n_t),
        in_specs=in_specs,
        out_specs=out_specs,
        out_shape=out_shape,
        scratch_shapes=scratch,
        compiler_params=pltpu.CompilerParams(
            dimension_semantics=("arbitrary", "arbitrary"), vmem_limit_bytes=VMEM_LIMIT_BYTES),
        name="mix",
    )(x, anw, win_r, convw, alog_p, dtb_p, dnw, poolw, pscale, wout, fnw, wr_p, br_p)


COPY_SUBLANES = COPY_ROWS * ROW_SUBLANES


def _run_copies(ncopy_ref, base_ref, local_buf, global_ref, sem, to_global):
    def expert_body(e, first):
        n = ncopy_ref[e]
        base = base_ref[e] * ROW_SUBLANES

        def copy_body(j, carry):
            loc = local_buf.at[pl.ds(pl.multiple_of((first + j) * COPY_SUBLANES, COPY_SUBLANES), COPY_SUBLANES)]
            glo = global_ref.at[pl.ds(pl.multiple_of(base + j * COPY_SUBLANES, SUBLANES), COPY_SUBLANES)]
            if to_global:
                pltpu.make_async_copy(loc, glo, sem).start()
            else:
                pltpu.make_async_copy(glo, loc, sem).start()
            return carry
        lax.fori_loop(0, n, copy_body, 0)
        return first + n
    return lax.fori_loop(0, N_EXPERTS, expert_body, 0)


def _wait_copies(n, local_buf, global_ref, sem):
    def body(i, carry):
        pltpu.make_async_copy(local_buf.at[pl.ds(0, COPY_SUBLANES)], global_ref.at[pl.ds(0, COPY_SUBLANES)], sem).wait()
        return carry
    lax.fori_loop(0, n, body, 0)


def _dispatch_kernel(slot_ref, ncopy_ref, base_ref, x1_ref, fnw_ref, xs_in_ref, xs_ref,
                     tok_tiles, sorted_buf, inflight, sem):
    del xs_in_ref
    step = pl.program_id(0)
    buf = sorted_buf.at[step % 2]
    x1 = x1_ref[...]
    hn = x1 * _rms_scale(x1) * fnw_ref[...]
    for s in range(ROW_SUBLANES):
        tok_tiles[pl.ds(s, TILE, stride=ROW_SUBLANES), :] = hn[:, s * LANES:(s + 1) * LANES]
    buf[...] = jnp.zeros(buf.shape, F32)

    def token_body(i, carry):
        tile = tok_tiles[pl.ds(pl.multiple_of(i * ROW_SUBLANES, ROW_SUBLANES), ROW_SUBLANES), :]
        for kk in range(TOP_K):
            slot = slot_ref[i * TOP_K + kk]
            buf[pl.ds(pl.multiple_of(slot * ROW_SUBLANES, ROW_SUBLANES), ROW_SUBLANES), :] = tile
        return carry
    lax.fori_loop(0, TILE, token_body, 0, unroll=8)

    @pl.when(step > 0)
    def _():
        _wait_copies(inflight[0], buf, xs_ref, sem)
    inflight[0] = _run_copies(ncopy_ref, base_ref, buf, xs_ref, sem, to_global=True)

    @pl.when(step == pl.num_programs(0) - 1)
    def _():
        _wait_copies(inflight[0], buf, xs_ref, sem)


def _smem_spec(n):
    return pl.BlockSpec((n,), lambda i: (i,), memory_space=pltpu.SMEM)


def _dispatch_call(slots, ncopy, base, x1_flat, fnw, xs_zero):
    n_tok, d = x1_flat.shape
    return pl.pallas_call(
        _dispatch_kernel,
        grid=(n_tok // TILE,),
        in_specs=[
            _smem_spec(TILE * TOP_K), _smem_spec(LANES), _smem_spec(LANES),
            pl.BlockSpec((TILE, d), lambda i: (i, 0)),
            pl.BlockSpec(fnw.shape, lambda i: (0, 0)),
            pl.BlockSpec(memory_space=pl.ANY),
        ],
        out_specs=pl.BlockSpec(memory_space=pl.ANY),
        out_shape=jax.ShapeDtypeStruct(xs_zero.shape, F32),
        scratch_shapes=[pltpu.VMEM((TILE * ROW_SUBLANES, LANES), F32),
                        pltpu.VMEM((2, SLOTS * ROW_SUBLANES, LANES), F32),
                        pltpu.SMEM((1,), jnp.int32),
                        pltpu.SemaphoreType.DMA(())],
        input_output_aliases={5: 0},
        compiler_params=pltpu.CompilerParams(dimension_semantics=("arbitrary",)),
        name="dispatch",
    )(slots, ncopy, base, x1_flat, fnw, xs_zero)


def _combine_kernel(slot_ref, gatev_ref, ncopy_ref, base_ref, ncopy_next_ref, base_next_ref, x1_ref, fw_ref, ys_ref,
                    out_ref, sorted_buf, tok_tiles, inflight, sems):
    step = pl.program_id(0)
    cur = step % 2
    nxt = 1 - cur
    buf = sorted_buf.at[cur]

    @pl.when(step == 0)
    def _():
        inflight[0] = _run_copies(ncopy_ref, base_ref, sorted_buf.at[0], ys_ref, sems.at[0], to_global=False)

    @pl.when(step + 1 < pl.num_programs(0))
    def _():
        inflight[nxt] = _run_copies(ncopy_next_ref, base_next_ref, sorted_buf.at[nxt], ys_ref, sems.at[nxt],
                                    to_global=False)
    _wait_copies(inflight[cur], buf, ys_ref, sems.at[cur])

    def token_body(i, carry):
        acc = None
        for kk in range(TOP_K):
            slot = slot_ref[i * TOP_K + kk]
            row = buf[pl.ds(pl.multiple_of(slot * ROW_SUBLANES, ROW_SUBLANES), ROW_SUBLANES), :]
            term = row * gatev_ref[i * TOP_K + kk]
            acc = term if acc is None else acc + term
        tok_tiles[pl.ds(pl.multiple_of(i * ROW_SUBLANES, ROW_SUBLANES), ROW_SUBLANES), :] = acc
        return carry
    lax.fori_loop(0, TILE, token_body, 0, unroll=8)

    x1 = x1_ref[...]
    cols = []
    ssq = None
    for s in range(ROW_SUBLANES):
        c = x1[:, s * LANES:(s + 1) * LANES] + tok_tiles[pl.ds(s, TILE, stride=ROW_SUBLANES), :]
        cols.append(c)
        sq = jnp.sum(c * c, axis=-1, keepdims=True)
        ssq = sq if ssq is None else ssq + sq
    scale = lax.rsqrt(ssq * (1.0 / D_MODEL) + RMS_EPS)
    for s in range(ROW_SUBLANES):
        out_ref[:, s * LANES:(s + 1) * LANES] = cols[s] * scale * fw_ref[:, s * LANES:(s + 1) * LANES]


def _combine_call(slots, gatev, ncopy, base, x1_flat, fw, ys):
    n_tok, d = x1_flat.shape
    n_steps = n_tok // TILE
    next_spec = pl.BlockSpec((LANES,), lambda i: (jnp.minimum(i + 1, n_steps - 1),), memory_space=pltpu.SMEM)
    return pl.pallas_call(
        _combine_kernel,
        grid=(n_steps,),
        in_specs=[
            _smem_spec(TILE * TOP_K), _smem_spec(TILE * TOP_K), _smem_spec(LANES), _smem_spec(LANES),
            next_spec, next_spec,
            pl.BlockSpec((TILE, d), lambda i: (i, 0)),
            pl.BlockSpec(fw.shape, lambda i: (0, 0)),
            pl.BlockSpec(memory_space=pl.ANY),
        ],
        out_specs=pl.BlockSpec((TILE, d), lambda i: (i, 0)),
        out_shape=jax.ShapeDtypeStruct((n_tok, d), F32),
        scratch_shapes=[pltpu.VMEM((2, SLOTS * ROW_SUBLANES, LANES), F32),
                        pltpu.VMEM((TILE * ROW_SUBLANES, LANES), F32),
                        pltpu.SMEM((2,), jnp.int32),
                        pltpu.SemaphoreType.DMA((2,))],
        compiler_params=pltpu.CompilerParams(dimension_semantics=("arbitrary",)),
        name="combine",
    )(slots, gatev, ncopy, base, ncopy, base, x1_flat, fw, ys)


def _moe_kernel(be_ref, nu_ref, xs_ref, wgu_ref, bgu_ref, wd_ref, bd_ref, ys_ref, wgu16, wd16):
    i = pl.program_id(0)

    @pl.when(i < nu_ref[0])
    def _():
        prev = be_ref[jnp.maximum(i - 1, 0)]
        new_expert = jnp.logical_or(i == 0, be_ref[i] != prev)

        @pl.when(new_expert)
        def _():
            wgu16[...] = wgu_ref[...].astype(BF16)
            wd16[...] = wd_ref[...].astype(BF16)

        xb = jnp.concatenate(
            [xs_ref[pl.ds(s, MOE_ROWS, stride=ROW_SUBLANES), :].astype(BF16) for s in range(ROW_SUBLANES)], axis=1)
        acc = None
        for j in range(D_EXPERT // FF_CHUNK):
            c0 = j * FF_CHUNK
            gate = _dot(xb, wgu16[:, c0:c0 + FF_CHUNK]) + bgu_ref[:, c0:c0 + FF_CHUNK]
            up = _dot(xb, wgu16[:, D_EXPERT + c0:D_EXPERT + c0 + FF_CHUNK]) + bgu_ref[:, D_EXPERT + c0:D_EXPERT + c0 + FF_CHUNK]
            gate = jnp.minimum(gate, SWIGLU_LIMIT)
            up = jnp.clip(up, -SWIGLU_LIMIT, SWIGLU_LIMIT)
            act = (up + 1.0) * gate * jax.nn.sigmoid(SWIGLU_ALPHA * gate)
            term = _dot(act.astype(BF16), wd16[c0:c0 + FF_CHUNK, :])
            acc = term if acc is None else acc + term
        y = acc + bd_ref[...]
        for s in range(ROW_SUBLANES):
            ys_ref[pl.ds(s, MOE_ROWS, stride=ROW_SUBLANES), :] = y[:, s * LANES:(s + 1) * LANES]

    @pl.when(i >= nu_ref[0])
    def _():
        ys_ref[...] = jnp.zeros(ys_ref.shape, F32)


def _moe_call(block_e, n_used, xs, wgu, bgu, wd, bd):
    n_rows = xs.shape[0] // ROW_SUBLANES
    d = D_MODEL
    n_blocks = n_rows // MOE_ROWS
    blk = lambda i, be, nu: (jnp.minimum(i, nu[0] - 1), 0)
    exp3 = lambda i, be, nu: (be[jnp.minimum(i, nu[0] - 1)], 0, 0)
    grid_spec = pltpu.PrefetchScalarGridSpec(
        num_scalar_prefetch=2,
        grid=(n_blocks,),
        in_specs=[
            pl.BlockSpec((MOE_ROWS * ROW_SUBLANES, LANES), blk),
            pl.BlockSpec((None, d, 2 * D_EXPERT), exp3),
            pl.BlockSpec((None, 1, 2 * D_EXPERT), exp3),
            pl.BlockSpec((None, D_EXPERT, d), exp3),
            pl.BlockSpec((None, 1, d), exp3),
        ],
        out_specs=pl.BlockSpec((MOE_ROWS * ROW_SUBLANES, LANES), lambda i, be, nu: (i, 0)),
        scratch_shapes=[pltpu.VMEM((d, 2 * D_EXPERT), BF16), pltpu.VMEM((D_EXPERT, d), BF16)],
    )
    return pl.pallas_call(
        _moe_kernel,
        grid_spec=grid_spec,
        out_shape=jax.ShapeDtypeStruct(xs.shape, F32),
        compiler_params=pltpu.CompilerParams(
            dimension_semantics=("arbitrary",), vmem_limit_bytes=VMEM_LIMIT_BYTES),
        name="moe",
    )(block_e, n_used, xs, wgu, bgu, wd, bd)


def _pad_lanes(v, lane0):
    out = jnp.zeros((1, LANES), F32)
    return lax.dynamic_update_slice(out, v.reshape(1, -1).astype(F32), (0, lane0))


def kernel(x, attn_norm_w, w_in, conv_w, a_log, dt_bias, dn_norm_w, pool_w, pool_scale, w_out, ffn_norm_w,
           w_router, b_router, w_gate_up, b_gate_up, w_down, b_down, final_norm_w):
    bsz, seq, d = x.shape
    n_tok = bsz * seq
    assert w_in.shape[0] == 1, "the combine kernel fuses the final norm, so exactly one layer is supported"
    assert d == D_MODEL and seq % TILE == 0
    l = 0

    s_z, s_a = 4 * D_DN, 4 * D_DN + 2 * N_DN_HEADS
    ba_cols = jnp.pad(w_in[l][:, s_z:s_a], ((0, 0), (0, LANES - 2 * N_DN_HEADS)))
    win_r = jnp.concatenate([w_in[l][:, :s_z], w_in[l][:, s_a:], ba_cols], axis=1).astype(BF16)
    wr_p = jnp.pad(w_router[l], ((0, 0), (0, LANES - N_EXPERTS)))
    br_p = _pad_lanes(b_router[l], 0)
    fnw = ffn_norm_w[l].reshape(1, d)

    n_blocks = (n_tok * TOP_K + N_EXPERTS * (MOE_ROWS + COPY_ROWS - 2)) // MOE_ROWS + 1
    n_rows = n_blocks * MOE_ROWS

    x1, route, gates, meta, xs_zero = _mix_call(
        x, attn_norm_w[l].reshape(1, d), win_r, conv_w[l], _pad_lanes(a_log[l], A_LANE0),
        _pad_lanes(dt_bias[l], A_LANE0), dn_norm_w[l].reshape(1, DN_HEAD_DIM), pool_w[l].astype(BF16),
        pool_scale[l].reshape(1, D_POOL), w_out[l].astype(BF16), fnw, wr_p, br_p, n_rows * ROW_SUBLANES)

    n_tiles = n_tok // TILE
    meta = meta.reshape(n_tiles, SUBLANES, LANES)[:, :2, :N_EXPERTS].astype(jnp.int32)
    tile_cnt, tile_before = meta[:, 0], meta[:, 1]
    total = tile_cnt[-1] + tile_before[-1]
    padded = jnp.where(total > 0, ((total + COPY_ROWS - 1 + MOE_ROWS - 1) // MOE_ROWS) * MOE_ROWS, 0)
    pad_end = jnp.cumsum(padded)
    pad_start = pad_end - padded
    lane_pad = ((0, 0), (0, LANES - N_EXPERTS))
    base = jnp.pad(pad_start[None, :] + tile_before, lane_pad).reshape(-1)
    ncopy = jnp.pad((tile_cnt + COPY_ROWS - 1) // COPY_ROWS, lane_pad).reshape(-1)
    block_start = jnp.arange(n_blocks, dtype=jnp.int32) * MOE_ROWS
    block_e = jnp.minimum(jnp.sum(block_start[:, None] >= pad_end[None, :], axis=1), N_EXPERTS - 1).astype(jnp.int32)
    n_used = (pad_end[-1:] // MOE_ROWS).astype(jnp.int32)
    slots = route[:, :TOP_K].reshape(-1)
    gatev = gates[:, :TOP_K].reshape(-1)

    x1_flat = x1.reshape(n_tok, d)
    xs = _dispatch_call(slots, ncopy, base, x1_flat, fnw, xs_zero)
    ys = _moe_call(block_e, n_used, xs, w_gate_up[l], b_gate_up[l].reshape(N_EXPERTS, 1, -1),
                   w_down[l], b_down[l].reshape(N_EXPERTS, 1, -1))
    out = _combine_call(slots, gatev, ncopy, base, x1_flat, final_norm_w.reshape(1, d), ys)
    return out.reshape(bsz, seq, d)
```

```python
import jax
import jax.numpy as jnp
from jax import lax
from jax.experimental import pallas as pl
from jax.experimental.pallas import tpu as pltpu

F32 = jnp.float32
BF16 = jnp.bfloat16

D_MODEL = 1024
N_DN_HEADS = 4
DN_HEAD_DIM = 128
D_DN = N_DN_HEADS * DN_HEAD_DIM
CONV_K = 4
CHUNK = 64
POOL_WINDOWS = (2, 4, 8, 16)
N_POOL_GROUPS = len(POOL_WINDOWS)
D_POOL = D_MODEL - D_DN
POOL_GROUP_DIM = D_POOL // N_POOL_GROUPS
N_EXPERTS = 32
TOP_K = 4
D_EXPERT = D_MODEL
SWIGLU_LIMIT = 7.0
SWIGLU_ALPHA = 1.702
RMS_EPS = 1e-6
L2_EPS = 1e-6

LANES = 128
SUBLANES = 8
ROW_SUBLANES = D_MODEL // LANES
VMEM_LIMIT_BYTES = 56 * 1024 * 1024

TILE = 256
PAIR = 2 * CHUNK
CONV_HALO = SUBLANES
POOL_HALO = 2 * SUBLANES
MOE_ROWS = 512
FF_CHUNK = 512
COPY_ROWS = 8
ZERO_COPIES = 4
SLOTS = TILE * TOP_K + N_EXPERTS * COPY_ROWS
assert TILE * TOP_K // COPY_ROWS + 1 <= 256

COL_QKV = 0
COL_Z = 3 * D_DN
COL_P = 4 * D_DN
COL_BA = 4 * D_DN + D_POOL
D_IN_R = COL_BA + LANES
A_LANE0 = N_DN_HEADS


def _dot(a, b):
    return jnp.dot(a, b, preferred_element_type=F32)


def _dot_nt(a, b):
    return lax.dot_general(a, b, (((1,), (1,)), ((), ())), preferred_element_type=F32)


def _hi_lo(x):
    hi = x.astype(BF16)
    lo = (x - hi.astype(F32)).astype(BF16)
    return hi, lo


def _dot_f32(a, b):
    ah, al = _hi_lo(a)
    bh, bl = _hi_lo(b)
    return _dot(ah, bh) + (_dot(ah, bl) + _dot(al, bh))


def _dot_exact_lhs(m_bf16, x):
    hi = x.astype(BF16)
    r = x - hi.astype(F32)
    mid = r.astype(BF16)
    lo = (r - mid.astype(F32)).astype(BF16)
    return _dot(m_bf16, hi) + (_dot(m_bf16, mid) + _dot(m_bf16, lo))


def _rms_scale(x):
    return lax.rsqrt(jnp.mean(x * x, axis=-1, keepdims=True) + RMS_EPS)


def _silu(x):
    return x * jax.nn.sigmoid(x)


def _softplus(x):
    return jnp.maximum(x, 0.0) + jnp.log1p(jnp.exp(-jnp.abs(x)))


def _mix_kernel(x_ref, anw_ref, win_ref, convw_ref, alog_ref, dtb_ref, dnw_ref, poolw_ref, pscale_ref,
                wout_ref, fnw_ref, wr_ref, br_ref,
                x1_ref, route_ref, gate_ref, meta_ref, xs_ref,
                qkv_ext, p_ext, state, cnt, zero_buf, zero_sem):
    b = pl.program_id(0)
    t = pl.program_id(1)
    tt = TILE

    @pl.when(jnp.logical_and(b == 0, t == 0))
    def _():
        zero_buf[...] = jnp.zeros(zero_buf.shape, F32)
    fill_rows = zero_buf.shape[0]
    last_start = xs_ref.shape[0] - fill_rows
    step = b * pl.num_programs(1) + t

    def zero_copy(j):
        start = jnp.minimum((step * ZERO_COPIES + j) * fill_rows, last_start)
        return pltpu.make_async_copy(zero_buf, xs_ref.at[pl.ds(pl.multiple_of(start, SUBLANES), fill_rows)], zero_sem)
    for j in range(ZERO_COPIES):
        zero_copy(j).start()

    @pl.when(t == 0)
    def _():
        qkv_ext[:, 0:CONV_HALO, :] = jnp.zeros((3 * N_DN_HEADS, CONV_HALO, LANES), F32)
        p_ext[:, 0:POOL_HALO, :] = jnp.zeros((N_POOL_GROUPS, POOL_HALO, LANES), F32)
        state[...] = jnp.zeros(state.shape, F32)

    @pl.when(jnp.logical_and(b == 0, t == 0))
    def _():
        cnt[...] = jnp.zeros(cnt.shape, F32)

    x = x_ref[...]
    hb = (x * _rms_scale(x) * anw_ref[...]).astype(BF16)
    qkv_pre = _dot(hb, win_ref[:, COL_QKV:COL_Z])
    for c in range(3 * N_DN_HEADS):
        qkv_ext[c, CONV_HALO:CONV_HALO + tt, :] = qkv_pre[:, c * LANES:(c + 1) * LANES]
    z = _dot(hb, win_ref[:, COL_Z:COL_P])
    p_pre = _dot(hb, win_ref[:, COL_P:COL_BA])
    for c in range(N_POOL_GROUPS):
        p_ext[c, POOL_HALO:POOL_HALO + tt, :] = p_pre[:, c * LANES:(c + 1) * LANES]
    ba = _dot(hb, win_ref[:, COL_BA:D_IN_R])

    beta = jax.nn.sigmoid(ba)
    g = -jnp.exp(alog_ref[...]) * _softplus(ba + dtb_ref[...])
    ri = lax.broadcasted_iota(jnp.int32, (tt, tt), 0)
    ci = lax.broadcasted_iota(jnp.int32, (tt, tt), 1)
    same_chunk = (ri // CHUNK) == (ci // CHUNK)
    l_incl = jnp.where(jnp.logical_and(same_chunk, ri >= ci), 1.0, 0.0).astype(BF16)
    gc = _dot_exact_lhs(l_incl, g)
    gl = jnp.concatenate(
        [jnp.broadcast_to(gc[c * CHUNK + CHUNK - 1:c * CHUNK + CHUNK, :], (CHUNK, LANES)) for c in range(tt // CHUNK)],
        axis=0)
    eg = jnp.exp(gc)
    ekd = jnp.exp(gl - gc)
    egl = jnp.exp(gl)
    gct = gc.T

    pr = lax.broadcasted_iota(jnp.int32, (PAIR, PAIR), 0)
    pc = lax.broadcasted_iota(jnp.int32, (PAIR, PAIR), 1)
    pair_same = (pr // CHUNK) == (pc // CHUNK)
    causal = jnp.logical_and(pair_same, pr >= pc)
    strict = jnp.logical_and(pair_same, pr > pc)
    eye = jnp.where(pr == pc, 1.0, 0.0).astype(F32)
    zeros_chunk = jnp.zeros((CHUNK, DN_HEAD_DIM), F32)

    def conv_silu(col):
        acc = None
        for j in range(CONV_K):
            rows = pl.ds(CONV_HALO - (CONV_K - 1) + j, tt)
            term = qkv_ext[col, rows, :] * convw_ref[j:j + 1, col * LANES:(col + 1) * LANES]
            acc = term if acc is None else acc + term
        return _silu(acc)

    def l2n(v):
        return v * lax.rsqrt(jnp.sum(v * v, axis=-1, keepdims=True) + L2_EPS)

    def block_diag(b0, b1):
        zero = jnp.zeros_like(b0)
        return jnp.concatenate([jnp.concatenate([b0, zero], axis=1), jnp.concatenate([zero, b1], axis=1)], axis=0)

    def dot_pair(a0, a1, b0, b1):
        out = _dot(jnp.concatenate([a0, a1], axis=1), block_diag(b0, b1))
        return out[:, :LANES], out[:, LANES:]

    n_pairs = tt // PAIR
    assert n_pairs == 2 and N_DN_HEADS % 2 == 0
    insts = []
    for h in range(N_DN_HEADS):
        q_all = l2n(conv_silu(h)) * (DN_HEAD_DIM ** -0.5)
        k_all = l2n(conv_silu(N_DN_HEADS + h))
        v_all = conv_silu(2 * N_DN_HEADS + h)
        la = A_LANE0 + h
        pre = []
        for d in range(n_pairs):
            r0 = d * PAIR
            q = q_all[r0:r0 + PAIR]
            k = k_all[r0:r0 + PAIR]
            v = v_all[r0:r0 + PAIR]
            bcol = beta[r0:r0 + PAIR, h:h + 1]
            gcol = gc[r0:r0 + PAIR, la:la + 1]
            egcol = eg[r0:r0 + PAIR, la:la + 1]
            ekdcol = ekd[r0:r0 + PAIR, la:la + 1]
            grow = gct[la:la + 1, r0:r0 + PAIR]
            diff = gcol - grow
            decay = jnp.where(causal, jnp.exp(jnp.where(causal, diff, 0.0)), 0.0)
            kb = k * bcol
            pre.append(dict(
                decay=decay, kq=jnp.concatenate([kb, q], axis=0).astype(BF16), kt=k.T.astype(BF16),
                rhs=jnp.concatenate([(v * bcol).astype(BF16), (kb * egcol).astype(BF16)], axis=1),
                qd=(q * egcol).astype(BF16), kdt=(k * ekdcol).T.astype(BF16)))
        kk0, kk1 = dot_pair(pre[0]["kq"], pre[1]["kq"], pre[0]["kt"], pre[1]["kt"])
        for d, kk in enumerate((kk0, kk1)):
            it = pre[d]
            it["a"] = jnp.where(strict, kk[:PAIR] * it["decay"], 0.0)
            it["qk"] = jnp.where(causal, kk[PAIR:] * it["decay"], 0.0).astype(BF16)
            insts.append(it)

    rs = [-it["a"] for it in insts]
    p16 = [it["a"].astype(BF16) for it in insts]
    for _ in range(CHUNK.bit_length() - 2):
        pws = []
        for i in range(0, len(insts), 2):
            pws.extend(dot_pair(p16[i], p16[i + 1], p16[i], p16[i + 1]))
        p16 = [pw.astype(BF16) for pw in pws]
        rps = []
        for i in range(0, len(insts), 2):
            rps.extend(dot_pair(rs[i].astype(BF16), rs[i + 1].astype(BF16), p16[i], p16[i + 1]))
        rs = [r + pw + rp for r, pw, rp in zip(rs, pws, rps)]
    for it, r in zip(insts, rs):
        uw = _dot((eye + r).astype(BF16), it["rhs"])
        it["u"] = uw[:, :DN_HEAD_DIM]
        it["wq"] = [jnp.concatenate([uw[c0:c0 + CHUNK, DN_HEAD_DIM:].astype(BF16), it["qd"][c0:c0 + CHUNK]], axis=0)
                    for c0 in range(0, PAIR, CHUNK)]
        it["qkk"] = [jnp.concatenate([it["qk"][c0:c0 + CHUNK], it["kdt"]], axis=0)
                     for c0 in range(0, PAIR, CHUNK)]

    s_cur = [state[h] for h in range(N_DN_HEADS)]
    o_rows = [[None] * (tt // CHUNK) for _ in range(N_DN_HEADS)]
    for d in range(n_pairs):
        for c in range(PAIR // CHUNK):
            c0 = c * CHUNK
            row = d * PAIR + c0
            for h0 in range(0, N_DN_HEADS, 2):
                its = [insts[h * n_pairs + d] for h in (h0, h0 + 1)]
                ss = [s_cur[h0], s_cur[h0 + 1]]
                ws = dot_pair(its[0]["wq"][c], its[1]["wq"][c], ss[0].astype(BF16), ss[1].astype(BF16))
                v_pads = []
                for it, wsq in zip(its, ws):
                    v_new = it["u"][c0:c0 + CHUNK] - wsq[:CHUNK]
                    parts = [zeros_chunk] * (PAIR // CHUNK)
                    parts[c] = v_new
                    v_pads.append(jnp.concatenate(parts, axis=0).astype(BF16))
                ov = dot_pair(its[0]["qkk"][c], its[1]["qkk"][c], v_pads[0], v_pads[1])
                for j, h in enumerate((h0, h0 + 1)):
                    o_rows[h][d * (PAIR // CHUNK) + c] = ws[j][CHUNK:] + ov[j][:CHUNK]
                    gt = egl[row:row + 1, A_LANE0 + h:A_LANE0 + h + 1]
                    s_cur[h] = ss[j] * gt + ov[j][CHUNK:]
    for h in range(N_DN_HEADS):
        state[h] = s_cur[h]

    dn_parts = []
    for h in range(N_DN_HEADS):
        o = jnp.concatenate(o_rows[h], axis=0)
        zh = z[:, h * LANES:(h + 1) * LANES]
        dn_parts.append(((o * _rms_scale(o) * dnw_ref[...]) * _silu(zh)).astype(BF16))

    pos = (t * tt + lax.broadcasted_iota(jnp.int32, (tt, 1), 0) + 1).astype(F32)
    pool_parts = []
    for gi, win in enumerate(POOL_WINDOWS):
        cols = slice(gi * POOL_GROUP_DIM, (gi + 1) * POOL_GROUP_DIM)
        cur = p_ext[gi, POOL_HALO:POOL_HALO + tt, :]
        acc = cur
        for sft in range(1, win):
            acc = acc + p_ext[gi, pl.ds(POOL_HALO - sft, tt), :]
        mixed = acc / jnp.minimum(pos, float(win)) - cur
        y = _dot(mixed.astype(BF16), poolw_ref[gi]) * pscale_ref[:, cols]
        pool_parts.append(y.astype(BF16))

    x1 = x + _dot(jnp.concatenate(dn_parts + pool_parts, axis=1), wout_ref[...])
    x1_ref[...] = x1

    qkv_ext[:, 0:CONV_HALO, :] = qkv_ext[:, tt:tt + CONV_HALO, :]
    p_ext[:, 0:POOL_HALO, :] = p_ext[:, tt:tt + POOL_HALO, :]

    hn = x1 * _rms_scale(x1) * fnw_ref[...]
    lane = lax.broadcasted_iota(jnp.int32, (tt, LANES), 1).astype(F32)
    logits = _dot_f32(hn, wr_ref[...]) + br_ref[...]
    work = jnp.where(lane < N_EXPERTS, logits, -jnp.inf)
    vals, idxs = [], []
    for _ in range(TOP_K):
        m = jnp.max(work, axis=-1, keepdims=True)
        idx = jnp.min(jnp.where(work == m, lane, float(LANES)), axis=-1, keepdims=True)
        vals.append(m)
        idxs.append(idx)
        work = jnp.where(lane == idx, -jnp.inf, work)
    exps = [jnp.exp(v - vals[0]) for v in vals]
    denom = exps[0] + exps[1] + exps[2] + exps[3]

    onehot = jnp.zeros((tt, LANES), F32)
    for idx in idxs:
        onehot = onehot + jnp.where(lane == idx, 1.0, 0.0)
    l_strict = jnp.where(ri > ci, 1.0, 0.0).astype(BF16)
    within = _dot(l_strict, onehot.astype(BF16))
    tile_cnt = jnp.sum(onehot, axis=0, keepdims=True)
    n_copies = jnp.floor((tile_cnt + float(COPY_ROWS - 1)) * (1.0 / COPY_ROWS))
    er = lax.broadcasted_iota(jnp.int32, (LANES, LANES), 0)
    ec = lax.broadcasted_iota(jnp.int32, (LANES, LANES), 1)
    before_lane = jnp.where(er < ec, 1.0, 0.0).astype(BF16)
    run_start = float(COPY_ROWS) * _dot(jnp.broadcast_to(n_copies, (SUBLANES, LANES)).astype(BF16), before_lane)[0:1]
    slot_of = run_start + within
    route = jnp.zeros((tt, LANES), F32)
    gates = jnp.zeros((tt, LANES), F32)
    for kk in range(TOP_K):
        slot = jnp.sum(jnp.where(lane == idxs[kk], slot_of, 0.0), axis=-1, keepdims=True)
        route = jnp.where(lane == float(kk), slot * float(ROW_SUBLANES), route)
        gates = jnp.where(lane == float(kk), exps[kk] / denom, gates)
    route_ref[...] = route.astype(jnp.int32)
    gate_ref[...] = gates
    srow = lax.broadcasted_iota(jnp.int32, (SUBLANES, LANES), 0)
    meta_ref[...] = jnp.where(srow == 0, tile_cnt, jnp.where(srow == 1, cnt[0:1, :], 0.0))
    cnt[...] = cnt[...] + tile_cnt
    for j in range(ZERO_COPIES):
        zero_copy(j).wait()


def _mix_call(x, anw, win_r, convw, alog_p, dtb_p, dnw, poolw, pscale, wout, fnw, wr_p, br_p, xs_sublanes):
    bsz, seq, d = x.shape
    n_t = seq // TILE
    n_tok = bsz * seq
    fill_rows = -(-xs_sublanes // (bsz * n_t * ZERO_COPIES * SUBLANES)) * SUBLANES
    const2 = lambda b, t: (0, 0)
    tok_map = lambda b, t: (b * n_t + t, 0)
    in_specs = [
        pl.BlockSpec((None, TILE, d), lambda b, t: (b, t, 0)),
        pl.BlockSpec(anw.shape, const2),
        pl.BlockSpec(win_r.shape, const2),
        pl.BlockSpec(convw.shape, const2),
        pl.BlockSpec(alog_p.shape, const2),
        pl.BlockSpec(dtb_p.shape, const2),
        pl.BlockSpec(dnw.shape, const2),
        pl.BlockSpec(poolw.shape, lambda b, t: (0, 0, 0)),
        pl.BlockSpec(pscale.shape, const2),
        pl.BlockSpec(wout.shape, const2),
        pl.BlockSpec(fnw.shape, const2),
        pl.BlockSpec(wr_p.shape, const2),
        pl.BlockSpec(br_p.shape, const2),
    ]
    out_shape = (
        jax.ShapeDtypeStruct((bsz, seq, d), F32),
        jax.ShapeDtypeStruct((n_tok, LANES), jnp.int32),
        jax.ShapeDtypeStruct((n_tok, LANES), F32),
        jax.ShapeDtypeStruct((bsz * n_t * SUBLANES, LANES), F32),
        jax.ShapeDtypeStruct((xs_sublanes, LANES), F32),
    )
    out_specs = (
        pl.BlockSpec((None, TILE, d), lambda b, t: (b, t, 0)),
        pl.BlockSpec((TILE, LANES), tok_map),
        pl.BlockSpec((TILE, LANES), tok_map),
        pl.BlockSpec((SUBLANES, LANES), tok_map),
        pl.BlockSpec(memory_space=pl.ANY),
    )
    scratch = [
        pltpu.VMEM((3 * N_DN_HEADS, CONV_HALO + TILE, LANES), F32),
        pltpu.VMEM((N_POOL_GROUPS, POOL_HALO + TILE, LANES), F32),
        pltpu.VMEM((N_DN_HEADS, DN_HEAD_DIM, DN_HEAD_DIM), F32),
        pltpu.VMEM((SUBLANES, LANES), F32),
        pltpu.VMEM((fill_rows, LANES), F32),
        pltpu.SemaphoreType.DMA(()),
    ]
    return pl.pallas_call(
        _mix_kernel,
        grid=(bsz, n_t),
        in_specs=in_specs,
        out_specs=out_specs,
        out_shape=out_shape,
        scratch_shapes=scratch,
        compiler_params=pltpu.CompilerParams(
            dimension_semantics=("arbitrary", "arbitrary"), vmem_limit_bytes=VMEM_LIMIT_BYTES),
        name="mix",
    )(x, anw, win_r, convw, alog_p, dtb_p, dnw, poolw, pscale, wout, fnw, wr_p, br_p)


COPY_SUBLANES = COPY_ROWS * ROW_SUBLANES


COPY_TABLE = 256
assert SLOTS // COPY_ROWS < COPY_TABLE
LOOP_UNROLL = 8


def _for_count(n, fn):
    def main(i, carry):
        for u in range(LOOP_UNROLL):
            fn(i * LOOP_UNROLL + u)
        return carry
    n_main = lax.shift_right_logical(n, LOOP_UNROLL.bit_length() - 1)
    lax.fori_loop(0, n_main, main, 0)

    def tail(i, carry):
        fn(i)
        return carry
    lax.fori_loop(n_main * LOOP_UNROLL, n, tail, 0)


def _run_copies(dst_ref, local_buf, global_ref, sem, to_global):
    n = dst_ref[COPY_TABLE - 1]

    def start(c):
        loc = local_buf.at[pl.ds(pl.multiple_of(c * COPY_SUBLANES, COPY_SUBLANES), COPY_SUBLANES)]
        glo = global_ref.at[pl.ds(pl.multiple_of(dst_ref[c], SUBLANES), COPY_SUBLANES)]
        if to_global:
            pltpu.make_async_copy(loc, glo, sem).start()
        else:
            pltpu.make_async_copy(glo, loc, sem).start()
    _for_count(n, start)
    return n


def _wait_copies(n, local_buf, global_ref, sem):
    def wait(i):
        del i
        pltpu.make_async_copy(local_buf.at[pl.ds(0, COPY_SUBLANES)], global_ref.at[pl.ds(0, COPY_SUBLANES)], sem).wait()
    _for_count(n, wait)


def _dispatch_kernel(slot_ref, dst_ref, x1_ref, fnw_ref, xs_in_ref, xs_ref,
                     tok_tiles, sorted_buf, inflight, sem):
    del xs_in_ref
    step = pl.program_id(0)
    buf = sorted_buf.at[step % 2]
    x1 = x1_ref[...]
    hn = x1 * _rms_scale(x1) * fnw_ref[...]
    for s in range(ROW_SUBLANES):
        tok_tiles[pl.ds(s, TILE, stride=ROW_SUBLANES), :] = hn[:, s * LANES:(s + 1) * LANES]
    buf[...] = jnp.zeros(buf.shape, F32)

    def token_body(i, carry):
        tile = tok_tiles[pl.ds(pl.multiple_of(i * ROW_SUBLANES, ROW_SUBLANES), ROW_SUBLANES), :]
        for kk in range(TOP_K):
            slot_row = slot_ref[i * TOP_K + kk]
            buf[pl.ds(pl.multiple_of(slot_row, ROW_SUBLANES), ROW_SUBLANES), :] = tile
        return carry
    lax.fori_loop(0, TILE, token_body, 0, unroll=8)

    @pl.when(step > 0)
    def _():
        _wait_copies(inflight[0], buf, xs_ref, sem)
    inflight[0] = _run_copies(dst_ref, buf, xs_ref, sem, to_global=True)

    @pl.when(step == pl.num_programs(0) - 1)
    def _():
        _wait_copies(inflight[0], buf, xs_ref, sem)


def _smem_spec(n):
    return pl.BlockSpec((n,), lambda i: (i,), memory_space=pltpu.SMEM)


def _dispatch_call(slots, dst, x1_flat, fnw, xs_zero):
    n_tok, d = x1_flat.shape
    return pl.pallas_call(
        _dispatch_kernel,
        grid=(n_tok // TILE,),
        in_specs=[
            _smem_spec(TILE * TOP_K), _smem_spec(COPY_TABLE),
            pl.BlockSpec((TILE, d), lambda i: (i, 0)),
            pl.BlockSpec(fnw.shape, lambda i: (0, 0)),
            pl.BlockSpec(memory_space=pl.ANY),
        ],
        out_specs=pl.BlockSpec(memory_space=pl.ANY),
        out_shape=jax.ShapeDtypeStruct(xs_zero.shape, F32),
        scratch_shapes=[pltpu.VMEM((TILE * ROW_SUBLANES, LANES), F32),
                        pltpu.VMEM((2, SLOTS * ROW_SUBLANES, LANES), F32),
                        pltpu.SMEM((1,), jnp.int32),
                        pltpu.SemaphoreType.DMA(())],
        input_output_aliases={4: 0},
        compiler_params=pltpu.CompilerParams(dimension_semantics=("arbitrary",)),
        name="dispatch",
    )(slots, dst, x1_flat, fnw, xs_zero)


def _combine_kernel(slot_ref, gatev_ref, dst_ref, dst_next_ref, x1_ref, fw_ref, ys_ref,
                    out_ref, sorted_buf, tok_tiles, inflight, sems):
    step = pl.program_id(0)
    cur = step % 2
    nxt = 1 - cur
    buf = sorted_buf.at[cur]

    @pl.when(step == 0)
    def _():
        inflight[0] = _run_copies(dst_ref, sorted_buf.at[0], ys_ref, sems.at[0], to_global=False)

    @pl.when(step + 1 < pl.num_programs(0))
    def _():
        inflight[nxt] = _run_copies(dst_next_ref, sorted_buf.at[nxt], ys_ref, sems.at[nxt], to_global=False)
    _wait_copies(inflight[cur], buf, ys_ref, sems.at[cur])

    def token_body(i, carry):
        acc = None
        for kk in range(TOP_K):
            slot_row = slot_ref[i * TOP_K + kk]
            row = buf[pl.ds(pl.multiple_of(slot_row, ROW_SUBLANES), ROW_SUBLANES), :]
            term = row * gatev_ref[i * TOP_K + kk]
            acc = term if acc is None else acc + term
        tok_tiles[pl.ds(pl.multiple_of(i * ROW_SUBLANES, ROW_SUBLANES), ROW_SUBLANES), :] = acc
        return carry
    lax.fori_loop(0, TILE, token_body, 0, unroll=8)

    x1 = x1_ref[...]
    cols = []
    ssq = None
    for s in range(ROW_SUBLANES):
        c = x1[:, s * LANES:(s + 1) * LANES] + tok_tiles[pl.ds(s, TILE, stride=ROW_SUBLANES), :]
        cols.append(c)
        sq = jnp.sum(c * c, axis=-1, keepdims=True)
        ssq = sq if ssq is None else ssq + sq
    scale = lax.rsqrt(ssq * (1.0 / D_MODEL) + RMS_EPS)
    for s in range(ROW_SUBLANES):
        out_ref[:, s * LANES:(s + 1) * LANES] = cols[s] * scale * fw_ref[:, s * LANES:(s + 1) * LANES]


def _combine_call(slots, gatev, dst, x1_flat, fw, ys):
    n_tok, d = x1_flat.shape
    n_steps = n_tok // TILE
    next_spec = pl.BlockSpec((COPY_TABLE,), lambda i: (jnp.minimum(i + 1, n_steps - 1),), memory_space=pltpu.SMEM)
    return pl.pallas_call(
        _combine_kernel,
        grid=(n_steps,),
        in_specs=[
            _smem_spec(TILE * TOP_K), _smem_spec(TILE * TOP_K), _smem_spec(COPY_TABLE), next_spec,
            pl.BlockSpec((TILE, d), lambda i: (i, 0)),
            pl.BlockSpec(fw.shape, lambda i: (0, 0)),
            pl.BlockSpec(memory_space=pl.ANY),
        ],
        out_specs=pl.BlockSpec((TILE, d), lambda i: (i, 0)),
        out_shape=jax.ShapeDtypeStruct((n_tok, d), F32),
        scratch_shapes=[pltpu.VMEM((2, SLOTS * ROW_SUBLANES, LANES), F32),
                        pltpu.VMEM((TILE * ROW_SUBLANES, LANES), F32),
                        pltpu.SMEM((2,), jnp.int32),
                        pltpu.SemaphoreType.DMA((2,))],
        compiler_params=pltpu.CompilerParams(dimension_semantics=("arbitrary",)),
        name="combine",
    )(slots, gatev, dst, dst, x1_flat, fw, ys)


def _moe_kernel(be_ref, nu_ref, xs_ref, wgu_ref, bgu_ref, wd_ref, bd_ref, ys_ref, wgu16, wd16):
    i = pl.program_id(0)

    @pl.when(i < nu_ref[0])
    def _():
        prev = be_ref[jnp.maximum(i - 1, 0)]
        new_expert = jnp.logical_or(i == 0, be_ref[i] != prev)

        @pl.when(new_expert)
        def _():
            wgu16[...] = wgu_ref[...].astype(BF16)
            wd16[...] = wd_ref[...].astype(BF16)

        xb = jnp.concatenate(
            [xs_ref[pl.ds(s, MOE_ROWS, stride=ROW_SUBLANES), :].astype(BF16) for s in range(ROW_SUBLANES)], axis=1)
        acc = None
        for j in range(D_EXPERT // FF_CHUNK):
            c0 = j * FF_CHUNK
            gate = _dot(xb, wgu16[:, c0:c0 + FF_CHUNK]) + bgu_ref[:, c0:c0 + FF_CHUNK]
            up = _dot(xb, wgu16[:, D_EXPERT + c0:D_EXPERT + c0 + FF_CHUNK]) + bgu_ref[:, D_EXPERT + c0:D_EXPERT + c0 + FF_CHUNK]
            gate = jnp.minimum(gate, SWIGLU_LIMIT)
            up = jnp.clip(up, -SWIGLU_LIMIT, SWIGLU_LIMIT)
            act = (up + 1.0) * gate * jax.nn.sigmoid(SWIGLU_ALPHA * gate)
            term = _dot(act.astype(BF16), wd16[c0:c0 + FF_CHUNK, :])
            acc = term if acc is None else acc + term
        y = acc + bd_ref[...]
        for s in range(ROW_SUBLANES):
            ys_ref[pl.ds(s, MOE_ROWS, stride=ROW_SUBLANES), :] = y[:, s * LANES:(s + 1) * LANES]

    @pl.when(i >= nu_ref[0])
    def _():
        ys_ref[...] = jnp.zeros(ys_ref.shape, F32)


def _moe_call(block_e, n_used, xs, wgu, bgu, wd, bd):
    n_rows = xs.shape[0] // ROW_SUBLANES
    d = D_MODEL
    n_blocks = n_rows // MOE_ROWS
    blk = lambda i, be, nu: (jnp.minimum(i, nu[0] - 1), 0)
    exp3 = lambda i, be, nu: (be[jnp.minimum(i, nu[0] - 1)], 0, 0)
    grid_spec = pltpu.PrefetchScalarGridSpec(
        num_scalar_prefetch=2,
        grid=(n_blocks,),
        in_specs=[
            pl.BlockSpec((MOE_ROWS * ROW_SUBLANES, LANES), blk),
            pl.BlockSpec((None, d, 2 * D_EXPERT), exp3),
            pl.BlockSpec((None, 1, 2 * D_EXPERT), exp3),
            pl.BlockSpec((None, D_EXPERT, d), exp3),
            pl.BlockSpec((None, 1, d), exp3),
        ],
        out_specs=pl.BlockSpec((MOE_ROWS * ROW_SUBLANES, LANES), lambda i, be, nu: (i, 0)),
        scratch_shapes=[pltpu.VMEM((d, 2 * D_EXPERT), BF16), pltpu.VMEM((D_EXPERT, d), BF16)],
    )
    return pl.pallas_call(
        _moe_kernel,
        grid_spec=grid_spec,
        out_shape=jax.ShapeDtypeStruct(xs.shape, F32),
        compiler_params=pltpu.CompilerParams(
            dimension_semantics=("arbitrary",), vmem_limit_bytes=VMEM_LIMIT_BYTES),
        name="moe",
    )(block_e, n_used, xs, wgu, bgu, wd, bd)


def _pad_lanes(v, lane0):
    out = jnp.zeros((1, LANES), F32)
    return lax.dynamic_update_slice(out, v.reshape(1, -1).astype(F32), (0, lane0))


def kernel(x, attn_norm_w, w_in, conv_w, a_log, dt_bias, dn_norm_w, pool_w, pool_scale, w_out, ffn_norm_w,
           w_router, b_router, w_gate_up, b_gate_up, w_down, b_down, final_norm_w):
    bsz, seq, d = x.shape
    n_tok = bsz * seq
    assert w_in.shape[0] == 1, "the combine kernel fuses the final norm, so exactly one layer is supported"
    assert d == D_MODEL and seq % TILE == 0
    l = 0

    s_z, s_a = 4 * D_DN, 4 * D_DN + 2 * N_DN_HEADS
    ba_cols = jnp.pad(w_in[l][:, s_z:s_a], ((0, 0), (0, LANES - 2 * N_DN_HEADS)))
    win_r = jnp.concatenate([w_in[l][:, :s_z], w_in[l][:, s_a:], ba_cols], axis=1).astype(BF16)
    wr_p = jnp.pad(w_router[l], ((0, 0), (0, LANES - N_EXPERTS)))
    br_p = _pad_lanes(b_router[l], 0)
    fnw = ffn_norm_w[l].reshape(1, d)

    n_blocks = (n_tok * TOP_K + N_EXPERTS * (MOE_ROWS + COPY_ROWS - 2)) // MOE_ROWS + 1
    n_rows = n_blocks * MOE_ROWS

    x1, route, gates, meta, xs_zero = _mix_call(
        x, attn_norm_w[l].reshape(1, d), win_r, conv_w[l], _pad_lanes(a_log[l], A_LANE0),
        _pad_lanes(dt_bias[l], A_LANE0), dn_norm_w[l].reshape(1, DN_HEAD_DIM), pool_w[l].astype(BF16),
        pool_scale[l].reshape(1, D_POOL), w_out[l].astype(BF16), fnw, wr_p, br_p, n_rows * ROW_SUBLANES)

    n_tiles = n_tok // TILE
    meta = meta.reshape(n_tiles, SUBLANES, LANES)[:, :2, :N_EXPERTS].astype(jnp.int32)
    tile_cnt, tile_before = meta[:, 0], meta[:, 1]
    total = tile_cnt[-1] + tile_before[-1]
    padded = jnp.where(total > 0, ((total + COPY_ROWS - 1 + MOE_ROWS - 1) // MOE_ROWS) * MOE_ROWS, 0)
    pad_end = jnp.cumsum(padded)
    pad_start = pad_end - padded
    run_base = pad_start[None, :] + tile_before
    ncopy = (tile_cnt + COPY_ROWS - 1) // COPY_ROWS
    copy_end = jnp.cumsum(ncopy, axis=1)
    copy_start = copy_end - ncopy
    c_idx = jnp.arange(COPY_TABLE, dtype=jnp.int32)[None, :, None]
    in_run = jnp.logical_and(c_idx >= copy_start[:, None, :], c_idx < copy_end[:, None, :])
    dst_row = jnp.sum(jnp.where(in_run, run_base[:, None, :] + COPY_ROWS * (c_idx - copy_start[:, None, :]), 0), axis=2)
    dst = (dst_row * ROW_SUBLANES).at[:, COPY_TABLE - 1].set(copy_end[:, -1]).reshape(-1)
    block_start = jnp.arange(n_blocks, dtype=jnp.int32) * MOE_ROWS
    block_e = jnp.minimum(jnp.sum(block_start[:, None] >= pad_end[None, :], axis=1), N_EXPERTS - 1).astype(jnp.int32)
    n_used = (pad_end[-1:] // MOE_ROWS).astype(jnp.int32)
    slots = route[:, :TOP_K].reshape(-1)
    gatev = gates[:, :TOP_K].reshape(-1)

    x1_flat = x1.reshape(n_tok, d)
    xs = _dispatch_call(slots, dst, x1_flat, fnw, xs_zero)
    ys = _moe_call(block_e, n_used, xs, w_gate_up[l], b_gate_up[l].reshape(N_EXPERTS, 1, -1),
                   w_down[l], b_down[l].reshape(N_EXPERTS, 1, -1))
    out = _combine_call(slots, gatev, dst, x1_flat, final_norm_w.reshape(1, d), ys)
    return out.reshape(bsz, seq, d)
```

```python
import jax
import jax.numpy as jnp
from jax import lax
from jax.experimental import pallas as pl
from jax.experimental.pallas import tpu as pltpu

F32 = jnp.float32
BF16 = jnp.bfloat16

D_MODEL = 1024
N_DN_HEADS = 4
DN_HEAD_DIM = 128
D_DN = N_DN_HEADS * DN_HEAD_DIM
CONV_K = 4
CHUNK = 64
POOL_WINDOWS = (2, 4, 8, 16)
N_POOL_GROUPS = len(POOL_WINDOWS)
D_POOL = D_MODEL - D_DN
POOL_GROUP_DIM = D_POOL // N_POOL_GROUPS
N_EXPERTS = 32
TOP_K = 4
D_EXPERT = D_MODEL
SWIGLU_LIMIT = 7.0
SWIGLU_ALPHA = 1.702
RMS_EPS = 1e-6
L2_EPS = 1e-6

LANES = 128
SUBLANES = 8
ROW_SUBLANES = D_MODEL // LANES
VMEM_LIMIT_BYTES = 56 * 1024 * 1024

TILE = 256
PAIR = 2 * CHUNK
CONV_HALO = SUBLANES
POOL_HALO = 2 * SUBLANES
MOE_ROWS = 512
FF_CHUNK = 512
COPY_ROWS = 8
ZERO_COPIES = 4
SLOTS = TILE * TOP_K + N_EXPERTS * COPY_ROWS
assert TILE * TOP_K // COPY_ROWS + 1 <= 256

COL_QKV = 0
COL_Z = 3 * D_DN
COL_P = 4 * D_DN
COL_BA = 4 * D_DN + D_POOL
D_IN_R = COL_BA + LANES
A_LANE0 = N_DN_HEADS


def _dot(a, b):
    return jnp.dot(a, b, preferred_element_type=F32)


def _dot_nt(a, b):
    return lax.dot_general(a, b, (((1,), (1,)), ((), ())), preferred_element_type=F32)


def _hi_lo(x):
    hi = x.astype(BF16)
    lo = (x - hi.astype(F32)).astype(BF16)
    return hi, lo


def _dot_f32(a, b):
    ah, al = _hi_lo(a)
    bh, bl = _hi_lo(b)
    return _dot(ah, bh) + (_dot(ah, bl) + _dot(al, bh))


def _dot_exact_lhs(m_bf16, x):
    hi = x.astype(BF16)
    r = x - hi.astype(F32)
    mid = r.astype(BF16)
    lo = (r - mid.astype(F32)).astype(BF16)
    return _dot(m_bf16, hi) + (_dot(m_bf16, mid) + _dot(m_bf16, lo))


def _rms_scale(x):
    return lax.rsqrt(jnp.mean(x * x, axis=-1, keepdims=True) + RMS_EPS)


def _silu(x):
    return x * jax.nn.sigmoid(x)


def _softplus(x):
    return jnp.maximum(x, 0.0) + jnp.log1p(jnp.exp(-jnp.abs(x)))


def _mix_kernel(x_ref, anw_ref, win_ref, convw_ref, alog_ref, dtb_ref, dnw_ref, poolw_ref, pscale_ref,
                wout_ref, fnw_ref, wr_ref, br_ref,
                x1_ref, route_ref, gate_ref, meta_ref, xs_ref,
                qkv_ext, p_ext, state, cnt, zero_buf, zero_sem):
    b = pl.program_id(0)
    t = pl.program_id(1)
    tt = TILE

    @pl.when(jnp.logical_and(b == 0, t == 0))
    def _():
        zero_buf[...] = jnp.zeros(zero_buf.shape, F32)
    fill_rows = zero_buf.shape[0]
    last_start = xs_ref.shape[0] - fill_rows
    step = b * pl.num_programs(1) + t

    def zero_copy(j):
        start = jnp.minimum((step * ZERO_COPIES + j) * fill_rows, last_start)
        return pltpu.make_async_copy(zero_buf, xs_ref.at[pl.ds(pl.multiple_of(start, SUBLANES), fill_rows)], zero_sem)
    for j in range(ZERO_COPIES):
        zero_copy(j).start()

    @pl.when(t == 0)
    def _():
        qkv_ext[:, 0:CONV_HALO, :] = jnp.zeros((3 * N_DN_HEADS, CONV_HALO, LANES), F32)
        p_ext[:, 0:POOL_HALO, :] = jnp.zeros((N_POOL_GROUPS, POOL_HALO, LANES), F32)
        state[...] = jnp.zeros(state.shape, F32)

    @pl.when(jnp.logical_and(b == 0, t == 0))
    def _():
        cnt[...] = jnp.zeros(cnt.shape, F32)

    x = x_ref[...]
    hb = (x * _rms_scale(x) * anw_ref[...]).astype(BF16)
    qkv_pre = _dot(hb, win_ref[:, COL_QKV:COL_Z])
    for c in range(3 * N_DN_HEADS):
        qkv_ext[c, CONV_HALO:CONV_HALO + tt, :] = qkv_pre[:, c * LANES:(c + 1) * LANES]
    z = _dot(hb, win_ref[:, COL_Z:COL_P])
    p_pre = _dot(hb, win_ref[:, COL_P:COL_BA])
    for c in range(N_POOL_GROUPS):
        p_ext[c, POOL_HALO:POOL_HALO + tt, :] = p_pre[:, c * LANES:(c + 1) * LANES]
    ba = _dot(hb, win_ref[:, COL_BA:D_IN_R])

    beta = jax.nn.sigmoid(ba)
    g = -jnp.exp(alog_ref[...]) * _softplus(ba + dtb_ref[...])
    ri = lax.broadcasted_iota(jnp.int32, (tt, tt), 0)
    ci = lax.broadcasted_iota(jnp.int32, (tt, tt), 1)
    same_chunk = (ri // CHUNK) == (ci // CHUNK)
    l_incl = jnp.where(jnp.logical_and(same_chunk, ri >= ci), 1.0, 0.0).astype(BF16)
    gc = _dot_exact_lhs(l_incl, g)
    gl = jnp.concatenate(
        [jnp.broadcast_to(gc[c * CHUNK + CHUNK - 1:c * CHUNK + CHUNK, :], (CHUNK, LANES)) for c in range(tt // CHUNK)],
        axis=0)
    eg = jnp.exp(gc)
    ekd = jnp.exp(gl - gc)
    egl = jnp.exp(gl)
    gct = gc.T

    pr = lax.broadcasted_iota(jnp.int32, (PAIR, PAIR), 0)
    pc = lax.broadcasted_iota(jnp.int32, (PAIR, PAIR), 1)
    pair_same = (pr // CHUNK) == (pc // CHUNK)
    causal = jnp.logical_and(pair_same, pr >= pc)
    strict = jnp.logical_and(pair_same, pr > pc)
    eye = jnp.where(pr == pc, 1.0, 0.0).astype(F32)
    zeros_chunk = jnp.zeros((CHUNK, DN_HEAD_DIM), F32)

    def conv_silu(col):
        acc = None
        for j in range(CONV_K):
            rows = pl.ds(CONV_HALO - (CONV_K - 1) + j, tt)
            term = qkv_ext[col, rows, :] * convw_ref[j:j + 1, col * LANES:(col + 1) * LANES]
            acc = term if acc is None else acc + term
        return _silu(acc)

    def l2n(v):
        return v * lax.rsqrt(jnp.sum(v * v, axis=-1, keepdims=True) + L2_EPS)

    def block_diag(b0, b1):
        zero = jnp.zeros_like(b0)
        return jnp.concatenate([jnp.concatenate([b0, zero], axis=1), jnp.concatenate([zero, b1], axis=1)], axis=0)

    def dot_pair(a0, a1, b0, b1):
        out = _dot(jnp.concatenate([a0, a1], axis=1), block_diag(b0, b1))
        return out[:, :LANES], out[:, LANES:]

    n_pairs = tt // PAIR
    assert n_pairs == 2 and N_DN_HEADS % 2 == 0
    insts = []
    for h in range(N_DN_HEADS):
        q_all = l2n(conv_silu(h)) * (DN_HEAD_DIM ** -0.5)
        k_all = l2n(conv_silu(N_DN_HEADS + h))
        v_all = conv_silu(2 * N_DN_HEADS + h)
        la = A_LANE0 + h
        pre = []
        for d in range(n_pairs):
            r0 = d * PAIR
            q = q_all[r0:r0 + PAIR]
            k = k_all[r0:r0 + PAIR]
            v = v_all[r0:r0 + PAIR]
            bcol = beta[r0:r0 + PAIR, h:h + 1]
            gcol = gc[r0:r0 + PAIR, la:la + 1]
            egcol = eg[r0:r0 + PAIR, la:la + 1]
            ekdcol = ekd[r0:r0 + PAIR, la:la + 1]
            grow = gct[la:la + 1, r0:r0 + PAIR]
            diff = gcol - grow
            decay = jnp.where(causal, jnp.exp(jnp.where(causal, diff, 0.0)), 0.0)
            kb = k * bcol
            pre.append(dict(
                decay=decay, kq=jnp.concatenate([kb, q], axis=0).astype(BF16), kt=k.T.astype(BF16),
                rhs=jnp.concatenate([(v * bcol).astype(BF16), (kb * egcol).astype(BF16)], axis=1),
                qd=(q * egcol).astype(BF16), kdt=(k * ekdcol).T.astype(BF16)))
        kk0, kk1 = dot_pair(pre[0]["kq"], pre[1]["kq"], pre[0]["kt"], pre[1]["kt"])
        for d, kk in enumerate((kk0, kk1)):
            it = pre[d]
            it["a"] = jnp.where(strict, kk[:PAIR] * it["decay"], 0.0)
            it["qk"] = jnp.where(causal, kk[PAIR:] * it["decay"], 0.0).astype(BF16)
            insts.append(it)

    rs = [-it["a"] for it in insts]
    p16 = [it["a"].astype(BF16) for it in insts]
    for _ in range(CHUNK.bit_length() - 2):
        pws = []
        for i in range(0, len(insts), 2):
            pws.extend(dot_pair(p16[i], p16[i + 1], p16[i], p16[i + 1]))
        p16 = [pw.astype(BF16) for pw in pws]
        rps = []
        for i in range(0, len(insts), 2):
            rps.extend(dot_pair(rs[i].astype(BF16), rs[i + 1].astype(BF16), p16[i], p16[i + 1]))
        rs = [r + pw + rp for r, pw, rp in zip(rs, pws, rps)]
    for it, r in zip(insts, rs):
        uw = _dot((eye + r).astype(BF16), it["rhs"])
        it["u"] = uw[:, :DN_HEAD_DIM]
        it["wq"] = [jnp.concatenate([uw[c0:c0 + CHUNK, DN_HEAD_DIM:].astype(BF16), it["qd"][c0:c0 + CHUNK]], axis=0)
                    for c0 in range(0, PAIR, CHUNK)]
        it["qkk"] = [jnp.concatenate([it["qk"][c0:c0 + CHUNK], it["kdt"]], axis=0)
                     for c0 in range(0, PAIR, CHUNK)]

    s_cur = [state[h] for h in range(N_DN_HEADS)]
    o_rows = [[None] * (tt // CHUNK) for _ in range(N_DN_HEADS)]
    for d in range(n_pairs):
        for c in range(PAIR // CHUNK):
            c0 = c * CHUNK
            row = d * PAIR + c0
            for h0 in range(0, N_DN_HEADS, 2):
                its = [insts[h * n_pairs + d] for h in (h0, h0 + 1)]
                ss = [s_cur[h0], s_cur[h0 + 1]]
                ws = dot_pair(its[0]["wq"][c], its[1]["wq"][c], ss[0].astype(BF16), ss[1].astype(BF16))
                v_pads = []
                for it, wsq in zip(its, ws):
                    v_new = it["u"][c0:c0 + CHUNK] - wsq[:CHUNK]
                    parts = [zeros_chunk] * (PAIR // CHUNK)
                    parts[c] = v_new
                    v_pads.append(jnp.concatenate(parts, axis=0).astype(BF16))
                ov = dot_pair(its[0]["qkk"][c], its[1]["qkk"][c], v_pads[0], v_pads[1])
                for j, h in enumerate((h0, h0 + 1)):
                    o_rows[h][d * (PAIR // CHUNK) + c] = ws[j][CHUNK:] + ov[j][:CHUNK]
                    gt = egl[row:row + 1, A_LANE0 + h:A_LANE0 + h + 1]
                    s_cur[h] = ss[j] * gt + ov[j][CHUNK:]
    for h in range(N_DN_HEADS):
        state[h] = s_cur[h]

    dn_parts = []
    for h in range(N_DN_HEADS):
        o = jnp.concatenate(o_rows[h], axis=0)
        zh = z[:, h * LANES:(h + 1) * LANES]
        dn_parts.append(((o * _rms_scale(o) * dnw_ref[...]) * _silu(zh)).astype(BF16))

    pos = (t * tt + lax.broadcasted_iota(jnp.int32, (tt, 1), 0) + 1).astype(F32)
    pool_parts = []
    for gi, win in enumerate(POOL_WINDOWS):
        cols = slice(gi * POOL_GROUP_DIM, (gi + 1) * POOL_GROUP_DIM)
        cur = p_ext[gi, POOL_HALO:POOL_HALO + tt, :]
        acc = cur
        for sft in range(1, win):
            acc = acc + p_ext[gi, pl.ds(POOL_HALO - sft, tt), :]
        mixed = acc / jnp.minimum(pos, float(win)) - cur
        y = _dot(mixed.astype(BF16), poolw_ref[gi]) * pscale_ref[:, cols]
        pool_parts.append(y.astype(BF16))

    x1 = x + _dot(jnp.concatenate(dn_parts + pool_parts, axis=1), wout_ref[...])
    x1_ref[...] = x1

    qkv_ext[:, 0:CONV_HALO, :] = qkv_ext[:, tt:tt + CONV_HALO, :]
    p_ext[:, 0:POOL_HALO, :] = p_ext[:, tt:tt + POOL_HALO, :]

    hn = x1 * _rms_scale(x1) * fnw_ref[...]
    hn_hi, hn_lo = _hi_lo(hn)
    hh = _dot(hn_hi, wr_ref[...])
    logits = hh[:, :LANES] + (hh[:, LANES:] + _dot(hn_lo, wr_ref[:, :LANES])) + br_ref[...]
    lt = logits.T[:N_EXPERTS]
    eidx = lax.broadcasted_iota(jnp.int32, (N_EXPERTS, tt), 0).astype(F32)
    work = lt
    vals, idxs = [], []
    for _ in range(TOP_K):
        m = jnp.max(work, axis=0, keepdims=True)
        idx = jnp.min(jnp.where(work == m, eidx, float(LANES)), axis=0, keepdims=True)
        vals.append(m)
        idxs.append(idx)
        work = jnp.where(eidx == idx, -jnp.inf, work)
    exps = [jnp.exp(v - vals[0]) for v in vals]
    denom = exps[0] + exps[1] + exps[2] + exps[3]

    onehot = jnp.zeros((N_EXPERTS, tt), F32)
    for idx in idxs:
        onehot = onehot + jnp.where(eidx == idx, 1.0, 0.0)
    earlier = jnp.where(ri < ci, 1.0, 0.0).astype(BF16)
    within = _dot(onehot.astype(BF16), earlier)
    tile_cnt = jnp.sum(onehot, axis=1, keepdims=True)
    n_copies = jnp.floor((tile_cnt + float(COPY_ROWS - 1)) * (1.0 / COPY_ROWS))
    er = lax.broadcasted_iota(jnp.int32, (N_EXPERTS, N_EXPERTS), 0)
    ec = lax.broadcasted_iota(jnp.int32, (N_EXPERTS, N_EXPERTS), 1)
    lower_expert = jnp.where(ec < er, 1.0, 0.0).astype(BF16)
    run_start = float(COPY_ROWS) * _dot(
        lower_expert, jnp.broadcast_to(n_copies, (N_EXPERTS, LANES)).astype(BF16))[:, 0:1]
    slot_of = run_start + within
    srow = lax.broadcasted_iota(jnp.int32, (SUBLANES, tt), 0)
    route = jnp.zeros((SUBLANES, tt), F32)
    gates = jnp.zeros((SUBLANES, tt), F32)
    for kk in range(TOP_K):
        slot = jnp.sum(jnp.where(eidx == idxs[kk], slot_of, 0.0), axis=0, keepdims=True)
        route = jnp.where(srow == kk, slot * float(ROW_SUBLANES), route)
        gates = jnp.where(srow == kk, exps[kk] / denom, gates)
    route_ref[...] = route.astype(jnp.int32)
    gate_ref[...] = gates
    mlane = lax.broadcasted_iota(jnp.int32, (N_EXPERTS, LANES), 1)
    meta_ref[...] = jnp.where(mlane == 0, tile_cnt, jnp.where(mlane == 1, cnt[...], 0.0))
    cnt[...] = cnt[...] + tile_cnt
    for j in range(ZERO_COPIES):
        zero_copy(j).wait()


def _mix_call(x, anw, win_r, convw, alog_p, dtb_p, dnw, poolw, pscale, wout, fnw, wr_p, br_p, xs_sublanes):
    bsz, seq, d = x.shape
    n_t = seq // TILE
    n_tok = bsz * seq
    fill_rows = -(-xs_sublanes // (bsz * n_t * ZERO_COPIES * SUBLANES)) * SUBLANES
    const2 = lambda b, t: (0, 0)
    tok_map = lambda b, t: (b * n_t + t, 0)
    in_specs = [
        pl.BlockSpec((None, TILE, d), lambda b, t: (b, t, 0)),
        pl.BlockSpec(anw.shape, const2),
        pl.BlockSpec(win_r.shape, const2),
        pl.BlockSpec(convw.shape, const2),
        pl.BlockSpec(alog_p.shape, const2),
        pl.BlockSpec(dtb_p.shape, const2),
        pl.BlockSpec(dnw.shape, const2),
        pl.BlockSpec(poolw.shape, lambda b, t: (0, 0, 0)),
        pl.BlockSpec(pscale.shape, const2),
        pl.BlockSpec(wout.shape, const2),
        pl.BlockSpec(fnw.shape, const2),
        pl.BlockSpec(wr_p.shape, const2),
        pl.BlockSpec(br_p.shape, const2),
    ]
    out_shape = (
        jax.ShapeDtypeStruct((bsz, seq, d), F32),
        jax.ShapeDtypeStruct((bsz * n_t * SUBLANES, TILE), jnp.int32),
        jax.ShapeDtypeStruct((bsz * n_t * SUBLANES, TILE), F32),
        jax.ShapeDtypeStruct((bsz * n_t * N_EXPERTS, LANES), F32),
        jax.ShapeDtypeStruct((xs_sublanes, LANES), F32),
    )
    out_specs = (
        pl.BlockSpec((None, TILE, d), lambda b, t: (b, t, 0)),
        pl.BlockSpec((SUBLANES, TILE), tok_map),
        pl.BlockSpec((SUBLANES, TILE), tok_map),
        pl.BlockSpec((N_EXPERTS, LANES), tok_map),
        pl.BlockSpec(memory_space=pl.ANY),
    )
    scratch = [
        pltpu.VMEM((3 * N_DN_HEADS, CONV_HALO + TILE, LANES), F32),
        pltpu.VMEM((N_POOL_GROUPS, POOL_HALO + TILE, LANES), F32),
        pltpu.VMEM((N_DN_HEADS, DN_HEAD_DIM, DN_HEAD_DIM), F32),
        pltpu.VMEM((N_EXPERTS, LANES), F32),
        pltpu.VMEM((fill_rows, LANES), F32),
        pltpu.SemaphoreType.DMA(()),
    ]
    return pl.pallas_call(
        _mix_kernel,
        grid=(bsz, n_t),
        in_specs=in_specs,
        out_specs=out_specs,
        out_shape=out_shape,
        scratch_shapes=scratch,
        compiler_params=pltpu.CompilerParams(
            dimension_semantics=("arbitrary", "arbitrary"), vmem_limit_bytes=VMEM_LIMIT_BYTES),
        name="mix",
    )(x, anw, win_r, convw, alog_p, dtb_p, dnw, poolw, pscale, wout, fnw, wr_p, br_p)


COPY_SUBLANES = COPY_ROWS * ROW_SUBLANES


COPY_TABLE = 256
assert SLOTS // COPY_ROWS < COPY_TABLE
LOOP_UNROLL = 8


def _for_count(n, fn):
    def main(i, carry):
        for u in range(LOOP_UNROLL):
            fn(i * LOOP_UNROLL + u)
        return carry
    n_main = lax.shift_right_logical(n, LOOP_UNROLL.bit_length() - 1)
    lax.fori_loop(0, n_main, main, 0)

    def tail(i, carry):
        fn(i)
        return carry
    lax.fori_loop(n_main * LOOP_UNROLL, n, tail, 0)


def _run_copies(dst_ref, local_buf, global_ref, sem, to_global):
    n = dst_ref[COPY_TABLE - 1]

    def start(c):
        loc = local_buf.at[pl.ds(pl.multiple_of(c * COPY_SUBLANES, COPY_SUBLANES), COPY_SUBLANES)]
        glo = global_ref.at[pl.ds(pl.multiple_of(dst_ref[c], SUBLANES), COPY_SUBLANES)]
        if to_global:
            pltpu.make_async_copy(loc, glo, sem).start()
        else:
            pltpu.make_async_copy(glo, loc, sem).start()
    _for_count(n, start)
    return n


def _wait_copies(n, local_buf, global_ref, sem):
    def wait(i):
        del i
        pltpu.make_async_copy(local_buf.at[pl.ds(0, COPY_SUBLANES)], global_ref.at[pl.ds(0, COPY_SUBLANES)], sem).wait()
    _for_count(n, wait)


def _dispatch_kernel(slot_ref, dst_ref, x1_ref, fnw_ref, xs_in_ref, xs_ref,
                     tok_tiles, sorted_buf, inflight, sem):
    del xs_in_ref
    step = pl.program_id(0)
    buf = sorted_buf.at[step % 2]
    x1 = x1_ref[...]
    hn = x1 * _rms_scale(x1) * fnw_ref[...]
    for s in range(ROW_SUBLANES):
        tok_tiles[pl.ds(s, TILE, stride=ROW_SUBLANES), :] = hn[:, s * LANES:(s + 1) * LANES]
    buf[...] = jnp.zeros(buf.shape, F32)

    def token_body(i, carry):
        tile = tok_tiles[pl.ds(pl.multiple_of(i * ROW_SUBLANES, ROW_SUBLANES), ROW_SUBLANES), :]
        for kk in range(TOP_K):
            slot_row = slot_ref[i * TOP_K + kk]
            buf[pl.ds(pl.multiple_of(slot_row, ROW_SUBLANES), ROW_SUBLANES), :] = tile
        return carry
    lax.fori_loop(0, TILE, token_body, 0, unroll=8)

    @pl.when(step > 0)
    def _():
        _wait_copies(inflight[0], buf, xs_ref, sem)
    inflight[0] = _run_copies(dst_ref, buf, xs_ref, sem, to_global=True)

    @pl.when(step == pl.num_programs(0) - 1)
    def _():
        _wait_copies(inflight[0], buf, xs_ref, sem)


def _smem_spec(n):
    return pl.BlockSpec((n,), lambda i: (i,), memory_space=pltpu.SMEM)


def _dispatch_call(slots, dst, x1_flat, fnw, xs_zero):
    n_tok, d = x1_flat.shape
    return pl.pallas_call(
        _dispatch_kernel,
        grid=(n_tok // TILE,),
        in_specs=[
            _smem_spec(TILE * TOP_K), _smem_spec(COPY_TABLE),
            pl.BlockSpec((TILE, d), lambda i: (i, 0)),
            pl.BlockSpec(fnw.shape, lambda i: (0, 0)),
            pl.BlockSpec(memory_space=pl.ANY),
        ],
        out_specs=pl.BlockSpec(memory_space=pl.ANY),
        out_shape=jax.ShapeDtypeStruct(xs_zero.shape, F32),
        scratch_shapes=[pltpu.VMEM((TILE * ROW_SUBLANES, LANES), F32),
                        pltpu.VMEM((2, SLOTS * ROW_SUBLANES, LANES), F32),
                        pltpu.SMEM((1,), jnp.int32),
                        pltpu.SemaphoreType.DMA(())],
        input_output_aliases={4: 0},
        compiler_params=pltpu.CompilerParams(dimension_semantics=("arbitrary",)),
        name="dispatch",
    )(slots, dst, x1_flat, fnw, xs_zero)


def _combine_kernel(slot_ref, gatev_ref, dst_ref, dst_next_ref, x1_ref, fw_ref, ys_ref,
                    out_ref, sorted_buf, tok_tiles, inflight, sems):
    step = pl.program_id(0)
    cur = step % 2
    nxt = 1 - cur
    buf = sorted_buf.at[cur]

    @pl.when(step == 0)
    def _():
        inflight[0] = _run_copies(dst_ref, sorted_buf.at[0], ys_ref, sems.at[0], to_global=False)

    @pl.when(step + 1 < pl.num_programs(0))
    def _():
        inflight[nxt] = _run_copies(dst_next_ref, sorted_buf.at[nxt], ys_ref, sems.at[nxt], to_global=False)
    _wait_copies(inflight[cur], buf, ys_ref, sems.at[cur])

    def token_body(i, carry):
        acc = None
        for kk in range(TOP_K):
            slot_row = slot_ref[i * TOP_K + kk]
            row = buf[pl.ds(pl.multiple_of(slot_row, ROW_SUBLANES), ROW_SUBLANES), :]
            term = row * gatev_ref[i * TOP_K + kk]
            acc = term if acc is None else acc + term
        tok_tiles[pl.ds(pl.multiple_of(i * ROW_SUBLANES, ROW_SUBLANES), ROW_SUBLANES), :] = acc
        return carry
    lax.fori_loop(0, TILE, token_body, 0, unroll=8)

    x1 = x1_ref[...]
    cols = []
    ssq = None
    for s in range(ROW_SUBLANES):
        c = x1[:, s * LANES:(s + 1) * LANES] + tok_tiles[pl.ds(s, TILE, stride=ROW_SUBLANES), :]
        cols.append(c)
        sq = jnp.sum(c * c, axis=-1, keepdims=True)
        ssq = sq if ssq is None else ssq + sq
    scale = lax.rsqrt(ssq * (1.0 / D_MODEL) + RMS_EPS)
    for s in range(ROW_SUBLANES):
        out_ref[:, s * LANES:(s + 1) * LANES] = cols[s] * scale * fw_ref[:, s * LANES:(s + 1) * LANES]


def _combine_call(slots, gatev, dst, x1_flat, fw, ys):
    n_tok, d = x1_flat.shape
    n_steps = n_tok // TILE
    next_spec = pl.BlockSpec((COPY_TABLE,), lambda i: (jnp.minimum(i + 1, n_steps - 1),), memory_space=pltpu.SMEM)
    return pl.pallas_call(
        _combine_kernel,
        grid=(n_steps,),
        in_specs=[
            _smem_spec(TILE * TOP_K), _smem_spec(TILE * TOP_K), _smem_spec(COPY_TABLE), next_spec,
            pl.BlockSpec((TILE, d), lambda i: (i, 0)),
            pl.BlockSpec(fw.shape, lambda i: (0, 0)),
            pl.BlockSpec(memory_space=pl.ANY),
        ],
        out_specs=pl.BlockSpec((TILE, d), lambda i: (i, 0)),
        out_shape=jax.ShapeDtypeStruct((n_tok, d), F32),
        scratch_shapes=[pltpu.VMEM((2, SLOTS * ROW_SUBLANES, LANES), F32),
                        pltpu.VMEM((TILE * ROW_SUBLANES, LANES), F32),
                        pltpu.SMEM((2,), jnp.int32),
                        pltpu.SemaphoreType.DMA((2,))],
        compiler_params=pltpu.CompilerParams(dimension_semantics=("arbitrary",)),
        name="combine",
    )(slots, gatev, dst, dst, x1_flat, fw, ys)


def _moe_kernel(be_ref, nu_ref, xs_ref, wgu_ref, bgu_ref, wd_ref, bd_ref, ys_ref, wgu16, wd16):
    i = pl.program_id(0)

    @pl.when(i < nu_ref[0])
    def _():
        prev = be_ref[jnp.maximum(i - 1, 0)]
        new_expert = jnp.logical_or(i == 0, be_ref[i] != prev)

        @pl.when(new_expert)
        def _():
            wgu16[...] = wgu_ref[...].astype(BF16)
            wd16[...] = wd_ref[...].astype(BF16)

        xb = jnp.concatenate(
            [xs_ref[pl.ds(s, MOE_ROWS, stride=ROW_SUBLANES), :].astype(BF16) for s in range(ROW_SUBLANES)], axis=1)
        acc = None
        for j in range(D_EXPERT // FF_CHUNK):
            c0 = j * FF_CHUNK
            gate = _dot(xb, wgu16[:, c0:c0 + FF_CHUNK]) + bgu_ref[:, c0:c0 + FF_CHUNK]
            up = _dot(xb, wgu16[:, D_EXPERT + c0:D_EXPERT + c0 + FF_CHUNK]) + bgu_ref[:, D_EXPERT + c0:D_EXPERT + c0 + FF_CHUNK]
            gate = jnp.minimum(gate, SWIGLU_LIMIT)
            up = jnp.clip(up, -SWIGLU_LIMIT, SWIGLU_LIMIT)
            act = (up + 1.0) * gate * jax.nn.sigmoid(SWIGLU_ALPHA * gate)
            term = _dot(act.astype(BF16), wd16[c0:c0 + FF_CHUNK, :])
            acc = term if acc is None else acc + term
        y = acc + bd_ref[...]
        for s in range(ROW_SUBLANES):
            ys_ref[pl.ds(s, MOE_ROWS, stride=ROW_SUBLANES), :] = y[:, s * LANES:(s + 1) * LANES]

    @pl.when(i >= nu_ref[0])
    def _():
        ys_ref[...] = jnp.zeros(ys_ref.shape, F32)


def _moe_call(block_e, n_used, xs, wgu, bgu, wd, bd):
    n_rows = xs.shape[0] // ROW_SUBLANES
    d = D_MODEL
    n_blocks = n_rows // MOE_ROWS
    blk = lambda i, be, nu: (jnp.minimum(i, nu[0] - 1), 0)
    exp3 = lambda i, be, nu: (be[jnp.minimum(i, nu[0] - 1)], 0, 0)
    grid_spec = pltpu.PrefetchScalarGridSpec(
        num_scalar_prefetch=2,
        grid=(n_blocks,),
        in_specs=[
            pl.BlockSpec((MOE_ROWS * ROW_SUBLANES, LANES), blk),
            pl.BlockSpec((None, d, 2 * D_EXPERT), exp3),
            pl.BlockSpec((None, 1, 2 * D_EXPERT), exp3),
            pl.BlockSpec((None, D_EXPERT, d), exp3),
            pl.BlockSpec((None, 1, d), exp3),
        ],
        out_specs=pl.BlockSpec((MOE_ROWS * ROW_SUBLANES, LANES), lambda i, be, nu: (i, 0)),
        scratch_shapes=[pltpu.VMEM((d, 2 * D_EXPERT), BF16), pltpu.VMEM((D_EXPERT, d), BF16)],
    )
    return pl.pallas_call(
        _moe_kernel,
        grid_spec=grid_spec,
        out_shape=jax.ShapeDtypeStruct(xs.shape, F32),
        compiler_params=pltpu.CompilerParams(
            dimension_semantics=("arbitrary",), vmem_limit_bytes=VMEM_LIMIT_BYTES),
        name="moe",
    )(block_e, n_used, xs, wgu, bgu, wd, bd)


def _pad_lanes(v, lane0):
    out = jnp.zeros((1, LANES), F32)
    return lax.dynamic_update_slice(out, v.reshape(1, -1).astype(F32), (0, lane0))


def kernel(x, attn_norm_w, w_in, conv_w, a_log, dt_bias, dn_norm_w, pool_w, pool_scale, w_out, ffn_norm_w,
           w_router, b_router, w_gate_up, b_gate_up, w_down, b_down, final_norm_w):
    bsz, seq, d = x.shape
    n_tok = bsz * seq
    assert w_in.shape[0] == 1, "the combine kernel fuses the final norm, so exactly one layer is supported"
    assert d == D_MODEL and seq % TILE == 0
    l = 0

    s_z, s_a = 4 * D_DN, 4 * D_DN + 2 * N_DN_HEADS
    ba_cols = jnp.pad(w_in[l][:, s_z:s_a], ((0, 0), (0, LANES - 2 * N_DN_HEADS)))
    win_r = jnp.concatenate([w_in[l][:, :s_z], w_in[l][:, s_a:], ba_cols], axis=1).astype(BF16)
    wr_hi, wr_lo = _hi_lo(jnp.pad(w_router[l], ((0, 0), (0, LANES - N_EXPERTS))))
    wr_p = jnp.concatenate([wr_hi, wr_lo], axis=1)
    br_p = _pad_lanes(b_router[l], 0)
    fnw = ffn_norm_w[l].reshape(1, d)

    n_blocks = (n_tok * TOP_K + N_EXPERTS * (MOE_ROWS + COPY_ROWS - 2)) // MOE_ROWS + 1
    n_rows = n_blocks * MOE_ROWS

    x1, route, gates, meta, xs_zero = _mix_call(
        x, attn_norm_w[l].reshape(1, d), win_r, conv_w[l], _pad_lanes(a_log[l], A_LANE0),
        _pad_lanes(dt_bias[l], A_LANE0), dn_norm_w[l].reshape(1, DN_HEAD_DIM), pool_w[l].astype(BF16),
        pool_scale[l].reshape(1, D_POOL), w_out[l].astype(BF16), fnw, wr_p, br_p, n_rows * ROW_SUBLANES)

    n_tiles = n_tok // TILE
    meta = meta.reshape(n_tiles, N_EXPERTS, LANES)[:, :, :2].astype(jnp.int32)
    tile_cnt, tile_before = meta[:, :, 0], meta[:, :, 1]
    total = tile_cnt[-1] + tile_before[-1]
    padded = jnp.where(total > 0, ((total + COPY_ROWS - 1 + MOE_ROWS - 1) // MOE_ROWS) * MOE_ROWS, 0)
    pad_end = jnp.cumsum(padded)
    pad_start = pad_end - padded
    run_base = pad_start[None, :] + tile_before
    ncopy = (tile_cnt + COPY_ROWS - 1) // COPY_ROWS
    copy_end = jnp.cumsum(ncopy, axis=1)
    copy_start = copy_end - ncopy
    c_idx = jnp.arange(COPY_TABLE, dtype=jnp.int32)[None, :, None]
    in_run = jnp.logical_and(c_idx >= copy_start[:, None, :], c_idx < copy_end[:, None, :])
    dst_row = jnp.sum(jnp.where(in_run, run_base[:, None, :] + COPY_ROWS * (c_idx - copy_start[:, None, :]), 0), axis=2)
    dst = (dst_row * ROW_SUBLANES).at[:, COPY_TABLE - 1].set(copy_end[:, -1]).reshape(-1)
    block_start = jnp.arange(n_blocks, dtype=jnp.int32) * MOE_ROWS
    block_e = jnp.minimum(jnp.sum(block_start[:, None] >= pad_end[None, :], axis=1), N_EXPERTS - 1).astype(jnp.int32)
    n_used = (pad_end[-1:] // MOE_ROWS).astype(jnp.int32)
    def token_major(a):
        return a.reshape(n_tiles, SUBLANES, TILE)[:, :TOP_K].transpose(0, 2, 1).reshape(-1)
    slots = token_major(route)
    gatev = token_major(gates)

    x1_flat = x1.reshape(n_tok, d)
    xs = _dispatch_call(slots, dst, x1_flat, fnw, xs_zero)
    ys = _moe_call(block_e, n_used, xs, w_gate_up[l], b_gate_up[l].reshape(N_EXPERTS, 1, -1),
                   w_down[l], b_down[l].reshape(N_EXPERTS, 1, -1))
    out = _combine_call(slots, gatev, dst, x1_flat, final_norm_w.reshape(1, d), ys)
    return out.reshape(bsz, seq, d)
```

```python
import jax
import jax.numpy as jnp
from jax import lax
from jax.experimental import pallas as pl
from jax.experimental.pallas import tpu as pltpu

F32 = jnp.float32
BF16 = jnp.bfloat16

D_MODEL = 1024
N_DN_HEADS = 4
DN_HEAD_DIM = 128
D_DN = N_DN_HEADS * DN_HEAD_DIM
CONV_K = 4
CHUNK = 64
POOL_WINDOWS = (2, 4, 8, 16)
N_POOL_GROUPS = len(POOL_WINDOWS)
D_POOL = D_MODEL - D_DN
POOL_GROUP_DIM = D_POOL // N_POOL_GROUPS
N_EXPERTS = 32
TOP_K = 4
D_EXPERT = D_MODEL
SWIGLU_LIMIT = 7.0
SWIGLU_ALPHA = 1.702
RMS_EPS = 1e-6
L2_EPS = 1e-6

LANES = 128
SUBLANES = 8
ROW_SUBLANES = D_MODEL // LANES
VMEM_LIMIT_BYTES = 56 * 1024 * 1024

TILE = 256
PAIR = 2 * CHUNK
CONV_HALO = SUBLANES
POOL_HALO = 2 * SUBLANES
MOE_ROWS = 512
FF_CHUNK = 512
COPY_ROWS = 8
ZERO_COPIES = 4
SEQ_PER_STEP = 2
PHASE_SKEW = 0
SLOTS = TILE * TOP_K + N_EXPERTS * COPY_ROWS
assert TILE * TOP_K // COPY_ROWS + 1 <= 256

COL_QKV = 0
COL_Z = 3 * D_DN
COL_P = 4 * D_DN
COL_BA = 4 * D_DN + D_POOL
D_IN_R = COL_BA + LANES
A_LANE0 = N_DN_HEADS


def _dot(a, b):
    return jnp.dot(a, b, preferred_element_type=F32)


def _dot_nt(a, b):
    return lax.dot_general(a, b, (((1,), (1,)), ((), ())), preferred_element_type=F32)


def _hi_lo(x):
    hi = x.astype(BF16)
    lo = (x - hi.astype(F32)).astype(BF16)
    return hi, lo


def _dot_f32(a, b):
    ah, al = _hi_lo(a)
    bh, bl = _hi_lo(b)
    return _dot(ah, bh) + (_dot(ah, bl) + _dot(al, bh))


def _dot_exact_lhs(m_bf16, x):
    hi = x.astype(BF16)
    r = x - hi.astype(F32)
    mid = r.astype(BF16)
    lo = (r - mid.astype(F32)).astype(BF16)
    return _dot(m_bf16, hi) + (_dot(m_bf16, mid) + _dot(m_bf16, lo))


def _rms_scale(x):
    return lax.rsqrt(jnp.mean(x * x, axis=-1, keepdims=True) + RMS_EPS)


def _silu(x):
    return x * jax.nn.sigmoid(x)


def _softplus(x):
    return jnp.maximum(x, 0.0) + jnp.log1p(jnp.exp(-jnp.abs(x)))


def _mix_tile(t, x_ref, anw_ref, win_ref, convw_ref, alog_ref, dtb_ref, dnw_ref, poolw_ref, pscale_ref,
              wout_ref, fnw_ref, wr_ref, br_ref,
              x1_ref, route_ref, gate_ref, meta_ref,
              qkv_ext, p_ext, state, cnt):
    tt = TILE

    x = x_ref[...]
    hb = (x * _rms_scale(x) * anw_ref[...]).astype(BF16)
    qkv_pre = _dot(hb, win_ref[:, COL_QKV:COL_Z])
    for c in range(3 * N_DN_HEADS):
        qkv_ext[c, CONV_HALO:CONV_HALO + tt, :] = qkv_pre[:, c * LANES:(c + 1) * LANES]
    z = _dot(hb, win_ref[:, COL_Z:COL_P])
    p_pre = _dot(hb, win_ref[:, COL_P:COL_BA])
    for c in range(N_POOL_GROUPS):
        p_ext[c, POOL_HALO:POOL_HALO + tt, :] = p_pre[:, c * LANES:(c + 1) * LANES]
    ba = _dot(hb, win_ref[:, COL_BA:D_IN_R])

    yield
    beta = jax.nn.sigmoid(ba)
    g = -jnp.exp(alog_ref[...]) * _softplus(ba + dtb_ref[...])
    ri = lax.broadcasted_iota(jnp.int32, (tt, tt), 0)
    ci = lax.broadcasted_iota(jnp.int32, (tt, tt), 1)
    same_chunk = (ri // CHUNK) == (ci // CHUNK)
    l_incl = jnp.where(jnp.logical_and(same_chunk, ri >= ci), 1.0, 0.0).astype(BF16)
    gc = _dot_exact_lhs(l_incl, g)
    gl = jnp.concatenate(
        [jnp.broadcast_to(gc[c * CHUNK + CHUNK - 1:c * CHUNK + CHUNK, :], (CHUNK, LANES)) for c in range(tt // CHUNK)],
        axis=0)
    eg = jnp.exp(gc)
    ekd = jnp.exp(gl - gc)
    egl = jnp.exp(gl)
    gct = gc.T

    pr = lax.broadcasted_iota(jnp.int32, (PAIR, PAIR), 0)
    pc = lax.broadcasted_iota(jnp.int32, (PAIR, PAIR), 1)
    pair_same = (pr // CHUNK) == (pc // CHUNK)
    causal = jnp.logical_and(pair_same, pr >= pc)
    strict = jnp.logical_and(pair_same, pr > pc)
    eye = jnp.where(pr == pc, 1.0, 0.0).astype(F32)
    zeros_chunk = jnp.zeros((CHUNK, DN_HEAD_DIM), F32)

    def conv_silu(col):
        acc = None
        for j in range(CONV_K):
            rows = pl.ds(CONV_HALO - (CONV_K - 1) + j, tt)
            term = qkv_ext[col, rows, :] * convw_ref[j:j + 1, col * LANES:(col + 1) * LANES]
            acc = term if acc is None else acc + term
        return _silu(acc)

    def l2n(v):
        return v * lax.rsqrt(jnp.sum(v * v, axis=-1, keepdims=True) + L2_EPS)

    def block_diag(b0, b1):
        zero = jnp.zeros_like(b0)
        return jnp.concatenate([jnp.concatenate([b0, zero], axis=1), jnp.concatenate([zero, b1], axis=1)], axis=0)

    def dot_pair(a0, a1, b0, b1):
        out = _dot(jnp.concatenate([a0, a1], axis=1), block_diag(b0, b1))
        return out[:, :LANES], out[:, LANES:]

    n_pairs = tt // PAIR
    assert n_pairs == 2 and N_DN_HEADS % 2 == 0
    insts = []
    for h in range(N_DN_HEADS):
        yield
        q_all = l2n(conv_silu(h)) * (DN_HEAD_DIM ** -0.5)
        k_all = l2n(conv_silu(N_DN_HEADS + h))
        v_all = conv_silu(2 * N_DN_HEADS + h)
        la = A_LANE0 + h
        pre = []
        for d in range(n_pairs):
            r0 = d * PAIR
            q = q_all[r0:r0 + PAIR]
            k = k_all[r0:r0 + PAIR]
            v = v_all[r0:r0 + PAIR]
            bcol = beta[r0:r0 + PAIR, h:h + 1]
            gcol = gc[r0:r0 + PAIR, la:la + 1]
            egcol = eg[r0:r0 + PAIR, la:la + 1]
            ekdcol = ekd[r0:r0 + PAIR, la:la + 1]
            grow = gct[la:la + 1, r0:r0 + PAIR]
            diff = gcol - grow
            decay = jnp.where(causal, jnp.exp(jnp.where(causal, diff, 0.0)), 0.0)
            kb = k * bcol
            pre.append(dict(
                decay=decay, kq=jnp.concatenate([kb, q], axis=0).astype(BF16), kt=k.T.astype(BF16),
                rhs=jnp.concatenate([(v * bcol).astype(BF16), (kb * egcol).astype(BF16)], axis=1),
                qd=(q * egcol).astype(BF16), kdt=(k * ekdcol).T.astype(BF16)))
        kk0, kk1 = dot_pair(pre[0]["kq"], pre[1]["kq"], pre[0]["kt"], pre[1]["kt"])
        for d, kk in enumerate((kk0, kk1)):
            it = pre[d]
            it["a"] = jnp.where(strict, kk[:PAIR] * it["decay"], 0.0)
            it["qk"] = jnp.where(causal, kk[PAIR:] * it["decay"], 0.0).astype(BF16)
            insts.append(it)

    rs = [-it["a"] for it in insts]
    p16 = [it["a"].astype(BF16) for it in insts]
    for _ in range(CHUNK.bit_length() - 2):
        yield
        pws = []
        for i in range(0, len(insts), 2):
            pws.extend(dot_pair(p16[i], p16[i + 1], p16[i], p16[i + 1]))
        p16 = [pw.astype(BF16) for pw in pws]
        rps = []
        for i in range(0, len(insts), 2):
            rps.extend(dot_pair(rs[i].astype(BF16), rs[i + 1].astype(BF16), p16[i], p16[i + 1]))
        rs = [r + pw + rp for r, pw, rp in zip(rs, pws, rps)]
    yield
    for it, r in zip(insts, rs):
        uw = _dot((eye + r).astype(BF16), it["rhs"])
        it["u"] = uw[:, :DN_HEAD_DIM]
        it["wq"] = [jnp.concatenate([uw[c0:c0 + CHUNK, DN_HEAD_DIM:].astype(BF16), it["qd"][c0:c0 + CHUNK]], axis=0)
                    for c0 in range(0, PAIR, CHUNK)]
        it["qkk"] = [jnp.concatenate([it["qk"][c0:c0 + CHUNK], it["kdt"]], axis=0)
                     for c0 in range(0, PAIR, CHUNK)]

    s_cur = [state[h] for h in range(N_DN_HEADS)]
    o_rows = [[None] * (tt // CHUNK) for _ in range(N_DN_HEADS)]
    for d in range(n_pairs):
        for c in range(PAIR // CHUNK):
            yield
            c0 = c * CHUNK
            row = d * PAIR + c0
            for h0 in range(0, N_DN_HEADS, 2):
                its = [insts[h * n_pairs + d] for h in (h0, h0 + 1)]
                ss = [s_cur[h0], s_cur[h0 + 1]]
                ws = dot_pair(its[0]["wq"][c], its[1]["wq"][c], ss[0].astype(BF16), ss[1].astype(BF16))
                v_pads = []
                for it, wsq in zip(its, ws):
                    v_new = it["u"][c0:c0 + CHUNK] - wsq[:CHUNK]
                    parts = [zeros_chunk] * (PAIR // CHUNK)
                    parts[c] = v_new
                    v_pads.append(jnp.concatenate(parts, axis=0).astype(BF16))
                ov = dot_pair(its[0]["qkk"][c], its[1]["qkk"][c], v_pads[0], v_pads[1])
                for j, h in enumerate((h0, h0 + 1)):
                    o_rows[h][d * (PAIR // CHUNK) + c] = ws[j][CHUNK:] + ov[j][:CHUNK]
                    gt = egl[row:row + 1, A_LANE0 + h:A_LANE0 + h + 1]
                    s_cur[h] = ss[j] * gt + ov[j][CHUNK:]
    for h in range(N_DN_HEADS):
        state[h] = s_cur[h]

    yield
    dn_parts = []
    for h in range(N_DN_HEADS):
        o = jnp.concatenate(o_rows[h], axis=0)
        zh = z[:, h * LANES:(h + 1) * LANES]
        dn_parts.append(((o * _rms_scale(o) * dnw_ref[...]) * _silu(zh)).astype(BF16))

    pos = (t * tt + lax.broadcasted_iota(jnp.int32, (tt, 1), 0) + 1).astype(F32)
    pool_parts = []
    for gi, win in enumerate(POOL_WINDOWS):
        cols = slice(gi * POOL_GROUP_DIM, (gi + 1) * POOL_GROUP_DIM)
        cur = p_ext[gi, POOL_HALO:POOL_HALO + tt, :]
        acc = cur
        for sft in range(1, win):
            acc = acc + p_ext[gi, pl.ds(POOL_HALO - sft, tt), :]
        mixed = acc / jnp.minimum(pos, float(win)) - cur
        y = _dot(mixed.astype(BF16), poolw_ref[gi]) * pscale_ref[:, cols]
        pool_parts.append(y.astype(BF16))

    yield
    x1 = x +_dot(jnp.concatenate(dn_parts + pool_parts, axis=1), wout_ref[...])
    x1_ref[...] = x1

    qkv_ext[:, 0:CONV_HALO, :] = qkv_ext[:, tt:tt + CONV_HALO, :]
    p_ext[:, 0:POOL_HALO, :] = p_ext[:, tt:tt + POOL_HALO, :]

    hn = x1 * _rms_scale(x1) * fnw_ref[...]
    hn_hi, hn_lo = _hi_lo(hn)
    hh = _dot(hn_hi, wr_ref[...])
    logits = hh[:, :LANES] + (hh[:, LANES:] + _dot(hn_lo, wr_ref[:, :LANES])) + br_ref[...]
    lt = logits.T[:N_EXPERTS]
    eidx = lax.broadcasted_iota(jnp.int32, (N_EXPERTS, tt), 0).astype(F32)
    work = lt
    vals, idxs = [], []
    for _ in range(TOP_K):
        m = jnp.max(work, axis=0, keepdims=True)
        idx = jnp.min(jnp.where(work == m, eidx, float(LANES)), axis=0, keepdims=True)
        vals.append(m)
        idxs.append(idx)
        work = jnp.where(eidx == idx, -jnp.inf, work)
    exps = [jnp.exp(v - vals[0]) for v in vals]
    denom = exps[0] + exps[1] + exps[2] + exps[3]

    yield
    onehot = jnp.zeros((N_EXPERTS, tt), F32)
    for idx in idxs:
        onehot = onehot + jnp.where(eidx == idx, 1.0, 0.0)
    earlier = jnp.where(ri < ci, 1.0, 0.0).astype(BF16)
    within = _dot(onehot.astype(BF16), earlier)
    tile_cnt = jnp.sum(onehot, axis=1, keepdims=True)
    n_copies = jnp.floor((tile_cnt + float(COPY_ROWS - 1)) * (1.0 / COPY_ROWS))
    er = lax.broadcasted_iota(jnp.int32, (N_EXPERTS, N_EXPERTS), 0)
    ec = lax.broadcasted_iota(jnp.int32, (N_EXPERTS, N_EXPERTS), 1)
    lower_expert = jnp.where(ec < er, 1.0, 0.0).astype(BF16)
    run_start = float(COPY_ROWS) * _dot(
        lower_expert, jnp.broadcast_to(n_copies, (N_EXPERTS, LANES)).astype(BF16))[:, 0:1]
    slot_of = run_start + within
    srow = lax.broadcasted_iota(jnp.int32, (SUBLANES, tt), 0)
    route = jnp.zeros((SUBLANES, tt), F32)
    gates = jnp.zeros((SUBLANES, tt), F32)
    for kk in range(TOP_K):
        slot = jnp.sum(jnp.where(eidx == idxs[kk], slot_of, 0.0), axis=0, keepdims=True)
        route = jnp.where(srow == kk, slot * float(ROW_SUBLANES), route)
        gates = jnp.where(srow == kk, exps[kk] / denom, gates)
    route_ref[...] = route.astype(jnp.int32)
    gate_ref[...] = gates
    mlane = lax.broadcasted_iota(jnp.int32, (N_EXPERTS, LANES), 1)
    meta_ref[...] = jnp.where(mlane == 0, tile_cnt, jnp.where(mlane == 1, cnt[...], 0.0))
    cnt[...] = cnt[...] + tile_cnt


def _mix_kernel(x_ref, anw_ref, win_ref, convw_ref, alog_ref, dtb_ref, dnw_ref, poolw_ref, pscale_ref,
                wout_ref, fnw_ref, wr_ref, br_ref,
                x1_ref, route_ref, gate_ref, meta_ref, xs_ref,
                qkv_ext, p_ext, state, cnt, zero_buf, zero_sem):
    b = pl.program_id(0)
    t = pl.program_id(1)
    first_step = jnp.logical_and(b == 0, t == 0)

    @pl.when(first_step)
    def _():
        zero_buf[...] = jnp.zeros(zero_buf.shape, F32)
        cnt[...] = jnp.zeros(cnt.shape, F32)
    fill_rows = zero_buf.shape[0]
    last_start = xs_ref.shape[0] - fill_rows
    step = b * pl.num_programs(1) + t

    def zero_copy(j):
        start = jnp.minimum((step * ZERO_COPIES + j) * fill_rows, last_start)
        return pltpu.make_async_copy(zero_buf, xs_ref.at[pl.ds(pl.multiple_of(start, SUBLANES), fill_rows)], zero_sem)
    for j in range(ZERO_COPIES):
        zero_copy(j).start()

    @pl.when(t == 0)
    def _():
        qkv_ext[:, :, 0:CONV_HALO, :] = jnp.zeros((SEQ_PER_STEP, 3 * N_DN_HEADS, CONV_HALO, LANES), F32)
        p_ext[:, :, 0:POOL_HALO, :] = jnp.zeros((SEQ_PER_STEP, N_POOL_GROUPS, POOL_HALO, LANES), F32)
        state[...] = jnp.zeros(state.shape, F32)

    tiles = [
        _mix_tile(t, x_ref.at[sq], anw_ref, win_ref, convw_ref, alog_ref, dtb_ref, dnw_ref, poolw_ref, pscale_ref,
                  wout_ref, fnw_ref, wr_ref, br_ref,
                  x1_ref.at[sq], route_ref.at[sq], gate_ref.at[sq], meta_ref.at[sq],
                  qkv_ext.at[sq], p_ext.at[sq], state.at[sq], cnt)
        for sq in range(SEQ_PER_STEP)]
    live = [True] * SEQ_PER_STEP
    tick = 0
    while any(live):
        for sq, gen in enumerate(tiles):
            if live[sq] and tick >= sq * PHASE_SKEW:
                live[sq] = next(gen, "done") != "done"
        tick += 1
    for j in range(ZERO_COPIES):
        zero_copy(j).wait()


def _mix_call(x, anw, win_r, convw, alog_p, dtb_p, dnw, poolw, pscale, wout, fnw, wr_p, br_p, xs_sublanes):
    bsz, seq, d = x.shape
    n_t = seq // TILE
    sps = SEQ_PER_STEP
    n_b = bsz // sps
    x = x.reshape(sps, n_b, seq, d)
    fill_rows = -(-xs_sublanes // (n_b * n_t * ZERO_COPIES * SUBLANES)) * SUBLANES
    const2 = lambda b, t: (0, 0)
    tok_map = lambda b, t: (0, b * n_t + t, 0)
    seq_map = lambda b, t: (0, b, t, 0)
    in_specs = [
        pl.BlockSpec((sps, None, TILE, d), seq_map),
        pl.BlockSpec(anw.shape, const2),
        pl.BlockSpec(win_r.shape, const2),
        pl.BlockSpec(convw.shape, const2),
        pl.BlockSpec(alog_p.shape, const2),
        pl.BlockSpec(dtb_p.shape, const2),
        pl.BlockSpec(dnw.shape, const2),
        pl.BlockSpec(poolw.shape, lambda b, t: (0, 0, 0)),
        pl.BlockSpec(pscale.shape, const2),
        pl.BlockSpec(wout.shape, const2),
        pl.BlockSpec(fnw.shape, const2),
        pl.BlockSpec(wr_p.shape, const2),
        pl.BlockSpec(br_p.shape, const2),
    ]
    out_shape = (
        jax.ShapeDtypeStruct((sps, n_b, seq, d), F32),
        jax.ShapeDtypeStruct((sps, n_b * n_t * SUBLANES, TILE), jnp.int32),
        jax.ShapeDtypeStruct((sps, n_b * n_t * SUBLANES, TILE), F32),
        jax.ShapeDtypeStruct((sps, n_b * n_t * N_EXPERTS, LANES), F32),
        jax.ShapeDtypeStruct((xs_sublanes, LANES), F32),
    )
    out_specs = (
        pl.BlockSpec((sps, None, TILE, d), seq_map),
        pl.BlockSpec((sps, SUBLANES, TILE), tok_map),
        pl.BlockSpec((sps, SUBLANES, TILE), tok_map),
        pl.BlockSpec((sps, N_EXPERTS, LANES), tok_map),
        pl.BlockSpec(memory_space=pl.ANY),
    )
    scratch = [
        pltpu.VMEM((sps, 3 * N_DN_HEADS, CONV_HALO + TILE, LANES), F32),
        pltpu.VMEM((sps, N_POOL_GROUPS, POOL_HALO + TILE, LANES), F32),
        pltpu.VMEM((sps, N_DN_HEADS, DN_HEAD_DIM, DN_HEAD_DIM), F32),
        pltpu.VMEM((N_EXPERTS, LANES), F32),
        pltpu.VMEM((fill_rows, LANES), F32),
        pltpu.SemaphoreType.DMA(()),
    ]
    return pl.pallas_call(
        _mix_kernel,
        grid=(n_b, n_t),
        in_specs=in_specs,
        out_specs=out_specs,
        out_shape=out_shape,
        scratch_shapes=scratch,
        compiler_params=pltpu.CompilerParams(
            dimension_semantics=("arbitrary", "arbitrary"), vmem_limit_bytes=VMEM_LIMIT_BYTES),
        name="mix",
    )(x, anw, win_r, convw, alog_p, dtb_p, dnw, poolw, pscale, wout, fnw, wr_p, br_p)


COPY_SUBLANES = COPY_ROWS * ROW_SUBLANES


COPY_TABLE = 256
assert SLOTS // COPY_ROWS < COPY_TABLE
LOOP_UNROLL = 8


def _for_count(n, fn):
    def main(i, carry):
        for u in range(LOOP_UNROLL):
            fn(i * LOOP_UNROLL + u)
        return carry
    n_main = lax.shift_right_logical(n, LOOP_UNROLL.bit_length() - 1)
    lax.fori_loop(0, n_main, main, 0)

    def tail(i, carry):
        fn(i)
        return carry
    lax.fori_loop(n_main * LOOP_UNROLL, n, tail, 0)


def _run_copies(dst_ref, local_buf, global_ref, sem, to_global):
    n = dst_ref[COPY_TABLE - 1]

    def start(c):
        loc = local_buf.at[pl.ds(pl.multiple_of(c * COPY_SUBLANES, COPY_SUBLANES), COPY_SUBLANES)]
        glo = global_ref.at[pl.ds(pl.multiple_of(dst_ref[c], SUBLANES), COPY_SUBLANES)]
        if to_global:
            pltpu.make_async_copy(loc, glo, sem).start()
        else:
            pltpu.make_async_copy(glo, loc, sem).start()
    _for_count(n, start)
    return n


def _wait_copies(n, local_buf, global_ref, sem):
    def wait(i):
        del i
        pltpu.make_async_copy(local_buf.at[pl.ds(0, COPY_SUBLANES)], global_ref.at[pl.ds(0, COPY_SUBLANES)], sem).wait()
    _for_count(n, wait)


def _dispatch_kernel(slot_ref, dst_ref, x1_ref, fnw_ref, xs_in_ref, xs_ref,
                     tok_tiles, sorted_buf, inflight, sem):
    del xs_in_ref
    step = pl.program_id(0)
    buf = sorted_buf.at[step % 2]
    x1 = x1_ref[...]
    hn = x1 * _rms_scale(x1) * fnw_ref[...]
    for s in range(ROW_SUBLANES):
        tok_tiles[pl.ds(s, TILE, stride=ROW_SUBLANES), :] = hn[:, s * LANES:(s + 1) * LANES]
    buf[...] = jnp.zeros(buf.shape, F32)

    def token_body(i, carry):
        tile = tok_tiles[pl.ds(pl.multiple_of(i * ROW_SUBLANES, ROW_SUBLANES), ROW_SUBLANES), :]
        for kk in range(TOP_K):
            slot_row = slot_ref[i * TOP_K + kk]
            buf[pl.ds(pl.multiple_of(slot_row, ROW_SUBLANES), ROW_SUBLANES), :] = tile
        return carry
    lax.fori_loop(0, TILE, token_body, 0, unroll=8)

    @pl.when(step > 0)
    def _():
        _wait_copies(inflight[0], buf, xs_ref, sem)
    inflight[0] = _run_copies(dst_ref, buf, xs_ref, sem, to_global=True)

    @pl.when(step == pl.num_programs(0) - 1)
    def _():
        _wait_copies(inflight[0], buf, xs_ref, sem)


def _smem_spec(n):
    return pl.BlockSpec((n,), lambda i: (i,), memory_space=pltpu.SMEM)


def _dispatch_call(slots, dst, x1_flat, fnw, xs_zero):
    n_tok, d = x1_flat.shape
    n_tiles = n_tok // TILE
    tile_of = lambda i: (i % SEQ_PER_STEP) * (n_tiles // SEQ_PER_STEP) + i // SEQ_PER_STEP
    return pl.pallas_call(
        _dispatch_kernel,
        grid=(n_tiles,),
        in_specs=[
            pl.BlockSpec((TILE * TOP_K,), lambda i: (tile_of(i),), memory_space=pltpu.SMEM),
            pl.BlockSpec((COPY_TABLE,), lambda i: (tile_of(i),), memory_space=pltpu.SMEM),
            pl.BlockSpec((TILE, d), lambda i: (tile_of(i), 0)),
            pl.BlockSpec(fnw.shape, lambda i: (0, 0)),
            pl.BlockSpec(memory_space=pl.ANY),
        ],
        out_specs=pl.BlockSpec(memory_space=pl.ANY),
        out_shape=jax.ShapeDtypeStruct(xs_zero.shape, F32),
        scratch_shapes=[pltpu.VMEM((TILE * ROW_SUBLANES, LANES), F32),
                        pltpu.VMEM((2, SLOTS * ROW_SUBLANES, LANES), F32),
                        pltpu.SMEM((1,), jnp.int32),
                        pltpu.SemaphoreType.DMA(())],
        input_output_aliases={4: 0},
        compiler_params=pltpu.CompilerParams(dimension_semantics=("arbitrary",)),
        name="dispatch",
    )(slots, dst, x1_flat, fnw, xs_zero)


def _combine_kernel(slot_ref, gatev_ref, dst_ref, dst_next_ref, x1_ref, fw_ref, ys_ref,
                    out_ref, sorted_buf, tok_tiles, inflight, sems):
    step = pl.program_id(0)
    cur = step % 2
    nxt = 1 - cur
    buf = sorted_buf.at[cur]

    @pl.when(step == 0)
    def _():
        inflight[0] = _run_copies(dst_ref, sorted_buf.at[0], ys_ref, sems.at[0], to_global=False)

    @pl.when(step + 1 < pl.num_programs(0))
    def _():
        inflight[nxt] = _run_copies(dst_next_ref, sorted_buf.at[nxt], ys_ref, sems.at[nxt], to_global=False)
    _wait_copies(inflight[cur], buf, ys_ref, sems.at[cur])

    def token_body(i, carry):
        acc = None
        for kk in range(TOP_K):
            slot_row = slot_ref[i * TOP_K + kk]
            row = buf[pl.ds(pl.multiple_of(slot_row, ROW_SUBLANES), ROW_SUBLANES), :]
            term = row * gatev_ref[i * TOP_K + kk]
            acc = term if acc is None else acc + term
        tok_tiles[pl.ds(pl.multiple_of(i * ROW_SUBLANES, ROW_SUBLANES), ROW_SUBLANES), :] = acc
        return carry
    lax.fori_loop(0, TILE, token_body, 0, unroll=8)

    x1 = x1_ref[...]
    cols = []
    ssq = None
    for s in range(ROW_SUBLANES):
        c = x1[:, s * LANES:(s + 1) * LANES] + tok_tiles[pl.ds(s, TILE, stride=ROW_SUBLANES), :]
        cols.append(c)
        sq = jnp.sum(c * c, axis=-1, keepdims=True)
        ssq = sq if ssq is None else ssq + sq
    scale = lax.rsqrt(ssq * (1.0 / D_MODEL) + RMS_EPS)
    for s in range(ROW_SUBLANES):
        out_ref[:, s * LANES:(s + 1) * LANES] = cols[s] * scale * fw_ref[:, s * LANES:(s + 1) * LANES]


def _combine_call(slots, gatev, dst, x1_flat, fw, ys):
    n_tok, d = x1_flat.shape
    n_steps = n_tok // TILE
    next_spec = pl.BlockSpec((COPY_TABLE,), lambda i: (jnp.minimum(i + 1, n_steps - 1),), memory_space=pltpu.SMEM)
    return pl.pallas_call(
        _combine_kernel,
        grid=(n_steps,),
        in_specs=[
            _smem_spec(TILE * TOP_K), _smem_spec(TILE * TOP_K), _smem_spec(COPY_TABLE), next_spec,
            pl.BlockSpec((TILE, d), lambda i: (i, 0)),
            pl.BlockSpec(fw.shape, lambda i: (0, 0)),
            pl.BlockSpec(memory_space=pl.ANY),
        ],
        out_specs=pl.BlockSpec((TILE, d), lambda i: (i, 0)),
        out_shape=jax.ShapeDtypeStruct((n_tok, d), F32),
        scratch_shapes=[pltpu.VMEM((2, SLOTS * ROW_SUBLANES, LANES), F32),
                        pltpu.VMEM((TILE * ROW_SUBLANES, LANES), F32),
                        pltpu.SMEM((2,), jnp.int32),
                        pltpu.SemaphoreType.DMA((2,))],
        compiler_params=pltpu.CompilerParams(dimension_semantics=("arbitrary",)),
        name="combine",
    )(slots, gatev, dst, dst, x1_flat, fw, ys)


def _moe_kernel(be_ref, nu_ref, xs_ref, wgu_ref, bgu_ref, wd_ref, bd_ref, ys_ref, wgu16, wd16):
    i = pl.program_id(0)

    @pl.when(i < nu_ref[0])
    def _():
        prev = be_ref[jnp.maximum(i - 1, 0)]
        new_expert = jnp.logical_or(i == 0, be_ref[i] != prev)

        @pl.when(new_expert)
        def _():
            wgu16[...] = wgu_ref[...].astype(BF16)
            wd16[...] = wd_ref[...].astype(BF16)

        xb = jnp.concatenate(
            [xs_ref[pl.ds(s, MOE_ROWS, stride=ROW_SUBLANES), :].astype(BF16) for s in range(ROW_SUBLANES)], axis=1)
        acc = None
        for j in range(D_EXPERT // FF_CHUNK):
            c0 = j * FF_CHUNK
            gate = _dot(xb, wgu16[:, c0:c0 + FF_CHUNK]) + bgu_ref[:, c0:c0 + FF_CHUNK]
            up = _dot(xb, wgu16[:, D_EXPERT + c0:D_EXPERT + c0 + FF_CHUNK]) + bgu_ref[:, D_EXPERT + c0:D_EXPERT + c0 + FF_CHUNK]
            gate = jnp.minimum(gate, SWIGLU_LIMIT)
            up = jnp.clip(up, -SWIGLU_LIMIT, SWIGLU_LIMIT)
            act = (up + 1.0) * gate * jax.nn.sigmoid(SWIGLU_ALPHA * gate)
            term = _dot(act.astype(BF16), wd16[c0:c0 + FF_CHUNK, :])
            acc = term if acc is None else acc + term
        y = acc + bd_ref[...]
        for s in range(ROW_SUBLANES):
            ys_ref[pl.ds(s, MOE_ROWS, stride=ROW_SUBLANES), :] = y[:, s * LANES:(s + 1) * LANES]

    @pl.when(i >= nu_ref[0])
    def _():
        ys_ref[...] = jnp.zeros(ys_ref.shape, F32)


def _moe_call(block_e, n_used, xs, wgu, bgu, wd, bd):
    n_rows = xs.shape[0] // ROW_SUBLANES
    d = D_MODEL
    n_blocks = n_rows // MOE_ROWS
    blk = lambda i, be, nu: (jnp.minimum(i, nu[0] - 1), 0)
    exp3 = lambda i, be, nu: (be[jnp.minimum(i, nu[0] - 1)], 0, 0)
    grid_spec = pltpu.PrefetchScalarGridSpec(
        num_scalar_prefetch=2,
        grid=(n_blocks,),
        in_specs=[
            pl.BlockSpec((MOE_ROWS * ROW_SUBLANES, LANES), blk),
            pl.BlockSpec((None, d, 2 * D_EXPERT), exp3),
            pl.BlockSpec((None, 1, 2 * D_EXPERT), exp3),
            pl.BlockSpec((None, D_EXPERT, d), exp3),
            pl.BlockSpec((None, 1, d), exp3),
        ],
        out_specs=pl.BlockSpec((MOE_ROWS * ROW_SUBLANES, LANES), lambda i, be, nu: (i, 0)),
        scratch_shapes=[pltpu.VMEM((d, 2 * D_EXPERT), BF16), pltpu.VMEM((D_EXPERT, d), BF16)],
    )
    return pl.pallas_call(
        _moe_kernel,
        grid_spec=grid_spec,
        out_shape=jax.ShapeDtypeStruct(xs.shape, F32),
        compiler_params=pltpu.CompilerParams(
            dimension_semantics=("arbitrary",), vmem_limit_bytes=VMEM_LIMIT_BYTES),
        name="moe",
    )(block_e, n_used, xs, wgu, bgu, wd, bd)


def _pad_lanes(v, lane0):
    out = jnp.zeros((1, LANES), F32)
    return lax.dynamic_update_slice(out, v.reshape(1, -1).astype(F32), (0, lane0))


def kernel(x, attn_norm_w, w_in, conv_w, a_log, dt_bias, dn_norm_w, pool_w, pool_scale, w_out, ffn_norm_w,
           w_router, b_router, w_gate_up, b_gate_up, w_down, b_down, final_norm_w):
    bsz, seq, d = x.shape
    n_tok = bsz * seq
    assert w_in.shape[0] == 1, "the combine kernel fuses the final norm, so exactly one layer is supported"
    assert d == D_MODEL and seq % TILE == 0 and bsz % SEQ_PER_STEP == 0
    l = 0

    s_z, s_a = 4 * D_DN, 4 * D_DN + 2 * N_DN_HEADS
    ba_cols = jnp.pad(w_in[l][:, s_z:s_a], ((0, 0), (0, LANES - 2 * N_DN_HEADS)))
    win_r = jnp.concatenate([w_in[l][:, :s_z], w_in[l][:, s_a:], ba_cols], axis=1).astype(BF16)
    wr_hi, wr_lo = _hi_lo(jnp.pad(w_router[l], ((0, 0), (0, LANES - N_EXPERTS))))
    wr_p = jnp.concatenate([wr_hi, wr_lo], axis=1)
    br_p = _pad_lanes(b_router[l], 0)
    fnw = ffn_norm_w[l].reshape(1, d)

    n_blocks = (n_tok * TOP_K + N_EXPERTS * (MOE_ROWS + COPY_ROWS - 2)) // MOE_ROWS + 1
    n_rows = n_blocks * MOE_ROWS

    x1, route, gates, meta, xs_zero = _mix_call(
        x, attn_norm_w[l].reshape(1, d), win_r, conv_w[l], _pad_lanes(a_log[l], A_LANE0),
        _pad_lanes(dt_bias[l], A_LANE0), dn_norm_w[l].reshape(1, DN_HEAD_DIM), pool_w[l].astype(BF16),
        pool_scale[l].reshape(1, D_POOL), w_out[l].astype(BF16), fnw, wr_p, br_p, n_rows * ROW_SUBLANES)

    n_tiles = n_tok // TILE
    meta = meta.reshape(n_tiles, N_EXPERTS, LANES)[:, :, :2].astype(jnp.int32)
    tile_cnt, tile_before = meta[:, :, 0], meta[:, :, 1]
    total = jnp.sum(tile_cnt, axis=0)
    padded = jnp.where(total > 0, ((total + COPY_ROWS - 1 + MOE_ROWS - 1) // MOE_ROWS) * MOE_ROWS, 0)
    pad_end = jnp.cumsum(padded)
    pad_start = pad_end - padded
    run_base = pad_start[None, :] + tile_before
    ncopy = (tile_cnt + COPY_ROWS - 1) // COPY_ROWS
    copy_end = jnp.cumsum(ncopy, axis=1)
    copy_start = copy_end - ncopy
    c_idx = jnp.arange(COPY_TABLE, dtype=jnp.int32)[None, :, None]
    in_run = jnp.logical_and(c_idx >= copy_start[:, None, :], c_idx < copy_end[:, None, :])
    dst_row = jnp.sum(jnp.where(in_run, run_base[:, None, :] + COPY_ROWS * (c_idx - copy_start[:, None, :]), 0), axis=2)
    dst = (dst_row * ROW_SUBLANES).at[:, COPY_TABLE - 1].set(copy_end[:, -1]).reshape(-1)
    block_start = jnp.arange(n_blocks, dtype=jnp.int32) * MOE_ROWS
    block_e = jnp.minimum(jnp.sum(block_start[:, None] >= pad_end[None, :], axis=1), N_EXPERTS - 1).astype(jnp.int32)
    n_used = (pad_end[-1:] // MOE_ROWS).astype(jnp.int32)
    def token_major(a):
        return a.reshape(n_tiles, SUBLANES, TILE)[:, :TOP_K].transpose(0, 2, 1).reshape(-1)
    slots = token_major(route)
    gatev = token_major(gates)

    x1_flat = x1.reshape(n_tok, d)
    xs = _dispatch_call(slots, dst, x1_flat, fnw, xs_zero)
    ys = _moe_call(block_e, n_used, xs, w_gate_up[l], b_gate_up[l].reshape(N_EXPERTS, 1, -1),
                   w_down[l], b_down[l].reshape(N_EXPERTS, 1, -1))
    out = _combine_call(slots, gatev, dst, x1_flat, final_norm_w.reshape(1, d), ys)
    return out.reshape(bsz, seq, d)
```

```python
import jax
import jax.numpy as jnp
from jax import lax
from jax.experimental import pallas as pl
from jax.experimental.pallas import tpu as pltpu

F32 = jnp.float32
BF16 = jnp.bfloat16

D_MODEL = 1024
N_DN_HEADS = 4
DN_HEAD_DIM = 128
D_DN = N_DN_HEADS * DN_HEAD_DIM
CONV_K = 4
CHUNK = 64
POOL_WINDOWS = (2, 4, 8, 16)
N_POOL_GROUPS = len(POOL_WINDOWS)
D_POOL = D_MODEL - D_DN
POOL_GROUP_DIM = D_POOL // N_POOL_GROUPS
N_EXPERTS = 32
TOP_K = 4
D_EXPERT = D_MODEL
SWIGLU_LIMIT = 7.0
SWIGLU_ALPHA = 1.702
RMS_EPS = 1e-6
L2_EPS = 1e-6

LANES = 128
SUBLANES = 8
ROW_SUBLANES = D_MODEL // LANES
VMEM_LIMIT_BYTES = 56 * 1024 * 1024

TILE = 256
PAIR = 2 * CHUNK
CONV_HALO = SUBLANES
POOL_HALO = 2 * SUBLANES
MOE_ROWS = 512
FF_CHUNK = 512
COPY_ROWS = 8
ZERO_COPIES = 4
SEQ_PER_STEP = 2
PHASE_SKEW = 0
SLOTS = TILE * TOP_K + N_EXPERTS * COPY_ROWS
assert TILE * TOP_K // COPY_ROWS + 1 <= 256

COL_QKV = 0
COL_Z = 3 * D_DN
COL_P = 4 * D_DN
COL_BA = 4 * D_DN + D_POOL
D_IN_R = COL_BA + LANES
A_LANE0 = N_DN_HEADS


def _dot(a, b):
    return jnp.dot(a, b, preferred_element_type=F32)


def _dot_nt(a, b):
    return lax.dot_general(a, b, (((1,), (1,)), ((), ())), preferred_element_type=F32)


def _hi_lo(x):
    hi = x.astype(BF16)
    lo = (x - hi.astype(F32)).astype(BF16)
    return hi, lo


def _dot_f32(a, b):
    ah, al = _hi_lo(a)
    bh, bl = _hi_lo(b)
    return _dot(ah, bh) + (_dot(ah, bl) + _dot(al, bh))


def _dot_exact_lhs(m_bf16, x):
    hi = x.astype(BF16)
    r = x - hi.astype(F32)
    mid = r.astype(BF16)
    lo = (r - mid.astype(F32)).astype(BF16)
    return _dot(m_bf16, hi) + (_dot(m_bf16, mid) + _dot(m_bf16, lo))


def _rms_scale(x):
    return lax.rsqrt(jnp.mean(x * x, axis=-1, keepdims=True) + RMS_EPS)


def _silu(x):
    return x * jax.nn.sigmoid(x)


def _softplus(x):
    return jnp.maximum(x, 0.0) + jnp.log1p(jnp.exp(-jnp.abs(x)))


def _mix_tile(t, x_ref, anw_ref, win_ref, convw_ref, alog_ref, dtb_ref, dnw_ref, poolw_ref, pscale_ref,
              wout_ref, fnw_ref, wr_ref, br_ref,
              x1_ref, route_ref, gate_ref, meta_ref,
              qkv_ext, p_ext, state, cnt):
    tt = TILE

    x = x_ref[...]
    hb = (x * _rms_scale(x) * anw_ref[...]).astype(BF16)
    qkv_pre = _dot(hb, win_ref[:, COL_QKV:COL_Z])
    for c in range(3 * N_DN_HEADS):
        qkv_ext[c, CONV_HALO:CONV_HALO + tt, :] = qkv_pre[:, c * LANES:(c + 1) * LANES]
    z = _dot(hb, win_ref[:, COL_Z:COL_P])
    p_pre = _dot(hb, win_ref[:, COL_P:COL_BA])
    for c in range(N_POOL_GROUPS):
        p_ext[c, POOL_HALO:POOL_HALO + tt, :] = p_pre[:, c * LANES:(c + 1) * LANES]
    ba = _dot(hb, win_ref[:, COL_BA:D_IN_R])

    yield
    beta = jax.nn.sigmoid(ba)
    g = -jnp.exp(alog_ref[...]) * _softplus(ba + dtb_ref[...])
    ri = lax.broadcasted_iota(jnp.int32, (tt, tt), 0)
    ci = lax.broadcasted_iota(jnp.int32, (tt, tt), 1)
    same_chunk = (ri // CHUNK) == (ci // CHUNK)
    l_incl = jnp.where(jnp.logical_and(same_chunk, ri >= ci), 1.0, 0.0).astype(BF16)
    gc = _dot_exact_lhs(l_incl, g)
    gl = jnp.concatenate(
        [jnp.broadcast_to(gc[c * CHUNK + CHUNK - 1:c * CHUNK + CHUNK, :], (CHUNK, LANES)) for c in range(tt // CHUNK)],
        axis=0)
    eg = jnp.exp(gc)
    ekd = jnp.exp(gl - gc)
    egl = jnp.exp(gl)
    gct = gc.T

    pr = lax.broadcasted_iota(jnp.int32, (PAIR, PAIR), 0)
    pc = lax.broadcasted_iota(jnp.int32, (PAIR, PAIR), 1)
    pair_same = (pr // CHUNK) == (pc // CHUNK)
    causal = jnp.logical_and(pair_same, pr >= pc)
    strict = jnp.logical_and(pair_same, pr > pc)
    eye = jnp.where(pr == pc, 1.0, 0.0).astype(F32)
    zeros_chunk = jnp.zeros((CHUNK, DN_HEAD_DIM), F32)

    def conv_silu(col):
        acc = None
        for j in range(CONV_K):
            rows = pl.ds(CONV_HALO - (CONV_K - 1) + j, tt)
            term = qkv_ext[col, rows, :] * convw_ref[j:j + 1, col * LANES:(col + 1) * LANES]
            acc = term if acc is None else acc + term
        return _silu(acc)

    def l2n(v):
        return v * lax.rsqrt(jnp.sum(v * v, axis=-1, keepdims=True) + L2_EPS)

    def block_diag(b0, b1):
        zero = jnp.zeros_like(b0)
        return jnp.concatenate([jnp.concatenate([b0, zero], axis=1), jnp.concatenate([zero, b1], axis=1)], axis=0)

    def dot_pair(a0, a1, b0, b1):
        out = _dot(jnp.concatenate([a0, a1], axis=1), block_diag(b0, b1))
        return out[:, :LANES], out[:, LANES:]

    n_pairs = tt // PAIR
    assert n_pairs == 2 and N_DN_HEADS % 2 == 0
    insts = []
    for h in range(N_DN_HEADS):
        yield
        q_all = l2n(conv_silu(h)) * (DN_HEAD_DIM ** -0.5)
        k_all = l2n(conv_silu(N_DN_HEADS + h))
        v_all = conv_silu(2 * N_DN_HEADS + h)
        la = A_LANE0 + h
        pre = []
        for d in range(n_pairs):
            r0 = d * PAIR
            q = q_all[r0:r0 + PAIR]
            k = k_all[r0:r0 + PAIR]
            v = v_all[r0:r0 + PAIR]
            bcol = beta[r0:r0 + PAIR, h:h + 1]
            gcol = gc[r0:r0 + PAIR, la:la + 1]
            egcol = eg[r0:r0 + PAIR, la:la + 1]
            ekdcol = ekd[r0:r0 + PAIR, la:la + 1]
            grow = gct[la:la + 1, r0:r0 + PAIR]
            diff = gcol - grow
            decay = jnp.where(causal, jnp.exp(jnp.where(causal, diff, 0.0)), 0.0)
            kb = k * bcol
            pre.append(dict(
                decay=decay, kq=jnp.concatenate([kb, q], axis=0).astype(BF16), kt=k.T.astype(BF16),
                rhs=jnp.concatenate([(v * bcol).astype(BF16), (kb * egcol).astype(BF16)], axis=1),
                qd=(q * egcol).astype(BF16), kdt=(k * ekdcol).T.astype(BF16)))
        kk0, kk1 = dot_pair(pre[0]["kq"], pre[1]["kq"], pre[0]["kt"], pre[1]["kt"])
        for d, kk in enumerate((kk0, kk1)):
            it = pre[d]
            it["a"] = jnp.where(strict, kk[:PAIR] * it["decay"], 0.0)
            it["qk"] = jnp.where(causal, kk[PAIR:] * it["decay"], 0.0).astype(BF16)
            insts.append(it)

    rs = [-it["a"] for it in insts]
    p16 = [it["a"].astype(BF16) for it in insts]
    for _ in range(CHUNK.bit_length() - 2):
        yield
        pws = []
        for i in range(0, len(insts), 2):
            pws.extend(dot_pair(p16[i], p16[i + 1], p16[i], p16[i + 1]))
        p16 = [pw.astype(BF16) for pw in pws]
        rps = []
        for i in range(0, len(insts), 2):
            rps.extend(dot_pair(rs[i].astype(BF16), rs[i + 1].astype(BF16), p16[i], p16[i + 1]))
        rs = [r + pw + rp for r, pw, rp in zip(rs, pws, rps)]
    yield
    for it, r in zip(insts, rs):
        uw = _dot((eye + r).astype(BF16), it["rhs"])
        it["u"] = uw[:, :DN_HEAD_DIM]
        it["wq"] = [jnp.concatenate([uw[c0:c0 + CHUNK, DN_HEAD_DIM:].astype(BF16), it["qd"][c0:c0 + CHUNK]], axis=0)
                    for c0 in range(0, PAIR, CHUNK)]
        it["qkk"] = [jnp.concatenate([it["qk"][c0:c0 + CHUNK], it["kdt"]], axis=0)
                     for c0 in range(0, PAIR, CHUNK)]

    s_cur = [state[h] for h in range(N_DN_HEADS)]
    o_rows = [[None] * (tt // CHUNK) for _ in range(N_DN_HEADS)]
    for d in range(n_pairs):
        for c in range(PAIR // CHUNK):
            yield
            c0 = c * CHUNK
            row = d * PAIR + c0
            for h0 in range(0, N_DN_HEADS, 2):
                its = [insts[h * n_pairs + d] for h in (h0, h0 + 1)]
                ss = [s_cur[h0], s_cur[h0 + 1]]
                ws = dot_pair(its[0]["wq"][c], its[1]["wq"][c], ss[0].astype(BF16), ss[1].astype(BF16))
                v_pads = []
                for it, wsq in zip(its, ws):
                    v_new = it["u"][c0:c0 + CHUNK] - wsq[:CHUNK]
                    parts = [zeros_chunk] * (PAIR // CHUNK)
                    parts[c] = v_new
                    v_pads.append(jnp.concatenate(parts, axis=0).astype(BF16))
                ov = dot_pair(its[0]["qkk"][c], its[1]["qkk"][c], v_pads[0], v_pads[1])
                for j, h in enumerate((h0, h0 + 1)):
                    o_rows[h][d * (PAIR // CHUNK) + c] = ws[j][CHUNK:] + ov[j][:CHUNK]
                    gt = egl[row:row + 1, A_LANE0 + h:A_LANE0 + h + 1]
                    s_cur[h] = ss[j] * gt + ov[j][CHUNK:]
    for h in range(N_DN_HEADS):
        state[h] = s_cur[h]

    yield
    dn_parts = []
    for h in range(N_DN_HEADS):
        o = jnp.concatenate(o_rows[h], axis=0)
        zh = z[:, h * LANES:(h + 1) * LANES]
        dn_parts.append(((o * _rms_scale(o) * dnw_ref[...]) * _silu(zh)).astype(BF16))

    pos = (t * tt + lax.broadcasted_iota(jnp.int32, (tt, 1), 0) + 1).astype(F32)
    pool_parts = []
    for gi, win in enumerate(POOL_WINDOWS):
        cols = slice(gi * POOL_GROUP_DIM, (gi + 1) * POOL_GROUP_DIM)
        cur = p_ext[gi, POOL_HALO:POOL_HALO + tt, :]
        acc = cur
        for sft in range(1, win):
            acc = acc + p_ext[gi, pl.ds(POOL_HALO - sft, tt), :]
        mixed = acc / jnp.minimum(pos, float(win)) - cur
        y = _dot(mixed.astype(BF16), poolw_ref[gi]) * pscale_ref[:, cols]
        pool_parts.append(y.astype(BF16))

    yield
    x1 = x +_dot(jnp.concatenate(dn_parts + pool_parts, axis=1), wout_ref[...])
    x1_ref[...] = x1

    qkv_ext[:, 0:CONV_HALO, :] = qkv_ext[:, tt:tt + CONV_HALO, :]
    p_ext[:, 0:POOL_HALO, :] = p_ext[:, tt:tt + POOL_HALO, :]

    hn = x1 * _rms_scale(x1) * fnw_ref[...]
    hn_hi, hn_lo = _hi_lo(hn)
    hh = _dot(hn_hi, wr_ref[...])
    logits = hh[:, :LANES] + (hh[:, LANES:] + _dot(hn_lo, wr_ref[:, :LANES])) + br_ref[...]
    lt = logits.T[:N_EXPERTS]
    eidx = lax.broadcasted_iota(jnp.int32, (N_EXPERTS, tt), 0).astype(F32)
    work = lt
    vals, idxs = [], []
    for _ in range(TOP_K):
        m = jnp.max(work, axis=0, keepdims=True)
        idx = jnp.min(jnp.where(work == m, eidx, float(LANES)), axis=0, keepdims=True)
        vals.append(m)
        idxs.append(idx)
        work = jnp.where(eidx == idx, -jnp.inf, work)
    exps = [jnp.exp(v - vals[0]) for v in vals]
    denom = exps[0] + exps[1] + exps[2] + exps[3]

    yield
    onehot = jnp.zeros((N_EXPERTS, tt), F32)
    for idx in idxs:
        onehot = onehot + jnp.where(eidx == idx, 1.0, 0.0)
    earlier = jnp.where(ri < ci, 1.0, 0.0).astype(BF16)
    within = _dot(onehot.astype(BF16), earlier)
    tile_cnt = jnp.sum(onehot, axis=1, keepdims=True)
    n_copies = jnp.floor((tile_cnt + float(COPY_ROWS - 1)) * (1.0 / COPY_ROWS))
    er = lax.broadcasted_iota(jnp.int32, (N_EXPERTS, N_EXPERTS), 0)
    ec = lax.broadcasted_iota(jnp.int32, (N_EXPERTS, N_EXPERTS), 1)
    lower_expert = jnp.where(ec < er, 1.0, 0.0).astype(BF16)
    run_start = float(COPY_ROWS) * _dot(
        lower_expert, jnp.broadcast_to(n_copies, (N_EXPERTS, LANES)).astype(BF16))[:, 0:1]
    slot_of = run_start + within
    srow = lax.broadcasted_iota(jnp.int32, (SUBLANES, tt), 0)
    route = jnp.zeros((SUBLANES, tt), F32)
    gates = jnp.zeros((SUBLANES, tt), F32)
    for kk in range(TOP_K):
        slot = jnp.sum(jnp.where(eidx == idxs[kk], slot_of, 0.0), axis=0, keepdims=True)
        route = jnp.where(srow == kk, slot * float(ROW_SUBLANES), route)
        gates = jnp.where(srow == kk, exps[kk] / denom, gates)
    route_ref[...] = route.astype(jnp.int32)
    gate_ref[...] = gates
    mlane = lax.broadcasted_iota(jnp.int32, (N_EXPERTS, LANES), 1)
    meta_ref[...] = jnp.where(mlane == 0, tile_cnt, jnp.where(mlane == 1, cnt[...], 0.0))
    cnt[...] = cnt[...] + tile_cnt


def _mix_kernel(x_ref, anw_ref, win_ref, convw_ref, alog_ref, dtb_ref, dnw_ref, poolw_ref, pscale_ref,
                wout_ref, fnw_ref, wr_ref, br_ref,
                x1_ref, route_ref, gate_ref, meta_ref, xs_ref,
                qkv_ext, p_ext, state, cnt, zero_buf, zero_sem):
    b = pl.program_id(0)
    t = pl.program_id(1)
    first_step = jnp.logical_and(b == 0, t == 0)

    @pl.when(first_step)
    def _():
        zero_buf[...] = jnp.zeros(zero_buf.shape, F32)
        cnt[...] = jnp.zeros(cnt.shape, F32)
    fill_rows = zero_buf.shape[0]
    last_start = xs_ref.shape[0] - fill_rows
    step = b * pl.num_programs(1) + t

    def zero_copy(j):
        start = jnp.minimum((step * ZERO_COPIES + j) * fill_rows, last_start)
        return pltpu.make_async_copy(zero_buf, xs_ref.at[pl.ds(pl.multiple_of(start, SUBLANES), fill_rows)], zero_sem)
    for j in range(ZERO_COPIES):
        zero_copy(j).start()

    @pl.when(t == 0)
    def _():
        qkv_ext[:, :, 0:CONV_HALO, :] = jnp.zeros((SEQ_PER_STEP, 3 * N_DN_HEADS, CONV_HALO, LANES), F32)
        p_ext[:, :, 0:POOL_HALO, :] = jnp.zeros((SEQ_PER_STEP, N_POOL_GROUPS, POOL_HALO, LANES), F32)
        state[...] = jnp.zeros(state.shape, F32)

    tiles = [
        _mix_tile(t, x_ref.at[sq], anw_ref, win_ref, convw_ref, alog_ref, dtb_ref, dnw_ref, poolw_ref, pscale_ref,
                  wout_ref, fnw_ref, wr_ref, br_ref,
                  x1_ref.at[sq], route_ref.at[sq], gate_ref.at[sq], meta_ref.at[sq],
                  qkv_ext.at[sq], p_ext.at[sq], state.at[sq], cnt)
        for sq in range(SEQ_PER_STEP)]
    live = [True] * SEQ_PER_STEP
    tick = 0
    while any(live):
        for sq, gen in enumerate(tiles):
            if live[sq] and tick >= sq * PHASE_SKEW:
                live[sq] = next(gen, "done") != "done"
        tick += 1
    for j in range(ZERO_COPIES):
        zero_copy(j).wait()


def _mix_call(x, anw, win_r, convw, alog_p, dtb_p, dnw, poolw, pscale, wout, fnw, wr_p, br_p, xs_sublanes):
    bsz, seq, d = x.shape
    n_t = seq // TILE
    sps = SEQ_PER_STEP
    n_b = bsz // sps
    x = x.reshape(sps, n_b, seq, d)
    fill_rows = -(-xs_sublanes // (n_b * n_t * ZERO_COPIES * SUBLANES)) * SUBLANES
    const2 = lambda b, t: (0, 0)
    tok_map = lambda b, t: (0, b * n_t + t, 0)
    seq_map = lambda b, t: (0, b, t, 0)
    in_specs = [
        pl.BlockSpec((sps, None, TILE, d), seq_map),
        pl.BlockSpec(anw.shape, const2),
        pl.BlockSpec(win_r.shape, const2),
        pl.BlockSpec(convw.shape, const2),
        pl.BlockSpec(alog_p.shape, const2),
        pl.BlockSpec(dtb_p.shape, const2),
        pl.BlockSpec(dnw.shape, const2),
        pl.BlockSpec(poolw.shape, lambda b, t: (0, 0, 0)),
        pl.BlockSpec(pscale.shape, const2),
        pl.BlockSpec(wout.shape, const2),
        pl.BlockSpec(fnw.shape, const2),
        pl.BlockSpec(wr_p.shape, const2),
        pl.BlockSpec(br_p.shape, const2),
    ]
    out_shape = (
        jax.ShapeDtypeStruct((sps, n_b, seq, d), F32),
        jax.ShapeDtypeStruct((sps, n_b * n_t * SUBLANES, TILE), jnp.int32),
        jax.ShapeDtypeStruct((sps, n_b * n_t * SUBLANES, TILE), F32),
        jax.ShapeDtypeStruct((sps, n_b * n_t * N_EXPERTS, LANES), F32),
        jax.ShapeDtypeStruct((xs_sublanes, LANES), F32),
    )
    out_specs = (
        pl.BlockSpec((sps, None, TILE, d), seq_map),
        pl.BlockSpec((sps, SUBLANES, TILE), tok_map),
        pl.BlockSpec((sps, SUBLANES, TILE), tok_map),
        pl.BlockSpec((sps, N_EXPERTS, LANES), tok_map),
        pl.BlockSpec(memory_space=pl.ANY),
    )
    scratch = [
        pltpu.VMEM((sps, 3 * N_DN_HEADS, CONV_HALO + TILE, LANES), F32),
        pltpu.VMEM((sps, N_POOL_GROUPS, POOL_HALO + TILE, LANES), F32),
        pltpu.VMEM((sps, N_DN_HEADS, DN_HEAD_DIM, DN_HEAD_DIM), F32),
        pltpu.VMEM((N_EXPERTS, LANES), F32),
        pltpu.VMEM((fill_rows, LANES), F32),
        pltpu.SemaphoreType.DMA(()),
    ]
    return pl.pallas_call(
        _mix_kernel,
        grid=(n_b, n_t),
        in_specs=in_specs,
        out_specs=out_specs,
        out_shape=out_shape,
        scratch_shapes=scratch,
        compiler_params=pltpu.CompilerParams(
            dimension_semantics=("arbitrary", "arbitrary"), vmem_limit_bytes=VMEM_LIMIT_BYTES),
        name="mix",
    )(x, anw, win_r, convw, alog_p, dtb_p, dnw, poolw, pscale, wout, fnw, wr_p, br_p)


COPY_SUBLANES = COPY_ROWS * ROW_SUBLANES


COPY_TABLE = 256
assert SLOTS // COPY_ROWS < COPY_TABLE
LOOP_UNROLL = 8


def _for_count(n, fn):
    def main(i, carry):
        for u in range(LOOP_UNROLL):
            fn(i * LOOP_UNROLL + u)
        return carry
    n_main = lax.shift_right_logical(n, LOOP_UNROLL.bit_length() - 1)
    lax.fori_loop(0, n_main, main, 0)

    def tail(i, carry):
        fn(i)
        return carry
    lax.fori_loop(n_main * LOOP_UNROLL, n, tail, 0)


def _run_copies(dst_ref, local_buf, global_ref, sem, to_global):
    n = dst_ref[COPY_TABLE - 1]

    def start(c):
        loc = local_buf.at[pl.ds(pl.multiple_of(c * COPY_SUBLANES, COPY_SUBLANES), COPY_SUBLANES)]
        glo = global_ref.at[pl.ds(pl.multiple_of(dst_ref[c], SUBLANES), COPY_SUBLANES)]
        if to_global:
            pltpu.make_async_copy(loc, glo, sem).start()
        else:
            pltpu.make_async_copy(glo, loc, sem).start()
    _for_count(n, start)
    return n


def _wait_copies(n, local_buf, global_ref, sem):
    def wait(i):
        del i
        pltpu.make_async_copy(local_buf.at[pl.ds(0, COPY_SUBLANES)], global_ref.at[pl.ds(0, COPY_SUBLANES)], sem).wait()
    _for_count(n, wait)


def _dispatch_kernel(slot_ref, dst_ref, x1_ref, fnw_ref, xs_in_ref, xs_ref,
                     tok_tiles, sorted_buf, inflight, sem):
    del xs_in_ref
    step = pl.program_id(0)
    buf = sorted_buf.at[step % 2]
    x1 = x1_ref[...]
    hn = x1 * _rms_scale(x1) * fnw_ref[...]
    for s in range(ROW_SUBLANES):
        tok_tiles[pl.ds(s, TILE, stride=ROW_SUBLANES), :] = hn[:, s * LANES:(s + 1) * LANES]
    @pl.when(step < 2)
    def _():
        buf[...] = jnp.zeros(buf.shape, F32)

    def token_body(i, carry):
        tile = tok_tiles[pl.ds(pl.multiple_of(i * ROW_SUBLANES, ROW_SUBLANES), ROW_SUBLANES), :]
        for kk in range(TOP_K):
            slot_row = slot_ref[i * TOP_K + kk]
            buf[pl.ds(pl.multiple_of(slot_row, ROW_SUBLANES), ROW_SUBLANES), :] = tile
        return carry
    lax.fori_loop(0, TILE, token_body, 0, unroll=8)

    @pl.when(step > 0)
    def _():
        _wait_copies(inflight[0], buf, xs_ref, sem)
    inflight[0] = _run_copies(dst_ref, buf, xs_ref, sem, to_global=True)

    @pl.when(step == pl.num_programs(0) - 1)
    def _():
        _wait_copies(inflight[0], buf, xs_ref, sem)


def _smem_spec(n):
    return pl.BlockSpec((n,), lambda i: (i,), memory_space=pltpu.SMEM)


def _dispatch_call(slots, dst, x1_flat, fnw, xs_zero):
    n_tok, d = x1_flat.shape
    n_tiles = n_tok // TILE
    tile_of = lambda i: (i % SEQ_PER_STEP) * (n_tiles // SEQ_PER_STEP) + i // SEQ_PER_STEP
    return pl.pallas_call(
        _dispatch_kernel,
        grid=(n_tiles,),
        in_specs=[
            pl.BlockSpec((TILE * TOP_K,), lambda i: (tile_of(i),), memory_space=pltpu.SMEM),
            pl.BlockSpec((COPY_TABLE,), lambda i: (tile_of(i),), memory_space=pltpu.SMEM),
            pl.BlockSpec((TILE, d), lambda i: (tile_of(i), 0)),
            pl.BlockSpec(fnw.shape, lambda i: (0, 0)),
            pl.BlockSpec(memory_space=pl.ANY),
        ],
        out_specs=pl.BlockSpec(memory_space=pl.ANY),
        out_shape=jax.ShapeDtypeStruct(xs_zero.shape, F32),
        scratch_shapes=[pltpu.VMEM((TILE * ROW_SUBLANES, LANES), F32),
                        pltpu.VMEM((2, SLOTS * ROW_SUBLANES, LANES), F32),
                        pltpu.SMEM((1,), jnp.int32),
                        pltpu.SemaphoreType.DMA(())],
        input_output_aliases={4: 0},
        compiler_params=pltpu.CompilerParams(dimension_semantics=("arbitrary",)),
        name="dispatch",
    )(slots, dst, x1_flat, fnw, xs_zero)


def _combine_kernel(slot_ref, gatev_ref, dst_ref, dst_next_ref, x1_ref, fw_ref, ys_ref,
                    out_ref, sorted_buf, tok_tiles, inflight, sems):
    step = pl.program_id(0)
    cur = step % 2
    nxt = 1 - cur
    buf = sorted_buf.at[cur]

    @pl.when(step == 0)
    def _():
        inflight[0] = _run_copies(dst_ref, sorted_buf.at[0], ys_ref, sems.at[0], to_global=False)

    @pl.when(step + 1 < pl.num_programs(0))
    def _():
        inflight[nxt] = _run_copies(dst_next_ref, sorted_buf.at[nxt], ys_ref, sems.at[nxt], to_global=False)
    _wait_copies(inflight[cur], buf, ys_ref, sems.at[cur])

    def token_body(i, carry):
        acc = None
        for kk in range(TOP_K):
            slot_row = slot_ref[i * TOP_K + kk]
            row = buf[pl.ds(pl.multiple_of(slot_row, ROW_SUBLANES), ROW_SUBLANES), :]
            term = row * gatev_ref[i * TOP_K + kk]
            acc = term if acc is None else acc + term
        tok_tiles[pl.ds(pl.multiple_of(i * ROW_SUBLANES, ROW_SUBLANES), ROW_SUBLANES), :] = acc
        return carry
    lax.fori_loop(0, TILE, token_body, 0, unroll=8)

    x1 = x1_ref[...]
    cols = []
    ssq = None
    for s in range(ROW_SUBLANES):
        c = x1[:, s * LANES:(s + 1) * LANES] + tok_tiles[pl.ds(s, TILE, stride=ROW_SUBLANES), :]
        cols.append(c)
        sq = jnp.sum(c * c, axis=-1, keepdims=True)
        ssq = sq if ssq is None else ssq + sq
    scale = lax.rsqrt(ssq * (1.0 / D_MODEL) + RMS_EPS)
    for s in range(ROW_SUBLANES):
        out_ref[:, s * LANES:(s + 1) * LANES] = cols[s] * scale * fw_ref[:, s * LANES:(s + 1) * LANES]


def _combine_call(slots, gatev, dst, x1_flat, fw, ys):
    n_tok, d = x1_flat.shape
    n_steps = n_tok // TILE
    next_spec = pl.BlockSpec((COPY_TABLE,), lambda i: (jnp.minimum(i + 1, n_steps - 1),), memory_space=pltpu.SMEM)
    return pl.pallas_call(
        _combine_kernel,
        grid=(n_steps,),
        in_specs=[
            _smem_spec(TILE * TOP_K), _smem_spec(TILE * TOP_K), _smem_spec(COPY_TABLE), next_spec,
            pl.BlockSpec((TILE, d), lambda i: (i, 0)),
            pl.BlockSpec(fw.shape, lambda i: (0, 0)),
            pl.BlockSpec(memory_space=pl.ANY),
        ],
        out_specs=pl.BlockSpec((TILE, d), lambda i: (i, 0)),
        out_shape=jax.ShapeDtypeStruct((n_tok, d), F32),
        scratch_shapes=[pltpu.VMEM((2, SLOTS * ROW_SUBLANES, LANES), F32),
                        pltpu.VMEM((TILE * ROW_SUBLANES, LANES), F32),
                        pltpu.SMEM((2,), jnp.int32),
                        pltpu.SemaphoreType.DMA((2,))],
        compiler_params=pltpu.CompilerParams(dimension_semantics=("arbitrary",)),
        name="combine",
    )(slots, gatev, dst, dst, x1_flat, fw, ys)


BLOCK_SUBLANES = MOE_ROWS * ROW_SUBLANES


def _expert_mlp(x_tiles, y_tiles, wgu16, wd16, bgu_ref, bd_ref):
    xb = jnp.concatenate(
        [x_tiles[pl.ds(s, MOE_ROWS, stride=ROW_SUBLANES), :].astype(BF16) for s in range(ROW_SUBLANES)], axis=1)
    acc = None
    for j in range(D_EXPERT // FF_CHUNK):
        c0 = j * FF_CHUNK
        gate = _dot(xb, wgu16[:, c0:c0 + FF_CHUNK]) + bgu_ref[:, c0:c0 + FF_CHUNK]
        up = _dot(xb, wgu16[:, D_EXPERT + c0:D_EXPERT + c0 + FF_CHUNK]) + bgu_ref[:, D_EXPERT + c0:D_EXPERT + c0 + FF_CHUNK]
        gate = jnp.minimum(gate, SWIGLU_LIMIT)
        up = jnp.clip(up, -SWIGLU_LIMIT, SWIGLU_LIMIT)
        act = (up + 1.0) * gate * jax.nn.sigmoid(SWIGLU_ALPHA * gate)
        term = _dot(act.astype(BF16), wd16[c0:c0 + FF_CHUNK, :])
        acc = term if acc is None else acc + term
    y = acc + bd_ref[...]
    for s in range(ROW_SUBLANES):
        y_tiles[pl.ds(s, MOE_ROWS, stride=ROW_SUBLANES), :] = y[:, s * LANES:(s + 1) * LANES]


def _moe_kernel(first_ref, nblk_ref, tail_ref, xs_ref, wgu_ref, bgu_ref, wd_ref, bd_ref, ys_ref,
                wgu16, wd16, xbuf, ybuf, sem_in, sem_out):
    e = pl.program_id(0)
    n = nblk_ref[e]
    b0 = first_ref[e]

    def rows(j):
        return pl.ds(pl.multiple_of((b0 + j) * BLOCK_SUBLANES, BLOCK_SUBLANES), BLOCK_SUBLANES)

    def read(j, slot):
        return pltpu.make_async_copy(xs_ref.at[rows(j)], xbuf.at[slot], sem_in.at[slot])

    def write(j, slot):
        return pltpu.make_async_copy(ybuf.at[slot], ys_ref.at[rows(j)], sem_out.at[slot])

    @pl.when(n > 0)
    def _():
        read(0, 0).start()
        wgu16[...] = wgu_ref[...].astype(BF16)
        wd16[...] = wd_ref[...].astype(BF16)

        def body(j, carry):
            slot = j % 2
            read(j, slot).wait()

            @pl.when(j + 1 < n)
            def _():
                read(j + 1, 1 - slot).start()

            @pl.when(j >= 2)
            def _():
                write(j - 2, slot).wait()
            _expert_mlp(xbuf.at[slot], ybuf.at[slot], wgu16, wd16, bgu_ref, bd_ref)
            write(j, slot).start()
            return carry
        lax.fori_loop(0, n, body, 0)

        @pl.when(n >= 2)
        def _():
            write(n - 2, n % 2).wait()
        write(n - 1, (n - 1) % 2).wait()

    @pl.when(e == pl.num_programs(0) - 1)
    def _():
        ybuf[0] = jnp.zeros(ybuf.shape[1:], F32)
        first_tail = tail_ref[0]

        def zero_body(j, carry):
            dst = ys_ref.at[pl.ds(pl.multiple_of((first_tail + j) * BLOCK_SUBLANES, BLOCK_SUBLANES), BLOCK_SUBLANES)]
            cp = pltpu.make_async_copy(ybuf.at[0], dst, sem_out.at[0])
            cp.start()
            cp.wait()
            return carry
        lax.fori_loop(0, tail_ref[1], zero_body, 0)


def _moe_call(first_block, n_block, tail, xs, wgu, bgu, wd, bd):
    d = D_MODEL
    exp3 = lambda e, fb, nb, tl: (e, 0, 0)
    grid_spec = pltpu.PrefetchScalarGridSpec(
        num_scalar_prefetch=3,
        grid=(N_EXPERTS,),
        in_specs=[
            pl.BlockSpec(memory_space=pl.ANY),
            pl.BlockSpec((None, d, 2 * D_EXPERT), exp3),
            pl.BlockSpec((None, 1, 2 * D_EXPERT), exp3),
            pl.BlockSpec((None, D_EXPERT, d), exp3),
            pl.BlockSpec((None, 1, d), exp3),
        ],
        out_specs=pl.BlockSpec(memory_space=pl.ANY),
        scratch_shapes=[pltpu.VMEM((d, 2 * D_EXPERT), BF16), pltpu.VMEM((D_EXPERT, d), BF16),
                        pltpu.VMEM((2, BLOCK_SUBLANES, LANES), F32), pltpu.VMEM((2, BLOCK_SUBLANES, LANES), F32),
                        pltpu.SemaphoreType.DMA((2,)), pltpu.SemaphoreType.DMA((2,))],
    )
    return pl.pallas_call(
        _moe_kernel,
        grid_spec=grid_spec,
        out_shape=jax.ShapeDtypeStruct(xs.shape, F32),
        compiler_params=pltpu.CompilerParams(
            dimension_semantics=("arbitrary",), vmem_limit_bytes=VMEM_LIMIT_BYTES),
        name="moe",
    )(first_block, n_block, tail, xs, wgu, bgu, wd, bd)


def _pad_lanes(v, lane0):
    out = jnp.zeros((1, LANES), F32)
    return lax.dynamic_update_slice(out, v.reshape(1, -1).astype(F32), (0, lane0))


def kernel(x, attn_norm_w, w_in, conv_w, a_log, dt_bias, dn_norm_w, pool_w, pool_scale, w_out, ffn_norm_w,
           w_router, b_router, w_gate_up, b_gate_up, w_down, b_down, final_norm_w):
    bsz, seq, d = x.shape
    n_tok = bsz * seq
    assert w_in.shape[0] == 1, "the combine kernel fuses the final norm, so exactly one layer is supported"
    assert d == D_MODEL and seq % TILE == 0 and bsz % SEQ_PER_STEP == 0
    l = 0

    s_z, s_a = 4 * D_DN, 4 * D_DN + 2 * N_DN_HEADS
    ba_cols = jnp.pad(w_in[l][:, s_z:s_a], ((0, 0), (0, LANES - 2 * N_DN_HEADS)))
    win_r = jnp.concatenate([w_in[l][:, :s_z], w_in[l][:, s_a:], ba_cols], axis=1).astype(BF16)
    wr_hi, wr_lo = _hi_lo(jnp.pad(w_router[l], ((0, 0), (0, LANES - N_EXPERTS))))
    wr_p = jnp.concatenate([wr_hi, wr_lo], axis=1)
    br_p = _pad_lanes(b_router[l], 0)
    fnw = ffn_norm_w[l].reshape(1, d)

    n_blocks = (n_tok * TOP_K + N_EXPERTS * (MOE_ROWS + COPY_ROWS - 2)) // MOE_ROWS + 1
    n_rows = n_blocks * MOE_ROWS

    x1, route, gates, meta, xs_zero = _mix_call(
        x, attn_norm_w[l].reshape(1, d), win_r, conv_w[l], _pad_lanes(a_log[l], A_LANE0),
        _pad_lanes(dt_bias[l], A_LANE0), dn_norm_w[l].reshape(1, DN_HEAD_DIM), pool_w[l].astype(BF16),
        pool_scale[l].reshape(1, D_POOL), w_out[l].astype(BF16), fnw, wr_p, br_p, n_rows * ROW_SUBLANES)

    n_tiles = n_tok // TILE
    meta = meta.reshape(n_tiles, N_EXPERTS, LANES)[:, :, :2].astype(jnp.int32)
    tile_cnt, tile_before = meta[:, :, 0], meta[:, :, 1]
    total = jnp.sum(tile_cnt, axis=0)
    padded = jnp.where(total > 0, ((total + COPY_ROWS - 1 + MOE_ROWS - 1) // MOE_ROWS) * MOE_ROWS, 0)
    pad_end = jnp.cumsum(padded)
    pad_start = pad_end - padded
    run_base = pad_start[None, :] + tile_before
    ncopy = (tile_cnt + COPY_ROWS - 1) // COPY_ROWS
    copy_end = jnp.cumsum(ncopy, axis=1)
    copy_start = copy_end - ncopy
    c_idx = jnp.arange(COPY_TABLE, dtype=jnp.int32)[None, :, None]
    in_run = jnp.logical_and(c_idx >= copy_start[:, None, :], c_idx < copy_end[:, None, :])
    dst_row = jnp.sum(jnp.where(in_run, run_base[:, None, :] + COPY_ROWS * (c_idx - copy_start[:, None, :]), 0), axis=2)
    dst = (dst_row * ROW_SUBLANES).at[:, COPY_TABLE - 1].set(copy_end[:, -1]).reshape(-1)
    first_block = (pad_start // MOE_ROWS).astype(jnp.int32)
    n_block = (padded // MOE_ROWS).astype(jnp.int32)
    n_used = pad_end[-1] // MOE_ROWS
    tail = jnp.stack([n_used, n_blocks - n_used]).astype(jnp.int32)

    def token_major(a):
        return a.reshape(n_tiles, SUBLANES, TILE)[:, :TOP_K].transpose(0, 2, 1).reshape(-1)
    slots = token_major(route)
    gatev = token_major(gates)

    x1_flat = x1.reshape(n_tok, d)
    xs = _dispatch_call(slots, dst, x1_flat, fnw, xs_zero)
    ys = _moe_call(first_block, n_block, tail, xs, w_gate_up[l], b_gate_up[l].reshape(N_EXPERTS, 1, -1),
                   w_down[l], b_down[l].reshape(N_EXPERTS, 1, -1))
    out = _combine_call(slots, gatev, dst, x1_flat, final_norm_w.reshape(1, d), ys)
    return out.reshape(bsz, seq, d)
```

```python
import jax
import jax.numpy as jnp
from jax import lax
from jax.experimental import pallas as pl
from jax.experimental.pallas import tpu as pltpu

F32 = jnp.float32
BF16 = jnp.bfloat16

D_MODEL = 1024
N_DN_HEADS = 4
DN_HEAD_DIM = 128
D_DN = N_DN_HEADS * DN_HEAD_DIM
CONV_K = 4
CHUNK = 64
POOL_WINDOWS = (2, 4, 8, 16)
N_POOL_GROUPS = len(POOL_WINDOWS)
D_POOL = D_MODEL - D_DN
POOL_GROUP_DIM = D_POOL // N_POOL_GROUPS
N_EXPERTS = 32
TOP_K = 4
D_EXPERT = D_MODEL
SWIGLU_LIMIT = 7.0
SWIGLU_ALPHA = 1.702
RMS_EPS = 1e-6
L2_EPS = 1e-6

LANES = 128
SUBLANES = 8
ROW_SUBLANES = D_MODEL // LANES
VMEM_LIMIT_BYTES = 56 * 1024 * 1024

TILE = 256
PAIR = 2 * CHUNK
CONV_HALO = SUBLANES
POOL_HALO = 2 * SUBLANES
MOE_ROWS = 512
FF_CHUNK = 512
COPY_ROWS = 8
ZERO_COPIES = 4
SEQ_PER_STEP = 2
PHASE_SKEW = 0
SLOTS = TILE * TOP_K + N_EXPERTS * COPY_ROWS
assert TILE * TOP_K // COPY_ROWS + 1 <= 256

COL_QKV = 0
COL_Z = 3 * D_DN
COL_P = 4 * D_DN
COL_BA = 4 * D_DN + D_POOL
D_IN_R = COL_BA + LANES
A_LANE0 = N_DN_HEADS


def _dot(a, b):
    return jnp.dot(a, b, preferred_element_type=F32)


def _dot_nt(a, b):
    return lax.dot_general(a, b, (((1,), (1,)), ((), ())), preferred_element_type=F32)


def _hi_lo(x):
    hi = x.astype(BF16)
    lo = (x - hi.astype(F32)).astype(BF16)
    return hi, lo


def _dot_f32(a, b):
    ah, al = _hi_lo(a)
    bh, bl = _hi_lo(b)
    return _dot(ah, bh) + (_dot(ah, bl) + _dot(al, bh))


def _dot_exact_lhs(m_bf16, x):
    hi = x.astype(BF16)
    r = x - hi.astype(F32)
    mid = r.astype(BF16)
    lo = (r - mid.astype(F32)).astype(BF16)
    return _dot(m_bf16, hi) + (_dot(m_bf16, mid) + _dot(m_bf16, lo))


def _rms_scale(x):
    return lax.rsqrt(jnp.mean(x * x, axis=-1, keepdims=True) + RMS_EPS)


def _silu(x):
    return x * jax.nn.sigmoid(x)


def _softplus(x):
    return jnp.maximum(x, 0.0) + jnp.log1p(jnp.exp(-jnp.abs(x)))


def _mix_tile(t, x_ref, anw_ref, win_ref, convw_ref, alog_ref, dtb_ref, dnw_ref, poolw_ref, pscale_ref,
              wout_ref, fnw_ref, wr_ref, br_ref,
              x1_ref, route_ref, gate_ref, meta_ref,
              qkv_ext, p_ext, state, cnt):
    tt = TILE

    x = x_ref[...]
    hb = (x * _rms_scale(x) * anw_ref[...]).astype(BF16)
    qkv_pre = _dot(hb, win_ref[:, COL_QKV:COL_Z])
    for c in range(3 * N_DN_HEADS):
        qkv_ext[c, CONV_HALO:CONV_HALO + tt, :] = qkv_pre[:, c * LANES:(c + 1) * LANES]
    yield
    z = _dot(hb, win_ref[:, COL_Z:COL_P])
    p_pre = _dot(hb, win_ref[:, COL_P:COL_BA])
    for c in range(N_POOL_GROUPS):
        p_ext[c, POOL_HALO:POOL_HALO + tt, :] = p_pre[:, c * LANES:(c + 1) * LANES]
    ba = _dot(hb, win_ref[:, COL_BA:D_IN_R])

    yield
    beta = jax.nn.sigmoid(ba)
    g = -jnp.exp(alog_ref[...]) * _softplus(ba + dtb_ref[...])
    ri = lax.broadcasted_iota(jnp.int32, (tt, tt), 0)
    ci = lax.broadcasted_iota(jnp.int32, (tt, tt), 1)
    same_chunk = (ri // CHUNK) == (ci // CHUNK)
    l_incl = jnp.where(jnp.logical_and(same_chunk, ri >= ci), 1.0, 0.0).astype(BF16)
    gc = _dot_exact_lhs(l_incl, g)
    gl = jnp.concatenate(
        [jnp.broadcast_to(gc[c * CHUNK + CHUNK - 1:c * CHUNK + CHUNK, :], (CHUNK, LANES)) for c in range(tt // CHUNK)],
        axis=0)
    eg = jnp.exp(gc)
    ekd = jnp.exp(gl - gc)
    egl = jnp.exp(gl)
    gct = gc.T

    pr = lax.broadcasted_iota(jnp.int32, (PAIR, PAIR), 0)
    pc = lax.broadcasted_iota(jnp.int32, (PAIR, PAIR), 1)
    pair_same = (pr // CHUNK) == (pc // CHUNK)
    causal = jnp.logical_and(pair_same, pr >= pc)
    strict = jnp.logical_and(pair_same, pr > pc)
    eye = jnp.where(pr == pc, 1.0, 0.0).astype(F32)
    zeros_chunk = jnp.zeros((CHUNK, DN_HEAD_DIM), F32)

    def conv_silu(col):
        acc = None
        for j in range(CONV_K):
            rows = pl.ds(CONV_HALO - (CONV_K - 1) + j, tt)
            term = qkv_ext[col, rows, :] * convw_ref[j:j + 1, col * LANES:(col + 1) * LANES]
            acc = term if acc is None else acc + term
        return _silu(acc)

    def l2n(v):
        return v * lax.rsqrt(jnp.sum(v * v, axis=-1, keepdims=True) + L2_EPS)

    def block_diag(b0, b1):
        zero = jnp.zeros_like(b0)
        return jnp.concatenate([jnp.concatenate([b0, zero], axis=1), jnp.concatenate([zero, b1], axis=1)], axis=0)

    def dot_pair(a0, a1, b0, b1):
        out = _dot(jnp.concatenate([a0, a1], axis=1), block_diag(b0, b1))
        return out[:, :LANES], out[:, LANES:]

    n_pairs = tt // PAIR
    assert n_pairs == 2 and N_DN_HEADS % 2 == 0
    insts = []
    for h in range(N_DN_HEADS):
        yield
        q_all = l2n(conv_silu(h)) * (DN_HEAD_DIM ** -0.5)
        k_all = l2n(conv_silu(N_DN_HEADS + h))
        v_all = conv_silu(2 * N_DN_HEADS + h)
        yield
        la = A_LANE0 + h
        pre = []
        for d in range(n_pairs):
            r0 = d * PAIR
            q = q_all[r0:r0 + PAIR]
            k = k_all[r0:r0 + PAIR]
            v = v_all[r0:r0 + PAIR]
            bcol = beta[r0:r0 + PAIR, h:h + 1]
            gcol = gc[r0:r0 + PAIR, la:la + 1]
            egcol = eg[r0:r0 + PAIR, la:la + 1]
            ekdcol = ekd[r0:r0 + PAIR, la:la + 1]
            grow = gct[la:la + 1, r0:r0 + PAIR]
            diff = gcol - grow
            decay = jnp.where(causal, jnp.exp(jnp.where(causal, diff, 0.0)), 0.0)
            kb = k * bcol
            pre.append(dict(
                decay=decay, kq=jnp.concatenate([kb, q], axis=0).astype(BF16), kt=k.T.astype(BF16),
                rhs=jnp.concatenate([(v * bcol).astype(BF16), (kb * egcol).astype(BF16)], axis=1),
                qd=(q * egcol).astype(BF16), kdt=(k * ekdcol).T.astype(BF16)))
        kk0, kk1 = dot_pair(pre[0]["kq"], pre[1]["kq"], pre[0]["kt"], pre[1]["kt"])
        for d, kk in enumerate((kk0, kk1)):
            it = pre[d]
            it["a"] = jnp.where(strict, kk[:PAIR] * it["decay"], 0.0)
            it["qk"] = jnp.where(causal, kk[PAIR:] * it["decay"], 0.0).astype(BF16)
            insts.append(it)

    rs = [-it["a"] for it in insts]
    p16 = [it["a"].astype(BF16) for it in insts]
    for _ in range(CHUNK.bit_length() - 2):
        yield
        pws = []
        for i in range(0, len(insts), 2):
            pws.extend(dot_pair(p16[i], p16[i + 1], p16[i], p16[i + 1]))
        p16 = [pw.astype(BF16) for pw in pws]
        yield
        rps = []
        for i in range(0, len(insts), 2):
            rps.extend(dot_pair(rs[i].astype(BF16), rs[i + 1].astype(BF16), p16[i], p16[i + 1]))
        rs = [r + pw + rp for r, pw, rp in zip(rs, pws, rps)]
    yield
    for it, r in zip(insts, rs):
        uw = _dot((eye + r).astype(BF16), it["rhs"])
        it["u"] = uw[:, :DN_HEAD_DIM]
        it["wq"] = [jnp.concatenate([uw[c0:c0 + CHUNK, DN_HEAD_DIM:].astype(BF16), it["qd"][c0:c0 + CHUNK]], axis=0)
                    for c0 in range(0, PAIR, CHUNK)]
        it["qkk"] = [jnp.concatenate([it["qk"][c0:c0 + CHUNK], it["kdt"]], axis=0)
                     for c0 in range(0, PAIR, CHUNK)]

    s_cur = [state[h] for h in range(N_DN_HEADS)]
    o_rows = [[None] * (tt // CHUNK) for _ in range(N_DN_HEADS)]
    for d in range(n_pairs):
        for c in range(PAIR // CHUNK):
            yield
            c0 = c * CHUNK
            row = d * PAIR + c0
            for h0 in range(0, N_DN_HEADS, 2):
                its = [insts[h * n_pairs + d] for h in (h0, h0 + 1)]
                ss = [s_cur[h0], s_cur[h0 + 1]]
                ws = dot_pair(its[0]["wq"][c], its[1]["wq"][c], ss[0].astype(BF16), ss[1].astype(BF16))
                v_pads = []
                for it, wsq in zip(its, ws):
                    v_new = it["u"][c0:c0 + CHUNK] - wsq[:CHUNK]
                    parts = [zeros_chunk] * (PAIR // CHUNK)
                    parts[c] = v_new
                    v_pads.append(jnp.concatenate(parts, axis=0).astype(BF16))
                ov = dot_pair(its[0]["qkk"][c], its[1]["qkk"][c], v_pads[0], v_pads[1])
                for j, h in enumerate((h0, h0 + 1)):
                    o_rows[h][d * (PAIR // CHUNK) + c] = ws[j][CHUNK:] + ov[j][:CHUNK]
                    gt = egl[row:row + 1, A_LANE0 + h:A_LANE0 + h + 1]
                    s_cur[h] = ss[j] * gt + ov[j][CHUNK:]
    for h in range(N_DN_HEADS):
        state[h] = s_cur[h]

    yield
    dn_parts = []
    for h in range(N_DN_HEADS):
        o = jnp.concatenate(o_rows[h], axis=0)
        zh = z[:, h * LANES:(h + 1) * LANES]
        dn_parts.append(((o * _rms_scale(o) * dnw_ref[...]) * _silu(zh)).astype(BF16))

    pos = (t * tt + lax.broadcasted_iota(jnp.int32, (tt, 1), 0) + 1).astype(F32)
    pool_parts = []
    for gi, win in enumerate(POOL_WINDOWS):
        yield
        cols = slice(gi * POOL_GROUP_DIM, (gi + 1) * POOL_GROUP_DIM)
        cur = p_ext[gi, POOL_HALO:POOL_HALO + tt, :]
        acc = cur
        for sft in range(1, win):
            acc = acc + p_ext[gi, pl.ds(POOL_HALO - sft, tt), :]
        mixed = acc / jnp.minimum(pos, float(win)) - cur
        y = _dot(mixed.astype(BF16), poolw_ref[gi]) * pscale_ref[:, cols]
        pool_parts.append(y.astype(BF16))

    yield
    x1 = x +_dot(jnp.concatenate(dn_parts + pool_parts, axis=1), wout_ref[...])
    x1_ref[...] = x1

    qkv_ext[:, 0:CONV_HALO, :] = qkv_ext[:, tt:tt + CONV_HALO, :]
    p_ext[:, 0:POOL_HALO, :] = p_ext[:, tt:tt + POOL_HALO, :]

    hn = x1 * _rms_scale(x1) * fnw_ref[...]
    hn_hi, hn_lo = _hi_lo(hn)
    hh = _dot(hn_hi, wr_ref[...])
    logits = hh[:, :LANES] + (hh[:, LANES:] + _dot(hn_lo, wr_ref[:, :LANES])) + br_ref[...]
    yield
    lt = logits.T[:N_EXPERTS]
    eidx = lax.broadcasted_iota(jnp.int32, (N_EXPERTS, tt), 0).astype(F32)
    work = lt
    vals, idxs = [], []
    for _ in range(TOP_K):
        m = jnp.max(work, axis=0, keepdims=True)
        idx = jnp.min(jnp.where(work == m, eidx, float(LANES)), axis=0, keepdims=True)
        vals.append(m)
        idxs.append(idx)
        work = jnp.where(eidx == idx, -jnp.inf, work)
    exps = [jnp.exp(v - vals[0]) for v in vals]
    denom = exps[0] + exps[1] + exps[2] + exps[3]

    yield
    onehot = jnp.zeros((N_EXPERTS, tt), F32)
    for idx in idxs:
        onehot = onehot + jnp.where(eidx == idx, 1.0, 0.0)
    earlier = jnp.where(ri < ci, 1.0, 0.0).astype(BF16)
    within = _dot(onehot.astype(BF16), earlier)
    tile_cnt = jnp.sum(onehot, axis=1, keepdims=True)
    n_copies = jnp.floor((tile_cnt + float(COPY_ROWS - 1)) * (1.0 / COPY_ROWS))
    er = lax.broadcasted_iota(jnp.int32, (N_EXPERTS, N_EXPERTS), 0)
    ec = lax.broadcasted_iota(jnp.int32, (N_EXPERTS, N_EXPERTS), 1)
    lower_expert = jnp.where(ec < er, 1.0, 0.0).astype(BF16)
    run_start = float(COPY_ROWS) * _dot(
        lower_expert, jnp.broadcast_to(n_copies, (N_EXPERTS, LANES)).astype(BF16))[:, 0:1]
    slot_of = run_start + within
    srow = lax.broadcasted_iota(jnp.int32, (SUBLANES, tt), 0)
    route = jnp.zeros((SUBLANES, tt), F32)
    gates = jnp.zeros((SUBLANES, tt), F32)
    for kk in range(TOP_K):
        slot = jnp.sum(jnp.where(eidx == idxs[kk], slot_of, 0.0), axis=0, keepdims=True)
        route = jnp.where(srow == kk, slot * float(ROW_SUBLANES), route)
        gates = jnp.where(srow == kk, exps[kk] / denom, gates)
    route_ref[...] = route.astype(jnp.int32)
    gate_ref[...] = gates
    mlane = lax.broadcasted_iota(jnp.int32, (N_EXPERTS, LANES), 1)
    meta_ref[...] = jnp.where(mlane == 0, tile_cnt, jnp.where(mlane == 1, cnt[...], 0.0))
    cnt[...] = cnt[...] + tile_cnt


def _mix_kernel(x_ref, anw_ref, win_ref, convw_ref, alog_ref, dtb_ref, dnw_ref, poolw_ref, pscale_ref,
                wout_ref, fnw_ref, wr_ref, br_ref,
                x1_ref, route_ref, gate_ref, meta_ref, xs_ref,
                qkv_ext, p_ext, state, cnt, zero_buf, zero_sem):
    b = pl.program_id(0)
    t = pl.program_id(1)
    first_step = jnp.logical_and(b == 0, t == 0)

    @pl.when(first_step)
    def _():
        zero_buf[...] = jnp.zeros(zero_buf.shape, F32)
        cnt[...] = jnp.zeros(cnt.shape, F32)
    fill_rows = zero_buf.shape[0]
    last_start = xs_ref.shape[0] - fill_rows
    step = b * pl.num_programs(1) + t

    def zero_copy(j):
        start = jnp.minimum((step * ZERO_COPIES + j) * fill_rows, last_start)
        return pltpu.make_async_copy(zero_buf, xs_ref.at[pl.ds(pl.multiple_of(start, SUBLANES), fill_rows)], zero_sem)
    for j in range(ZERO_COPIES):
        zero_copy(j).start()

    @pl.when(t == 0)
    def _():
        qkv_ext[:, :, 0:CONV_HALO, :] = jnp.zeros((SEQ_PER_STEP, 3 * N_DN_HEADS, CONV_HALO, LANES), F32)
        p_ext[:, :, 0:POOL_HALO, :] = jnp.zeros((SEQ_PER_STEP, N_POOL_GROUPS, POOL_HALO, LANES), F32)
        state[...] = jnp.zeros(state.shape, F32)

    tiles = [
        _mix_tile(t, x_ref.at[sq], anw_ref, win_ref, convw_ref, alog_ref, dtb_ref, dnw_ref, poolw_ref, pscale_ref,
                  wout_ref, fnw_ref, wr_ref, br_ref,
                  x1_ref.at[sq], route_ref.at[sq], gate_ref.at[sq], meta_ref.at[sq],
                  qkv_ext.at[sq], p_ext.at[sq], state.at[sq], cnt)
        for sq in range(SEQ_PER_STEP)]
    live = [True] * SEQ_PER_STEP
    tick = 0
    while any(live):
        for sq, gen in enumerate(tiles):
            if live[sq] and tick >= sq * PHASE_SKEW:
                live[sq] = next(gen, "done") != "done"
        tick += 1
    for j in range(ZERO_COPIES):
        zero_copy(j).wait()


def _mix_call(x, anw, win_r, convw, alog_p, dtb_p, dnw, poolw, pscale, wout, fnw, wr_p, br_p, xs_sublanes):
    bsz, seq, d = x.shape
    n_t = seq // TILE
    sps = SEQ_PER_STEP
    n_b = bsz // sps
    x = x.reshape(sps, n_b, seq, d)
    fill_rows = -(-xs_sublanes // (n_b * n_t * ZERO_COPIES * SUBLANES)) * SUBLANES
    const2 = lambda b, t: (0, 0)
    tok_map = lambda b, t: (0, b * n_t + t, 0)
    seq_map = lambda b, t: (0, b, t, 0)
    in_specs = [
        pl.BlockSpec((sps, None, TILE, d), seq_map),
        pl.BlockSpec(anw.shape, const2),
        pl.BlockSpec(win_r.shape, const2),
        pl.BlockSpec(convw.shape, const2),
        pl.BlockSpec(alog_p.shape, const2),
        pl.BlockSpec(dtb_p.shape, const2),
        pl.BlockSpec(dnw.shape, const2),
        pl.BlockSpec(poolw.shape, lambda b, t: (0, 0, 0)),
        pl.BlockSpec(pscale.shape, const2),
        pl.BlockSpec(wout.shape, const2),
        pl.BlockSpec(fnw.shape, const2),
        pl.BlockSpec(wr_p.shape, const2),
        pl.BlockSpec(br_p.shape, const2),
    ]
    out_shape = (
        jax.ShapeDtypeStruct((sps, n_b, seq, d), F32),
        jax.ShapeDtypeStruct((sps, n_b * n_t * SUBLANES, TILE), jnp.int32),
        jax.ShapeDtypeStruct((sps, n_b * n_t * SUBLANES, TILE), F32),
        jax.ShapeDtypeStruct((sps, n_b * n_t * N_EXPERTS, LANES), F32),
        jax.ShapeDtypeStruct((xs_sublanes, LANES), F32),
    )
    out_specs = (
        pl.BlockSpec((sps, None, TILE, d), seq_map),
        pl.BlockSpec((sps, SUBLANES, TILE), tok_map),
        pl.BlockSpec((sps, SUBLANES, TILE), tok_map),
        pl.BlockSpec((sps, N_EXPERTS, LANES), tok_map),
        pl.BlockSpec(memory_space=pl.ANY),
    )
    scratch = [
        pltpu.VMEM((sps, 3 * N_DN_HEADS, CONV_HALO + TILE, LANES), F32),
        pltpu.VMEM((sps, N_POOL_GROUPS, POOL_HALO + TILE, LANES), F32),
        pltpu.VMEM((sps, N_DN_HEADS, DN_HEAD_DIM, DN_HEAD_DIM), F32),
        pltpu.VMEM((N_EXPERTS, LANES), F32),
        pltpu.VMEM((fill_rows, LANES), F32),
        pltpu.SemaphoreType.DMA(()),
    ]
    return pl.pallas_call(
        _mix_kernel,
        grid=(n_b, n_t),
        in_specs=in_specs,
        out_specs=out_specs,
        out_shape=out_shape,
        scratch_shapes=scratch,
        compiler_params=pltpu.CompilerParams(
            dimension_semantics=("arbitrary", "arbitrary"), vmem_limit_bytes=VMEM_LIMIT_BYTES),
        name="mix",
    )(x, anw, win_r, convw, alog_p, dtb_p, dnw, poolw, pscale, wout, fnw, wr_p, br_p)


COPY_SUBLANES = COPY_ROWS * ROW_SUBLANES


COPY_TABLE = 256
assert SLOTS // COPY_ROWS < COPY_TABLE
LOOP_UNROLL = 8


def _for_count(n, fn):
    def main(i, carry):
        for u in range(LOOP_UNROLL):
            fn(i * LOOP_UNROLL + u)
        return carry
    n_main = lax.shift_right_logical(n, LOOP_UNROLL.bit_length() - 1)
    lax.fori_loop(0, n_main, main, 0)

    def tail(i, carry):
        fn(i)
        return carry
    lax.fori_loop(n_main * LOOP_UNROLL, n, tail, 0)


def _run_copies(dst_ref, local_buf, global_ref, sem, to_global):
    n = dst_ref[COPY_TABLE - 1]

    def start(c):
        loc = local_buf.at[pl.ds(pl.multiple_of(c * COPY_SUBLANES, COPY_SUBLANES), COPY_SUBLANES)]
        glo = global_ref.at[pl.ds(pl.multiple_of(dst_ref[c], SUBLANES), COPY_SUBLANES)]
        if to_global:
            pltpu.make_async_copy(loc, glo, sem).start()
        else:
            pltpu.make_async_copy(glo, loc, sem).start()
    _for_count(n, start)
    return n


def _wait_copies(n, local_buf, global_ref, sem):
    def wait(i):
        del i
        pltpu.make_async_copy(local_buf.at[pl.ds(0, COPY_SUBLANES)], global_ref.at[pl.ds(0, COPY_SUBLANES)], sem).wait()
    _for_count(n, wait)


def _dispatch_kernel(slot_ref, dst_ref, x1_ref, fnw_ref, xs_in_ref, xs_ref,
                     tok_tiles, sorted_buf, inflight, sem):
    del xs_in_ref
    step = pl.program_id(0)
    buf = sorted_buf.at[step % 2]
    x1 = x1_ref[...]
    hn = x1 * _rms_scale(x1) * fnw_ref[...]
    for s in range(ROW_SUBLANES):
        tok_tiles[pl.ds(s, TILE, stride=ROW_SUBLANES), :] = hn[:, s * LANES:(s + 1) * LANES]
    @pl.when(step < 2)
    def _():
        buf[...] = jnp.zeros(buf.shape, F32)

    def token_body(i, carry):
        tile = tok_tiles[pl.ds(pl.multiple_of(i * ROW_SUBLANES, ROW_SUBLANES), ROW_SUBLANES), :]
        for kk in range(TOP_K):
            slot_row = slot_ref[kk * TILE + i]
            buf[pl.ds(pl.multiple_of(slot_row, ROW_SUBLANES), ROW_SUBLANES), :] = tile
        return carry
    lax.fori_loop(0, TILE, token_body, 0, unroll=8)

    @pl.when(step > 0)
    def _():
        _wait_copies(inflight[0], buf, xs_ref, sem)
    inflight[0] = _run_copies(dst_ref, buf, xs_ref, sem, to_global=True)

    @pl.when(step == pl.num_programs(0) - 1)
    def _():
        _wait_copies(inflight[0], buf, xs_ref, sem)


def _smem_spec(n):
    return pl.BlockSpec((n,), lambda i: (i,), memory_space=pltpu.SMEM)


def _dispatch_call(slots, dst, x1_flat, fnw, xs_zero):
    n_tok, d = x1_flat.shape
    n_tiles = n_tok // TILE
    tile_of = lambda i: (i % SEQ_PER_STEP) * (n_tiles // SEQ_PER_STEP) + i // SEQ_PER_STEP
    return pl.pallas_call(
        _dispatch_kernel,
        grid=(n_tiles,),
        in_specs=[
            pl.BlockSpec((TILE * TOP_K,), lambda i: (tile_of(i),), memory_space=pltpu.SMEM),
            pl.BlockSpec((COPY_TABLE,), lambda i: (tile_of(i),), memory_space=pltpu.SMEM),
            pl.BlockSpec((TILE, d), lambda i: (tile_of(i), 0)),
            pl.BlockSpec(fnw.shape, lambda i: (0, 0)),
            pl.BlockSpec(memory_space=pl.ANY),
        ],
        out_specs=pl.BlockSpec(memory_space=pl.ANY),
        out_shape=jax.ShapeDtypeStruct(xs_zero.shape, F32),
        scratch_shapes=[pltpu.VMEM((TILE * ROW_SUBLANES, LANES), F32),
                        pltpu.VMEM((2, SLOTS * ROW_SUBLANES, LANES), F32),
                        pltpu.SMEM((1,), jnp.int32),
                        pltpu.SemaphoreType.DMA(())],
        input_output_aliases={4: 0},
        compiler_params=pltpu.CompilerParams(dimension_semantics=("arbitrary",)),
        name="dispatch",
    )(slots, dst, x1_flat, fnw, xs_zero)


def _combine_kernel(slot_ref, gatev_ref, dst_ref, dst_next_ref, x1_ref, fw_ref, ys_ref,
                    out_ref, sorted_buf, tok_tiles, inflight, sems):
    step = pl.program_id(0)
    cur = step % 2
    nxt = 1 - cur
    buf = sorted_buf.at[cur]

    @pl.when(step == 0)
    def _():
        inflight[0] = _run_copies(dst_ref, sorted_buf.at[0], ys_ref, sems.at[0], to_global=False)

    @pl.when(step + 1 < pl.num_programs(0))
    def _():
        inflight[nxt] = _run_copies(dst_next_ref, sorted_buf.at[nxt], ys_ref, sems.at[nxt], to_global=False)
    _wait_copies(inflight[cur], buf, ys_ref, sems.at[cur])

    def token_body(i, carry):
        acc = None
        for kk in range(TOP_K):
            slot_row = slot_ref[kk * TILE + i]
            row = buf[pl.ds(pl.multiple_of(slot_row, ROW_SUBLANES), ROW_SUBLANES), :]
            term = row * gatev_ref[kk * TILE + i]
            acc = term if acc is None else acc + term
        tok_tiles[pl.ds(pl.multiple_of(i * ROW_SUBLANES, ROW_SUBLANES), ROW_SUBLANES), :] = acc
        return carry
    lax.fori_loop(0, TILE, token_body, 0, unroll=8)

    x1 = x1_ref[...]
    cols = []
    ssq = None
    for s in range(ROW_SUBLANES):
        c = x1[:, s * LANES:(s + 1) * LANES] + tok_tiles[pl.ds(s, TILE, stride=ROW_SUBLANES), :]
        cols.append(c)
        sq = jnp.sum(c * c, axis=-1, keepdims=True)
        ssq = sq if ssq is None else ssq + sq
    scale = lax.rsqrt(ssq * (1.0 / D_MODEL) + RMS_EPS)
    for s in range(ROW_SUBLANES):
        out_ref[:, s * LANES:(s + 1) * LANES] = cols[s] * scale * fw_ref[:, s * LANES:(s + 1) * LANES]


def _combine_call(slots, gatev, dst, x1_flat, fw, ys):
    n_tok, d = x1_flat.shape
    n_steps = n_tok // TILE
    next_spec = pl.BlockSpec((COPY_TABLE,), lambda i: (jnp.minimum(i + 1, n_steps - 1),), memory_space=pltpu.SMEM)
    return pl.pallas_call(
        _combine_kernel,
        grid=(n_steps,),
        in_specs=[
            _smem_spec(TILE * TOP_K), _smem_spec(TILE * TOP_K), _smem_spec(COPY_TABLE), next_spec,
            pl.BlockSpec((TILE, d), lambda i: (i, 0)),
            pl.BlockSpec(fw.shape, lambda i: (0, 0)),
            pl.BlockSpec(memory_space=pl.ANY),
        ],
        out_specs=pl.BlockSpec((TILE, d), lambda i: (i, 0)),
        out_shape=jax.ShapeDtypeStruct((n_tok, d), F32),
        scratch_shapes=[pltpu.VMEM((2, SLOTS * ROW_SUBLANES, LANES), F32),
                        pltpu.VMEM((TILE * ROW_SUBLANES, LANES), F32),
                        pltpu.SMEM((2,), jnp.int32),
                        pltpu.SemaphoreType.DMA((2,))],
        compiler_params=pltpu.CompilerParams(dimension_semantics=("arbitrary",)),
        name="combine",
    )(slots, gatev, dst, dst, x1_flat, fw, ys)


BLOCK_SUBLANES = MOE_ROWS * ROW_SUBLANES


def _expert_mlp(x_tiles, y_tiles, wgu16, wd16, bgu_ref, bd_ref):
    xb = jnp.concatenate(
        [x_tiles[pl.ds(s, MOE_ROWS, stride=ROW_SUBLANES), :].astype(BF16) for s in range(ROW_SUBLANES)], axis=1)
    acc = None
    for j in range(D_EXPERT // FF_CHUNK):
        c0 = j * FF_CHUNK
        gate = _dot(xb, wgu16[:, c0:c0 + FF_CHUNK]) + bgu_ref[:, c0:c0 + FF_CHUNK]
        up = _dot(xb, wgu16[:, D_EXPERT + c0:D_EXPERT + c0 + FF_CHUNK]) + bgu_ref[:, D_EXPERT + c0:D_EXPERT + c0 + FF_CHUNK]
        gate = jnp.minimum(gate, SWIGLU_LIMIT)
        up = jnp.clip(up, -SWIGLU_LIMIT, SWIGLU_LIMIT)
        act = (up + 1.0) * gate * jax.nn.sigmoid(SWIGLU_ALPHA * gate)
        term = _dot(act.astype(BF16), wd16[c0:c0 + FF_CHUNK, :])
        acc = term if acc is None else acc + term
    y = acc + bd_ref[...]
    for s in range(ROW_SUBLANES):
        y_tiles[pl.ds(s, MOE_ROWS, stride=ROW_SUBLANES), :] = y[:, s * LANES:(s + 1) * LANES]


def _moe_kernel(be_ref, nu_ref, xs_ref, wgu_ref, bgu_ref, wd_ref, bd_ref, ys_ref, wgu16, wd16):
    i = pl.program_id(0)

    @pl.when(i < nu_ref[0])
    def _():
        prev = be_ref[jnp.maximum(i - 1, 0)]
        new_expert = jnp.logical_or(i == 0, be_ref[i] != prev)

        @pl.when(new_expert)
        def _():
            wgu16[...] = wgu_ref[...].astype(BF16)
            wd16[...] = wd_ref[...].astype(BF16)
        _expert_mlp(xs_ref, ys_ref, wgu16, wd16, bgu_ref, bd_ref)

    @pl.when(i >= nu_ref[0])
    def _():
        ys_ref[...] = jnp.zeros(ys_ref.shape, F32)


def _moe_call(block_e, n_used, xs, wgu, bgu, wd, bd):
    d = D_MODEL
    n_blocks = xs.shape[0] // BLOCK_SUBLANES
    blk = lambda i, be, nu: (jnp.minimum(i, nu[0] - 1), 0)
    exp3 = lambda i, be, nu: (be[jnp.minimum(i, nu[0] - 1)], 0, 0)
    grid_spec = pltpu.PrefetchScalarGridSpec(
        num_scalar_prefetch=2,
        grid=(n_blocks,),
        in_specs=[
            pl.BlockSpec((BLOCK_SUBLANES, LANES), blk),
            pl.BlockSpec((None, d, 2 * D_EXPERT), exp3),
            pl.BlockSpec((None, 1, 2 * D_EXPERT), exp3),
            pl.BlockSpec((None, D_EXPERT, d), exp3),
            pl.BlockSpec((None, 1, d), exp3),
        ],
        out_specs=pl.BlockSpec((BLOCK_SUBLANES, LANES), lambda i, be, nu: (i, 0)),
        scratch_shapes=[pltpu.VMEM((d, 2 * D_EXPERT), BF16), pltpu.VMEM((D_EXPERT, d), BF16)],
    )
    return pl.pallas_call(
        _moe_kernel,
        grid_spec=grid_spec,
        out_shape=jax.ShapeDtypeStruct(xs.shape, F32),
        compiler_params=pltpu.CompilerParams(
            dimension_semantics=("arbitrary",), vmem_limit_bytes=VMEM_LIMIT_BYTES),
        name="moe",
    )(block_e, n_used, xs, wgu, bgu, wd, bd)


def _pad_lanes(v, lane0):
    out = jnp.zeros((1, LANES), F32)
    return lax.dynamic_update_slice(out, v.reshape(1, -1).astype(F32), (0, lane0))


def kernel(x, attn_norm_w, w_in, conv_w, a_log, dt_bias, dn_norm_w, pool_w, pool_scale, w_out, ffn_norm_w,
           w_router, b_router, w_gate_up, b_gate_up, w_down, b_down, final_norm_w):
    bsz, seq, d = x.shape
    n_tok = bsz * seq
    assert w_in.shape[0] == 1, "the combine kernel fuses the final norm, so exactly one layer is supported"
    assert d == D_MODEL and seq % TILE == 0 and bsz % SEQ_PER_STEP == 0
    l = 0

    s_z, s_a = 4 * D_DN, 4 * D_DN + 2 * N_DN_HEADS
    ba_cols = jnp.pad(w_in[l][:, s_z:s_a], ((0, 0), (0, LANES - 2 * N_DN_HEADS)))
    win_r = jnp.concatenate([w_in[l][:, :s_z], w_in[l][:, s_a:], ba_cols], axis=1).astype(BF16)
    wr32 = jnp.pad(w_router[l].astype(F32), ((0, 0), (0, LANES - N_EXPERTS)))
    wr_hi = lax.bitcast_convert_type(
        lax.bitcast_convert_type(wr32, jnp.uint32) & jnp.uint32(0xFFFF0000), F32)
    wr_p = jnp.concatenate([wr_hi.astype(BF16), (wr32 - wr_hi).astype(BF16)], axis=1)
    br_p = _pad_lanes(b_router[l], 0)
    fnw = ffn_norm_w[l].reshape(1, d)

    n_blocks = (n_tok * TOP_K + N_EXPERTS * (MOE_ROWS + COPY_ROWS - 2)) // MOE_ROWS + 1
    n_rows = n_blocks * MOE_ROWS

    x1, route, gates, meta, xs_zero = _mix_call(
        x, attn_norm_w[l].reshape(1, d), win_r, conv_w[l], _pad_lanes(a_log[l], A_LANE0),
        _pad_lanes(dt_bias[l], A_LANE0), dn_norm_w[l].reshape(1, DN_HEAD_DIM), pool_w[l].astype(BF16),
        pool_scale[l].reshape(1, D_POOL), w_out[l].astype(BF16), fnw, wr_p, br_p, n_rows * ROW_SUBLANES)

    n_tiles = n_tok // TILE
    meta = meta.reshape(n_tiles, N_EXPERTS, LANES)[:, :, :2].astype(jnp.int32)
    tile_cnt, tile_before = meta[:, :, 0], meta[:, :, 1]
    total = jnp.sum(tile_cnt, axis=0)
    padded = jnp.where(total > 0, ((total + COPY_ROWS - 1 + MOE_ROWS - 1) // MOE_ROWS) * MOE_ROWS, 0)
    pad_end = jnp.cumsum(padded)
    pad_start = pad_end - padded
    run_base = pad_start[None, :] + tile_before
    ncopy = (tile_cnt + COPY_ROWS - 1) // COPY_ROWS
    copy_end = jnp.cumsum(ncopy, axis=1)
    copy_start = copy_end - ncopy
    c_idx = jnp.arange(COPY_TABLE, dtype=jnp.int32)[None, :, None]
    in_run = jnp.logical_and(c_idx >= copy_start[:, None, :], c_idx < copy_end[:, None, :])
    dst_row = jnp.sum(jnp.where(in_run, run_base[:, None, :] + COPY_ROWS * (c_idx - copy_start[:, None, :]), 0), axis=2)
    dst = (dst_row * ROW_SUBLANES).at[:, COPY_TABLE - 1].set(copy_end[:, -1]).reshape(-1)
    block_start = jnp.arange(n_blocks, dtype=jnp.int32) * MOE_ROWS
    block_e = jnp.minimum(jnp.sum(block_start[:, None] >= pad_end[None, :], axis=1), N_EXPERTS - 1).astype(jnp.int32)
    n_used = (pad_end[-1:] // MOE_ROWS).astype(jnp.int32)

    def per_tile_table(a):
        return a.reshape(n_tiles, SUBLANES, TILE)[:, :TOP_K].reshape(-1)
    slots = per_tile_table(route)
    gatev = per_tile_table(gates)

    x1_flat = x1.reshape(n_tok, d)
    xs = _dispatch_call(slots, dst, x1_flat, fnw, xs_zero)
    ys = _moe_call(block_e, n_used, xs, w_gate_up[l], b_gate_up[l].reshape(N_EXPERTS, 1, -1),
                   w_down[l], b_down[l].reshape(N_EXPERTS, 1, -1))
    out = _combine_call(slots, gatev, dst, x1_flat, final_norm_w.reshape(1, d), ys)
    return out.reshape(bsz, seq, d)
```

```python
import functools

import jax
import jax.numpy as jnp
from jax import lax
from jax.experimental import pallas as pl
from jax.experimental.pallas import tpu as pltpu

F32 = jnp.float32
BF16 = jnp.bfloat16

D_MODEL = 1024
N_DN_HEADS = 4
DN_HEAD_DIM = 128
D_DN = N_DN_HEADS * DN_HEAD_DIM
CONV_K = 4
CHUNK = 64
POOL_WINDOWS = (2, 4, 8, 16)
N_POOL_GROUPS = len(POOL_WINDOWS)
D_POOL = D_MODEL - D_DN
POOL_GROUP_DIM = D_POOL // N_POOL_GROUPS
N_EXPERTS = 32
TOP_K = 4
D_EXPERT = D_MODEL
SWIGLU_LIMIT = 7.0
SWIGLU_ALPHA = 1.702
RMS_EPS = 1e-6
L2_EPS = 1e-6

LANES = 128
SUBLANES = 8
ROW_SUBLANES = D_MODEL // LANES
VMEM_LIMIT_BYTES = 56 * 1024 * 1024

TILE = 256
PAIR = 2 * CHUNK
CONV_HALO = SUBLANES
POOL_HALO = 2 * SUBLANES
MOE_ROWS = 608
FF_CHUNK = 512
COPY_ROWS = 16
SEQ_PER_STEP = 2
PHASE_SKEW = 0
SLOTS = TILE * TOP_K + N_EXPERTS * COPY_ROWS
assert TILE * TOP_K // COPY_ROWS + 1 <= 256

COL_QKV = 0
COL_Z = 3 * D_DN
COL_P = 4 * D_DN
COL_BA = 4 * D_DN + D_POOL
D_IN_R = COL_BA + LANES
A_LANE0 = N_DN_HEADS


def _dot(a, b):
    return jnp.dot(a, b, preferred_element_type=F32)


def _dot_nt(a, b):
    return lax.dot_general(a, b, (((1,), (1,)), ((), ())), preferred_element_type=F32)


def _hi_lo(x):
    hi = x.astype(BF16)
    lo = (x - hi.astype(F32)).astype(BF16)
    return hi, lo


def _dot_f32(a, b):
    ah, al = _hi_lo(a)
    bh, bl = _hi_lo(b)
    return _dot(ah, bh) + (_dot(ah, bl) + _dot(al, bh))


def _dot_exact_lhs(m_bf16, x):
    hi = x.astype(BF16)
    r = x - hi.astype(F32)
    mid = r.astype(BF16)
    lo = (r - mid.astype(F32)).astype(BF16)
    return _dot(m_bf16, hi) + (_dot(m_bf16, mid) + _dot(m_bf16, lo))


def _rms_scale(x):
    return lax.rsqrt(jnp.mean(x * x, axis=-1, keepdims=True) + RMS_EPS)


def _silu(x):
    return x * jax.nn.sigmoid(x)


def _softplus(x):
    return jnp.maximum(x, 0.0) + jnp.log1p(jnp.exp(-jnp.abs(x)))


def _mix_tile(t, x_ref, anw_ref, win_ref, convw_ref, alog_ref, dtb_ref, dnw_ref, poolw_ref, pscale_ref,
              wout_ref, fnw_ref, wr_ref, br_ref,
              x1_ref, route_ref, gate_ref, meta_ref,
              qkv_ext, p_ext, state, cnt):
    tt = TILE

    x = x_ref[...]
    hb = (x * _rms_scale(x) * anw_ref[...]).astype(BF16)
    qkv_pre = _dot(hb, win_ref[:, COL_QKV:COL_Z])
    for c in range(3 * N_DN_HEADS):
        qkv_ext[c, CONV_HALO:CONV_HALO + tt, :] = qkv_pre[:, c * LANES:(c + 1) * LANES]
    yield
    z = _dot(hb, win_ref[:, COL_Z:COL_P])
    p_pre = _dot(hb, win_ref[:, COL_P:COL_BA])
    for c in range(N_POOL_GROUPS):
        p_ext[c, POOL_HALO:POOL_HALO + tt, :] = p_pre[:, c * LANES:(c + 1) * LANES]
    ba = _dot(hb, win_ref[:, COL_BA:D_IN_R])

    yield
    beta = jax.nn.sigmoid(ba)
    g = -jnp.exp(alog_ref[...]) * _softplus(ba + dtb_ref[...])
    ri = lax.broadcasted_iota(jnp.int32, (tt, tt), 0)
    ci = lax.broadcasted_iota(jnp.int32, (tt, tt), 1)
    same_chunk = (ri // CHUNK) == (ci // CHUNK)
    l_incl = jnp.where(jnp.logical_and(same_chunk, ri >= ci), 1.0, 0.0).astype(BF16)
    gc = _dot_exact_lhs(l_incl, g)
    gl = jnp.concatenate(
        [jnp.broadcast_to(gc[c * CHUNK + CHUNK - 1:c * CHUNK + CHUNK, :], (CHUNK, LANES)) for c in range(tt // CHUNK)],
        axis=0)
    eg = jnp.exp(gc)
    ekd = jnp.exp(gl - gc)
    egl = jnp.exp(gl)
    gct = gc.T

    pr = lax.broadcasted_iota(jnp.int32, (PAIR, PAIR), 0)
    pc = lax.broadcasted_iota(jnp.int32, (PAIR, PAIR), 1)
    pair_same = (pr // CHUNK) == (pc // CHUNK)
    causal = jnp.logical_and(pair_same, pr >= pc)
    strict = jnp.logical_and(pair_same, pr > pc)
    eye = jnp.where(pr == pc, 1.0, 0.0).astype(F32)
    zeros_chunk = jnp.zeros((CHUNK, DN_HEAD_DIM), F32)

    def conv_silu(col):
        acc = None
        for j in range(CONV_K):
            rows = pl.ds(CONV_HALO - (CONV_K - 1) + j, tt)
            term = qkv_ext[col, rows, :] * convw_ref[j:j + 1, col * LANES:(col + 1) * LANES]
            acc = term if acc is None else acc + term
        return _silu(acc)

    def l2n(v):
        return v * lax.rsqrt(jnp.sum(v * v, axis=-1, keepdims=True) + L2_EPS)

    def block_diag(b0, b1):
        zero = jnp.zeros_like(b0)
        return jnp.concatenate([jnp.concatenate([b0, zero], axis=1), jnp.concatenate([zero, b1], axis=1)], axis=0)

    def dot_pair(a0, a1, b0, b1):
        out = _dot(jnp.concatenate([a0, a1], axis=1), block_diag(b0, b1))
        return out[:, :LANES], out[:, LANES:]

    n_pairs = tt // PAIR
    assert n_pairs == 2 and N_DN_HEADS % 2 == 0
    insts = []
    for h in range(N_DN_HEADS):
        yield
        q_all = l2n(conv_silu(h)) * (DN_HEAD_DIM ** -0.5)
        k_all = l2n(conv_silu(N_DN_HEADS + h))
        v_all = conv_silu(2 * N_DN_HEADS + h)
        yield
        la = A_LANE0 + h
        pre = []
        for d in range(n_pairs):
            r0 = d * PAIR
            q = q_all[r0:r0 + PAIR]
            k = k_all[r0:r0 + PAIR]
            v = v_all[r0:r0 + PAIR]
            bcol = beta[r0:r0 + PAIR, h:h + 1]
            gcol = gc[r0:r0 + PAIR, la:la + 1]
            egcol = eg[r0:r0 + PAIR, la:la + 1]
            ekdcol = ekd[r0:r0 + PAIR, la:la + 1]
            grow = gct[la:la + 1, r0:r0 + PAIR]
            diff = gcol - grow
            decay = jnp.where(causal, jnp.exp(jnp.where(causal, diff, 0.0)), 0.0)
            kb = k * bcol
            pre.append(dict(
                decay=decay, kq=jnp.concatenate([kb, q], axis=0).astype(BF16), kt=k.T.astype(BF16),
                rhs=jnp.concatenate([(v * bcol).astype(BF16), (kb * egcol).astype(BF16)], axis=1),
                qd=(q * egcol).astype(BF16), kdt=(k * ekdcol).T.astype(BF16)))
        kk0, kk1 = dot_pair(pre[0]["kq"], pre[1]["kq"], pre[0]["kt"], pre[1]["kt"])
        for d, kk in enumerate((kk0, kk1)):
            it = pre[d]
            it["a"] = jnp.where(strict, kk[:PAIR] * it["decay"], 0.0)
            it["qk"] = jnp.where(causal, kk[PAIR:] * it["decay"], 0.0).astype(BF16)
            insts.append(it)

    rs = [-it["a"] for it in insts]
    p16 = [it["a"].astype(BF16) for it in insts]
    for _ in range(CHUNK.bit_length() - 2):
        yield
        pws = []
        for i in range(0, len(insts), 2):
            pws.extend(dot_pair(p16[i], p16[i + 1], p16[i], p16[i + 1]))
        p16 = [pw.astype(BF16) for pw in pws]
        yield
        rps = []
        for i in range(0, len(insts), 2):
            rps.extend(dot_pair(rs[i].astype(BF16), rs[i + 1].astype(BF16), p16[i], p16[i + 1]))
        rs = [r + pw + rp for r, pw, rp in zip(rs, pws, rps)]
    yield
    for it, r in zip(insts, rs):
        uw = _dot((eye + r).astype(BF16), it["rhs"])
        it["u"] = uw[:, :DN_HEAD_DIM]
        it["wq"] = [jnp.concatenate([uw[c0:c0 + CHUNK, DN_HEAD_DIM:].astype(BF16), it["qd"][c0:c0 + CHUNK]], axis=0)
                    for c0 in range(0, PAIR, CHUNK)]
        it["qkk"] = [jnp.concatenate([it["qk"][c0:c0 + CHUNK], it["kdt"]], axis=0)
                     for c0 in range(0, PAIR, CHUNK)]

    s_cur = [state[h] for h in range(N_DN_HEADS)]
    o_rows = [[None] * (tt // CHUNK) for _ in range(N_DN_HEADS)]
    for d in range(n_pairs):
        for c in range(PAIR // CHUNK):
            yield
            c0 = c * CHUNK
            row = d * PAIR + c0
            for h0 in range(0, N_DN_HEADS, 2):
                its = [insts[h * n_pairs + d] for h in (h0, h0 + 1)]
                ss = [s_cur[h0], s_cur[h0 + 1]]
                ws = dot_pair(its[0]["wq"][c], its[1]["wq"][c], ss[0].astype(BF16), ss[1].astype(BF16))
                v_pads = []
                for it, wsq in zip(its, ws):
                    v_new = it["u"][c0:c0 + CHUNK] - wsq[:CHUNK]
                    parts = [zeros_chunk] * (PAIR // CHUNK)
                    parts[c] = v_new
                    v_pads.append(jnp.concatenate(parts, axis=0).astype(BF16))
                ov = dot_pair(its[0]["qkk"][c], its[1]["qkk"][c], v_pads[0], v_pads[1])
                for j, h in enumerate((h0, h0 + 1)):
                    o_rows[h][d * (PAIR // CHUNK) + c] = ws[j][CHUNK:] + ov[j][:CHUNK]
                    gt = egl[row:row + 1, A_LANE0 + h:A_LANE0 + h + 1]
                    s_cur[h] = ss[j] * gt + ov[j][CHUNK:]
    for h in range(N_DN_HEADS):
        state[h] = s_cur[h]

    yield
    dn_parts = []
    for h in range(N_DN_HEADS):
        o = jnp.concatenate(o_rows[h], axis=0)
        zh = z[:, h * LANES:(h + 1) * LANES]
        dn_parts.append(((o * _rms_scale(o) * dnw_ref[...]) * _silu(zh)).astype(BF16))

    pos = (t * tt + lax.broadcasted_iota(jnp.int32, (tt, 1), 0) + 1).astype(F32)
    pool_parts = []
    for gi, win in enumerate(POOL_WINDOWS):
        yield
        cols = slice(gi * POOL_GROUP_DIM, (gi + 1) * POOL_GROUP_DIM)
        cur = p_ext[gi, POOL_HALO:POOL_HALO + tt, :]
        acc = cur
        for sft in range(1, win):
            acc = acc + p_ext[gi, pl.ds(POOL_HALO - sft, tt), :]
        mixed = acc / jnp.minimum(pos, float(win)) - cur
        y = _dot(mixed.astype(BF16), poolw_ref[gi]) * pscale_ref[:, cols]
        pool_parts.append(y.astype(BF16))

    yield
    x1 = x +_dot(jnp.concatenate(dn_parts + pool_parts, axis=1), wout_ref[...])
    x1_ref[...] = x1

    qkv_ext[:, 0:CONV_HALO, :] = qkv_ext[:, tt:tt + CONV_HALO, :]
    p_ext[:, 0:POOL_HALO, :] = p_ext[:, tt:tt + POOL_HALO, :]

    hn = x1 * _rms_scale(x1) * fnw_ref[...]
    hn_hi, hn_lo = _hi_lo(hn)
    hh = _dot(hn_hi, wr_ref[...])
    logits = hh[:, :LANES] + (hh[:, LANES:] + _dot(hn_lo, wr_ref[:, :LANES])) + br_ref[...]
    yield
    lt = logits.T[:N_EXPERTS]
    eidx = lax.broadcasted_iota(jnp.int32, (N_EXPERTS, tt), 0).astype(F32)
    work = lt
    vals, idxs = [], []
    for _ in range(TOP_K):
        m = jnp.max(work, axis=0, keepdims=True)
        idx = jnp.min(jnp.where(work == m, eidx, float(LANES)), axis=0, keepdims=True)
        vals.append(m)
        idxs.append(idx)
        work = jnp.where(eidx == idx, -jnp.inf, work)
    exps = [jnp.exp(v - vals[0]) for v in vals]
    denom = exps[0] + exps[1] + exps[2] + exps[3]

    yield
    onehot = jnp.zeros((N_EXPERTS, tt), F32)
    for idx in idxs:
        onehot = onehot + jnp.where(eidx == idx, 1.0, 0.0)
    earlier = jnp.where(ri < ci, 1.0, 0.0).astype(BF16)
    within = _dot(onehot.astype(BF16), earlier)
    tile_cnt = jnp.sum(onehot, axis=1, keepdims=True)
    n_copies = jnp.floor((tile_cnt + float(COPY_ROWS - 1)) * (1.0 / COPY_ROWS))
    er = lax.broadcasted_iota(jnp.int32, (N_EXPERTS, N_EXPERTS), 0)
    ec = lax.broadcasted_iota(jnp.int32, (N_EXPERTS, N_EXPERTS), 1)
    lower_expert = jnp.where(ec < er, 1.0, 0.0).astype(BF16)
    run_start = float(COPY_ROWS) * _dot(
        lower_expert, jnp.broadcast_to(n_copies, (N_EXPERTS, LANES)).astype(BF16))[:, 0:1]
    slot_of = run_start + within
    srow = lax.broadcasted_iota(jnp.int32, (SUBLANES, tt), 0)
    route = jnp.zeros((SUBLANES, tt), F32)
    gates = jnp.zeros((SUBLANES, tt), F32)
    for kk in range(TOP_K):
        slot = jnp.sum(jnp.where(eidx == idxs[kk], slot_of, 0.0), axis=0, keepdims=True)
        route = jnp.where(srow == kk, slot * float(ROW_SUBLANES), route)
        gates = jnp.where(srow == kk, exps[kk] / denom, gates)
    route_ref[...] = route.astype(jnp.int32)
    gate_ref[...] = gates
    mlane = lax.broadcasted_iota(jnp.int32, (N_EXPERTS, LANES), 1)
    meta_ref[...] = jnp.where(mlane == 0, tile_cnt, jnp.where(mlane == 1, cnt[...], 0.0))
    cnt[...] = cnt[...] + tile_cnt


def _mix_kernel(x_ref, anw_ref, win_ref, convw_ref, alog_ref, dtb_ref, dnw_ref, poolw_ref, pscale_ref,
                wout_ref, fnw_ref, wr_ref, br_ref,
                x1_ref, route_ref, gate_ref, meta_ref, xs_ref,
                qkv_ext, p_ext, state, cnt, zero_buf, zero_sem, *, zero_per_step):
    b = pl.program_id(0)
    t = pl.program_id(1)
    first_step = jnp.logical_and(b == 0, t == 0)

    @pl.when(first_step)
    def _():
        zero_buf[...] = jnp.zeros(zero_buf.shape, F32)
        cnt[...] = jnp.zeros(cnt.shape, F32)
    fill_rows = zero_buf.shape[0]
    n_fill = xs_ref.shape[0] // fill_rows
    step = b * pl.num_programs(1) + t

    def zero_copy(j, go):
        idx = step * zero_per_step + j

        @pl.when(idx < n_fill)
        def _():
            go(pltpu.make_async_copy(
                zero_buf, xs_ref.at[pl.ds(pl.multiple_of(idx * fill_rows, SUBLANES), fill_rows)], zero_sem))
    for j in range(zero_per_step):
        zero_copy(j, lambda cp: cp.start())

    @pl.when(t == 0)
    def _():
        qkv_ext[:, :, 0:CONV_HALO, :] = jnp.zeros((SEQ_PER_STEP, 3 * N_DN_HEADS, CONV_HALO, LANES), F32)
        p_ext[:, :, 0:POOL_HALO, :] = jnp.zeros((SEQ_PER_STEP, N_POOL_GROUPS, POOL_HALO, LANES), F32)
        state[...] = jnp.zeros(state.shape, F32)

    tiles = [
        _mix_tile(t, x_ref.at[sq], anw_ref, win_ref, convw_ref, alog_ref, dtb_ref, dnw_ref, poolw_ref, pscale_ref,
                  wout_ref, fnw_ref, wr_ref, br_ref,
                  x1_ref.at[sq], route_ref.at[sq], gate_ref.at[sq], meta_ref.at[sq],
                  qkv_ext.at[sq], p_ext.at[sq], state.at[sq], cnt)
        for sq in range(SEQ_PER_STEP)]
    live = [True] * SEQ_PER_STEP
    tick = 0
    while any(live):
        for sq, gen in enumerate(tiles):
            if live[sq] and tick >= sq * PHASE_SKEW:
                live[sq] = next(gen, "done") != "done"
        tick += 1
    for j in range(zero_per_step):
        zero_copy(j, lambda cp: cp.wait())


def _mix_call(x, anw, win_r, convw, alog_p, dtb_p, dnw, poolw, pscale, wout, fnw, wr_p, br_p, xs_sublanes):
    bsz, seq, d = x.shape
    n_t = seq // TILE
    sps = SEQ_PER_STEP
    n_b = bsz // sps
    x = x.reshape(sps, n_b, seq, d)
    fill_rows = BLOCK_SUBLANES
    assert xs_sublanes % fill_rows == 0
    zero_per_step = -(-(xs_sublanes // fill_rows) // (n_b * n_t))
    const2 = lambda b, t: (0, 0)
    tok_map = lambda b, t: (0, b * n_t + t, 0)
    seq_map = lambda b, t: (0, b, t, 0)
    in_specs = [
        pl.BlockSpec((sps, None, TILE, d), seq_map),
        pl.BlockSpec(anw.shape, const2),
        pl.BlockSpec(win_r.shape, const2),
        pl.BlockSpec(convw.shape, const2),
        pl.BlockSpec(alog_p.shape, const2),
        pl.BlockSpec(dtb_p.shape, const2),
        pl.BlockSpec(dnw.shape, const2),
        pl.BlockSpec(poolw.shape, lambda b, t: (0, 0, 0)),
        pl.BlockSpec(pscale.shape, const2),
        pl.BlockSpec(wout.shape, const2),
        pl.BlockSpec(fnw.shape, const2),
        pl.BlockSpec(wr_p.shape, const2),
        pl.BlockSpec(br_p.shape, const2),
    ]
    out_shape = (
        jax.ShapeDtypeStruct((sps, n_b, seq, d), F32),
        jax.ShapeDtypeStruct((sps, n_b * n_t * SUBLANES, TILE), jnp.int32),
        jax.ShapeDtypeStruct((sps, n_b * n_t * SUBLANES, TILE), F32),
        jax.ShapeDtypeStruct((sps, n_b * n_t * N_EXPERTS, LANES), F32),
        jax.ShapeDtypeStruct((xs_sublanes, LANES), F32),
    )
    out_specs = (
        pl.BlockSpec((sps, None, TILE, d), seq_map),
        pl.BlockSpec((sps, SUBLANES, TILE), tok_map),
        pl.BlockSpec((sps, SUBLANES, TILE), tok_map),
        pl.BlockSpec((sps, N_EXPERTS, LANES), tok_map),
        pl.BlockSpec(memory_space=pl.ANY),
    )
    scratch = [
        pltpu.VMEM((sps, 3 * N_DN_HEADS, CONV_HALO + TILE, LANES), F32),
        pltpu.VMEM((sps, N_POOL_GROUPS, POOL_HALO + TILE, LANES), F32),
        pltpu.VMEM((sps, N_DN_HEADS, DN_HEAD_DIM, DN_HEAD_DIM), F32),
        pltpu.VMEM((N_EXPERTS, LANES), F32),
        pltpu.VMEM((fill_rows, LANES), F32),
        pltpu.SemaphoreType.DMA(()),
    ]
    return pl.pallas_call(
        functools.partial(_mix_kernel, zero_per_step=zero_per_step),
        grid=(n_b, n_t),
        in_specs=in_specs,
        out_specs=out_specs,
        out_shape=out_shape,
        scratch_shapes=scratch,
        compiler_params=pltpu.CompilerParams(
            dimension_semantics=("arbitrary", "arbitrary"), vmem_limit_bytes=VMEM_LIMIT_BYTES),
        name="mix",
    )(x, anw, win_r, convw, alog_p, dtb_p, dnw, poolw, pscale, wout, fnw, wr_p, br_p)


COPY_SUBLANES = COPY_ROWS * ROW_SUBLANES


COPY_TABLE = 256
assert SLOTS // COPY_ROWS < COPY_TABLE
LOOP_UNROLL = 8


def _for_count(n, fn):
    def main(i, carry):
        for u in range(LOOP_UNROLL):
            fn(i * LOOP_UNROLL + u)
        return carry
    n_main = lax.shift_right_logical(n, LOOP_UNROLL.bit_length() - 1)
    lax.fori_loop(0, n_main, main, 0)

    def tail(i, carry):
        fn(i)
        return carry
    lax.fori_loop(n_main * LOOP_UNROLL, n, tail, 0)


def _run_copies(dst_ref, local_buf, global_ref, sem, to_global):
    n = dst_ref[COPY_TABLE - 1]

    def start(c):
        loc = local_buf.at[pl.ds(pl.multiple_of(c * COPY_SUBLANES, COPY_SUBLANES), COPY_SUBLANES)]
        glo = global_ref.at[pl.ds(pl.multiple_of(dst_ref[c], SUBLANES), COPY_SUBLANES)]
        if to_global:
            pltpu.make_async_copy(loc, glo, sem).start()
        else:
            pltpu.make_async_copy(glo, loc, sem).start()
    _for_count(n, start)
    return n


def _wait_copies(n, local_buf, global_ref, sem):
    def wait(i):
        del i
        pltpu.make_async_copy(local_buf.at[pl.ds(0, COPY_SUBLANES)], global_ref.at[pl.ds(0, COPY_SUBLANES)], sem).wait()
    _for_count(n, wait)


def _dispatch_kernel(slot_ref, dst_ref, x1_ref, fnw_ref, xs_in_ref, xs_ref,
                     tok_tiles, sorted_buf, inflight, sem):
    del xs_in_ref
    step = pl.program_id(0)
    buf = sorted_buf.at[step % 2]
    x1 = x1_ref[...]
    hn = x1 * _rms_scale(x1) * fnw_ref[...]
    for s in range(ROW_SUBLANES):
        tok_tiles[pl.ds(s, TILE, stride=ROW_SUBLANES), :] = hn[:, s * LANES:(s + 1) * LANES]
    @pl.when(step < 2)
    def _():
        buf[...] = jnp.zeros(buf.shape, F32)

    def token_body(i, carry):
        tile = tok_tiles[pl.ds(pl.multiple_of(i * ROW_SUBLANES, ROW_SUBLANES), ROW_SUBLANES), :]
        for kk in range(TOP_K):
            slot_row = slot_ref[kk * TILE + i]
            buf[pl.ds(pl.multiple_of(slot_row, ROW_SUBLANES), ROW_SUBLANES), :] = tile
        return carry
    lax.fori_loop(0, TILE, token_body, 0, unroll=8)

    @pl.when(step > 0)
    def _():
        _wait_copies(inflight[0], buf, xs_ref, sem)
    inflight[0] = _run_copies(dst_ref, buf, xs_ref, sem, to_global=True)

    @pl.when(step == pl.num_programs(0) - 1)
    def _():
        _wait_copies(inflight[0], buf, xs_ref, sem)


def _smem_spec(n):
    return pl.BlockSpec((n,), lambda i: (i,), memory_space=pltpu.SMEM)


def _dispatch_call(slots, dst, x1_flat, fnw, xs_zero):
    n_tok, d = x1_flat.shape
    n_tiles = n_tok // TILE
    tile_of = lambda i: (i % SEQ_PER_STEP) * (n_tiles // SEQ_PER_STEP) + i // SEQ_PER_STEP
    return pl.pallas_call(
        _dispatch_kernel,
        grid=(n_tiles,),
        in_specs=[
            pl.BlockSpec((TILE * TOP_K,), lambda i: (tile_of(i),), memory_space=pltpu.SMEM),
            pl.BlockSpec((COPY_TABLE,), lambda i: (tile_of(i),), memory_space=pltpu.SMEM),
            pl.BlockSpec((TILE, d), lambda i: (tile_of(i), 0)),
            pl.BlockSpec(fnw.shape, lambda i: (0, 0)),
            pl.BlockSpec(memory_space=pl.ANY),
        ],
        out_specs=pl.BlockSpec(memory_space=pl.ANY),
        out_shape=jax.ShapeDtypeStruct(xs_zero.shape, F32),
        scratch_shapes=[pltpu.VMEM((TILE * ROW_SUBLANES, LANES), F32),
                        pltpu.VMEM((2, SLOTS * ROW_SUBLANES, LANES), F32),
                        pltpu.SMEM((1,), jnp.int32),
                        pltpu.SemaphoreType.DMA(())],
        input_output_aliases={4: 0},
        compiler_params=pltpu.CompilerParams(dimension_semantics=("arbitrary",)),
        name="dispatch",
    )(slots, dst, x1_flat, fnw, xs_zero)


def _combine_kernel(slot_ref, gatev_ref, dst_ref, dst_next_ref, x1_ref, fw_ref, ys_ref,
                    out_ref, sorted_buf, tok_tiles, inflight, sems):
    step = pl.program_id(0)
    cur = step % 2
    nxt = 1 - cur
    buf = sorted_buf.at[cur]

    @pl.when(step == 0)
    def _():
        inflight[0] = _run_copies(dst_ref, sorted_buf.at[0], ys_ref, sems.at[0], to_global=False)

    @pl.when(step + 1 < pl.num_programs(0))
    def _():
        inflight[nxt] = _run_copies(dst_next_ref, sorted_buf.at[nxt], ys_ref, sems.at[nxt], to_global=False)
    _wait_copies(inflight[cur], buf, ys_ref, sems.at[cur])

    def token_body(i, carry):
        acc = None
        for kk in range(TOP_K):
            slot_row = slot_ref[kk * TILE + i]
            row = buf[pl.ds(pl.multiple_of(slot_row, ROW_SUBLANES), ROW_SUBLANES), :]
            term = row * gatev_ref[kk * TILE + i]
            acc = term if acc is None else acc + term
        tok_tiles[pl.ds(pl.multiple_of(i * ROW_SUBLANES, ROW_SUBLANES), ROW_SUBLANES), :] = acc
        return carry
    lax.fori_loop(0, TILE, token_body, 0, unroll=8)

    x1 = x1_ref[...]
    cols = []
    ssq = None
    for s in range(ROW_SUBLANES):
        c = x1[:, s * LANES:(s + 1) * LANES] + tok_tiles[pl.ds(s, TILE, stride=ROW_SUBLANES), :]
        cols.append(c)
        sq = jnp.sum(c * c, axis=-1, keepdims=True)
        ssq = sq if ssq is None else ssq + sq
    scale = lax.rsqrt(ssq * (1.0 / D_MODEL) + RMS_EPS)
    for s in range(ROW_SUBLANES):
        out_ref[:, s * LANES:(s + 1) * LANES] = cols[s] * scale * fw_ref[:, s * LANES:(s + 1) * LANES]


def _combine_call(slots, gatev, dst, x1_flat, fw, ys):
    n_tok, d = x1_flat.shape
    n_steps = n_tok // TILE
    next_spec = pl.BlockSpec((COPY_TABLE,), lambda i: (jnp.minimum(i + 1, n_steps - 1),), memory_space=pltpu.SMEM)
    return pl.pallas_call(
        _combine_kernel,
        grid=(n_steps,),
        in_specs=[
            _smem_spec(TILE * TOP_K), _smem_spec(TILE * TOP_K), _smem_spec(COPY_TABLE), next_spec,
            pl.BlockSpec((TILE, d), lambda i: (i, 0)),
            pl.BlockSpec(fw.shape, lambda i: (0, 0)),
            pl.BlockSpec(memory_space=pl.ANY),
        ],
        out_specs=pl.BlockSpec((TILE, d), lambda i: (i, 0)),
        out_shape=jax.ShapeDtypeStruct((n_tok, d), F32),
        scratch_shapes=[pltpu.VMEM((2, SLOTS * ROW_SUBLANES, LANES), F32),
                        pltpu.VMEM((TILE * ROW_SUBLANES, LANES), F32),
                        pltpu.SMEM((2,), jnp.int32),
                        pltpu.SemaphoreType.DMA((2,))],
        compiler_params=pltpu.CompilerParams(dimension_semantics=("arbitrary",)),
        name="combine",
    )(slots, gatev, dst, dst, x1_flat, fw, ys)


BLOCK_SUBLANES = MOE_ROWS * ROW_SUBLANES


def _expert_mlp(x_tiles, y_tiles, wgu16, wd16, bgu_ref, bd_ref):
    xb = jnp.concatenate(
        [x_tiles[pl.ds(s, MOE_ROWS, stride=ROW_SUBLANES), :].astype(BF16) for s in range(ROW_SUBLANES)], axis=1)
    acc = None
    for j in range(D_EXPERT // FF_CHUNK):
        c0 = j * FF_CHUNK
        gate = _dot(xb, wgu16[:, c0:c0 + FF_CHUNK]) + bgu_ref[:, c0:c0 + FF_CHUNK]
        up = _dot(xb, wgu16[:, D_EXPERT + c0:D_EXPERT + c0 + FF_CHUNK]) + bgu_ref[:, D_EXPERT + c0:D_EXPERT + c0 + FF_CHUNK]
        gate = jnp.minimum(gate, SWIGLU_LIMIT)
        up = jnp.clip(up, -SWIGLU_LIMIT, SWIGLU_LIMIT)
        act = (up + 1.0) * gate * jax.nn.sigmoid(SWIGLU_ALPHA * gate)
        term = _dot(act.astype(BF16), wd16[c0:c0 + FF_CHUNK, :])
        acc = term if acc is None else acc + term
    y = acc + bd_ref[...]
    for s in range(ROW_SUBLANES):
        y_tiles[pl.ds(s, MOE_ROWS, stride=ROW_SUBLANES), :] = y[:, s * LANES:(s + 1) * LANES]


def _moe_kernel(be_ref, nu_ref, xs_ref, wgu_ref, bgu_ref, wd_ref, bd_ref, ys_ref, wgu16, wd16):
    i = pl.program_id(0)

    @pl.when(i < nu_ref[0])
    def _():
        prev = be_ref[jnp.maximum(i - 1, 0)]
        new_expert = jnp.logical_or(i == 0, be_ref[i] != prev)

        @pl.when(new_expert)
        def _():
            wgu16[...] = wgu_ref[...].astype(BF16)
            wd16[...] = wd_ref[...].astype(BF16)
        _expert_mlp(xs_ref, ys_ref, wgu16, wd16, bgu_ref, bd_ref)

    @pl.when(i >= nu_ref[0])
    def _():
        ys_ref[...] = jnp.zeros(ys_ref.shape, F32)


def _moe_call(block_e, n_used, xs, wgu, bgu, wd, bd):
    d = D_MODEL
    n_blocks = xs.shape[0] // BLOCK_SUBLANES
    blk = lambda i, be, nu: (jnp.minimum(i, nu[0] - 1), 0)
    exp3 = lambda i, be, nu: (be[jnp.minimum(i, nu[0] - 1)], 0, 0)
    grid_spec = pltpu.PrefetchScalarGridSpec(
        num_scalar_prefetch=2,
        grid=(n_blocks,),
        in_specs=[
            pl.BlockSpec((BLOCK_SUBLANES, LANES), blk),
            pl.BlockSpec((None, d, 2 * D_EXPERT), exp3),
            pl.BlockSpec((None, 1, 2 * D_EXPERT), exp3),
            pl.BlockSpec((None, D_EXPERT, d), exp3),
            pl.BlockSpec((None, 1, d), exp3),
        ],
        out_specs=pl.BlockSpec((BLOCK_SUBLANES, LANES), lambda i, be, nu: (i, 0)),
        scratch_shapes=[pltpu.VMEM((d, 2 * D_EXPERT), BF16), pltpu.VMEM((D_EXPERT, d), BF16)],
    )
    return pl.pallas_call(
        _moe_kernel,
        grid_spec=grid_spec,
        out_shape=jax.ShapeDtypeStruct(xs.shape, F32),
        compiler_params=pltpu.CompilerParams(
            dimension_semantics=("arbitrary",), vmem_limit_bytes=VMEM_LIMIT_BYTES),
        name="moe",
    )(block_e, n_used, xs, wgu, bgu, wd, bd)


def _pad_lanes(v, lane0):
    out = jnp.zeros((1, LANES), F32)
    return lax.dynamic_update_slice(out, v.reshape(1, -1).astype(F32), (0, lane0))


def kernel(x, attn_norm_w, w_in, conv_w, a_log, dt_bias, dn_norm_w, pool_w, pool_scale, w_out, ffn_norm_w,
           w_router, b_router, w_gate_up, b_gate_up, w_down, b_down, final_norm_w):
    bsz, seq, d = x.shape
    n_tok = bsz * seq
    assert w_in.shape[0] == 1, "the combine kernel fuses the final norm, so exactly one layer is supported"
    assert d == D_MODEL and seq % TILE == 0 and bsz % SEQ_PER_STEP == 0
    l = 0

    s_z, s_a = 4 * D_DN, 4 * D_DN + 2 * N_DN_HEADS
    ba_cols = jnp.pad(w_in[l][:, s_z:s_a], ((0, 0), (0, LANES - 2 * N_DN_HEADS)))
    win_r = jnp.concatenate([w_in[l][:, :s_z], w_in[l][:, s_a:], ba_cols], axis=1).astype(BF16)
    wr32 = jnp.pad(w_router[l].astype(F32), ((0, 0), (0, LANES - N_EXPERTS)))
    wr_hi = lax.bitcast_convert_type(
        lax.bitcast_convert_type(wr32, jnp.uint32) & jnp.uint32(0xFFFF0000), F32)
    wr_p = jnp.concatenate([wr_hi.astype(BF16), (wr32 - wr_hi).astype(BF16)], axis=1)
    br_p = _pad_lanes(b_router[l], 0)
    fnw = ffn_norm_w[l].reshape(1, d)

    n_blocks = (n_tok * TOP_K + N_EXPERTS * (MOE_ROWS + COPY_ROWS - 2)) // MOE_ROWS + 1
    n_rows = n_blocks * MOE_ROWS

    x1, route, gates, meta, xs_zero = _mix_call(
        x, attn_norm_w[l].reshape(1, d), win_r, conv_w[l], _pad_lanes(a_log[l], A_LANE0),
        _pad_lanes(dt_bias[l], A_LANE0), dn_norm_w[l].reshape(1, DN_HEAD_DIM), pool_w[l].astype(BF16),
        pool_scale[l].reshape(1, D_POOL), w_out[l].astype(BF16), fnw, wr_p, br_p, n_rows * ROW_SUBLANES)

    n_tiles = n_tok // TILE
    meta = meta.reshape(n_tiles, N_EXPERTS, LANES)[:, :, :2].astype(jnp.int32)
    tile_cnt, tile_before = meta[:, :, 0], meta[:, :, 1]
    total = jnp.sum(tile_cnt, axis=0)
    padded = jnp.where(total > 0, ((total + COPY_ROWS - 1 + MOE_ROWS - 1) // MOE_ROWS) * MOE_ROWS, 0)
    pad_end = jnp.cumsum(padded)
    pad_start = pad_end - padded
    run_base = pad_start[None, :] + tile_before
    ncopy = (tile_cnt + COPY_ROWS - 1) // COPY_ROWS
    copy_end = jnp.cumsum(ncopy, axis=1)
    copy_start = copy_end - ncopy
    c_idx = jnp.arange(COPY_TABLE, dtype=jnp.int32)[None, :, None]
    in_run = jnp.logical_and(c_idx >= copy_start[:, None, :], c_idx < copy_end[:, None, :])
    dst_row = jnp.sum(jnp.where(in_run, run_base[:, None, :] + COPY_ROWS * (c_idx - copy_start[:, None, :]), 0), axis=2)
    dst = (dst_row * ROW_SUBLANES).at[:, COPY_TABLE - 1].set(copy_end[:, -1]).reshape(-1)
    block_start = jnp.arange(n_blocks, dtype=jnp.int32) * MOE_ROWS
    block_e = jnp.minimum(jnp.sum(block_start[:, None] >= pad_end[None, :], axis=1), N_EXPERTS - 1).astype(jnp.int32)
    n_used = (pad_end[-1:] // MOE_ROWS).astype(jnp.int32)

    def per_tile_table(a):
        return a.reshape(n_tiles, SUBLANES, TILE)[:, :TOP_K].reshape(-1)
    slots = per_tile_table(route)
    gatev = per_tile_table(gates)

    x1_flat = x1.reshape(n_tok, d)
    xs = _dispatch_call(slots, dst, x1_flat, fnw, xs_zero)
    ys = _moe_call(block_e, n_used, xs, w_gate_up[l], b_gate_up[l].reshape(N_EXPERTS, 1, -1),
                   w_down[l], b_down[l].reshape(N_EXPERTS, 1, -1))
    out = _combine_call(slots, gatev, dst, x1_flat, final_norm_w.reshape(1, d), ys)
    return out.reshape(bsz, seq, d)
```

```python
import functools

import jax
import jax.numpy as jnp
from jax import lax
from jax.experimental import pallas as pl
from jax.experimental.pallas import tpu as pltpu

F32 = jnp.float32
BF16 = jnp.bfloat16

D_MODEL = 1024
N_DN_HEADS = 4
DN_HEAD_DIM = 128
D_DN = N_DN_HEADS * DN_HEAD_DIM
CONV_K = 4
CHUNK = 64
POOL_WINDOWS = (2, 4, 8, 16)
N_POOL_GROUPS = len(POOL_WINDOWS)
D_POOL = D_MODEL - D_DN
POOL_GROUP_DIM = D_POOL // N_POOL_GROUPS
N_EXPERTS = 32
TOP_K = 4
D_EXPERT = D_MODEL
SWIGLU_LIMIT = 7.0
SWIGLU_ALPHA = 1.702
RMS_EPS = 1e-6
L2_EPS = 1e-6

LANES = 128
SUBLANES = 8
ROW_SUBLANES = D_MODEL // LANES
VMEM_LIMIT_BYTES = 56 * 1024 * 1024

TILE = 256
PAIR = 2 * CHUNK
CONV_HALO = SUBLANES
POOL_HALO = 2 * SUBLANES
MOE_ROWS = 608
FF_CHUNK = 512
COPY_ROWS = 8
SEQ_PER_STEP = 2
PHASE_SKEW = 0
SLOTS = TILE * TOP_K + N_EXPERTS * COPY_ROWS
assert TILE * TOP_K // COPY_ROWS + 1 <= 256

COL_QKV = 0
COL_Z = 3 * D_DN
COL_P = 4 * D_DN
COL_BA = 4 * D_DN + D_POOL
D_IN_R = COL_BA + LANES
A_LANE0 = N_DN_HEADS


def _dot(a, b):
    return jnp.dot(a, b, preferred_element_type=F32)


def _dot_nt(a, b):
    return lax.dot_general(a, b, (((1,), (1,)), ((), ())), preferred_element_type=F32)


def _hi_lo(x):
    hi = x.astype(BF16)
    lo = (x - hi.astype(F32)).astype(BF16)
    return hi, lo


def _dot_f32(a, b):
    ah, al = _hi_lo(a)
    bh, bl = _hi_lo(b)
    return _dot(ah, bh) + (_dot(ah, bl) + _dot(al, bh))


def _dot_exact_lhs(m_bf16, x):
    hi = x.astype(BF16)
    r = x - hi.astype(F32)
    mid = r.astype(BF16)
    lo = (r - mid.astype(F32)).astype(BF16)
    return _dot(m_bf16, hi) + (_dot(m_bf16, mid) + _dot(m_bf16, lo))


def _rms_scale(x):
    return lax.rsqrt(jnp.mean(x * x, axis=-1, keepdims=True) + RMS_EPS)


def _silu(x):
    return x * jax.nn.sigmoid(x)


def _softplus(x):
    return jnp.maximum(x, 0.0) + jnp.log1p(jnp.exp(-jnp.abs(x)))


def _mix_tile(t, x_ref, anw_ref, win_ref, convw_ref, alog_ref, dtb_ref, dnw_ref, poolw_ref, pscale_ref,
              wout_ref, fnw_ref, wr_ref, br_ref,
              x1_ref, route_ref, gate_ref, meta_ref,
              qkv_ext, p_ext, state, cnt):
    tt = TILE

    x = x_ref[...]
    hb = (x * _rms_scale(x) * anw_ref[...]).astype(BF16)
    qkv_pre = _dot(hb, win_ref[:, COL_QKV:COL_Z])
    for c in range(3 * N_DN_HEADS):
        qkv_ext[c, CONV_HALO:CONV_HALO + tt, :] = qkv_pre[:, c * LANES:(c + 1) * LANES]
    yield
    z = _dot(hb, win_ref[:, COL_Z:COL_P])
    p_pre = _dot(hb, win_ref[:, COL_P:COL_BA])
    for c in range(N_POOL_GROUPS):
        p_ext[c, POOL_HALO:POOL_HALO + tt, :] = p_pre[:, c * LANES:(c + 1) * LANES]
    ba = _dot(hb, win_ref[:, COL_BA:D_IN_R])

    yield
    beta = jax.nn.sigmoid(ba)
    g = -jnp.exp(alog_ref[...]) * _softplus(ba + dtb_ref[...])
    ri = lax.broadcasted_iota(jnp.int32, (tt, tt), 0)
    ci = lax.broadcasted_iota(jnp.int32, (tt, tt), 1)
    same_chunk = (ri // CHUNK) == (ci // CHUNK)
    l_incl = jnp.where(jnp.logical_and(same_chunk, ri >= ci), 1.0, 0.0).astype(BF16)
    gc = _dot_exact_lhs(l_incl, g)
    gl = jnp.concatenate(
        [jnp.broadcast_to(gc[c * CHUNK + CHUNK - 1:c * CHUNK + CHUNK, :], (CHUNK, LANES)) for c in range(tt // CHUNK)],
        axis=0)
    eg = jnp.exp(gc)
    ekd = jnp.exp(gl - gc)
    egl = jnp.exp(gl)
    gct = gc.T

    pr = lax.broadcasted_iota(jnp.int32, (PAIR, PAIR), 0)
    pc = lax.broadcasted_iota(jnp.int32, (PAIR, PAIR), 1)
    pair_same = (pr // CHUNK) == (pc // CHUNK)
    causal = jnp.logical_and(pair_same, pr >= pc)
    strict = jnp.logical_and(pair_same, pr > pc)
    eye = jnp.where(pr == pc, 1.0, 0.0).astype(F32)
    zeros_chunk = jnp.zeros((CHUNK, DN_HEAD_DIM), F32)

    def conv_silu(col):
        acc = None
        for j in range(CONV_K):
            rows = pl.ds(CONV_HALO - (CONV_K - 1) + j, tt)
            term = qkv_ext[col, rows, :] * convw_ref[j:j + 1, col * LANES:(col + 1) * LANES]
            acc = term if acc is None else acc + term
        return _silu(acc)

    def l2n(v):
        return v * lax.rsqrt(jnp.sum(v * v, axis=-1, keepdims=True) + L2_EPS)

    def block_diag(b0, b1):
        zero = jnp.zeros_like(b0)
        return jnp.concatenate([jnp.concatenate([b0, zero], axis=1), jnp.concatenate([zero, b1], axis=1)], axis=0)

    def dot_pair(a0, a1, b0, b1):
        out = _dot(jnp.concatenate([a0, a1], axis=1), block_diag(b0, b1))
        return out[:, :LANES], out[:, LANES:]

    n_pairs = tt // PAIR
    assert n_pairs == 2 and N_DN_HEADS % 2 == 0
    insts = []
    for h in range(N_DN_HEADS):
        yield
        q_all = l2n(conv_silu(h)) * (DN_HEAD_DIM ** -0.5)
        k_all = l2n(conv_silu(N_DN_HEADS + h))
        v_all = conv_silu(2 * N_DN_HEADS + h)
        yield
        la = A_LANE0 + h
        pre = []
        for d in range(n_pairs):
            r0 = d * PAIR
            q = q_all[r0:r0 + PAIR]
            k = k_all[r0:r0 + PAIR]
            v = v_all[r0:r0 + PAIR]
            bcol = beta[r0:r0 + PAIR, h:h + 1]
            gcol = gc[r0:r0 + PAIR, la:la + 1]
            egcol = eg[r0:r0 + PAIR, la:la + 1]
            ekdcol = ekd[r0:r0 + PAIR, la:la + 1]
            grow = gct[la:la + 1, r0:r0 + PAIR]
            diff = gcol - grow
            decay = jnp.where(causal, jnp.exp(jnp.where(causal, diff, 0.0)), 0.0)
            kb = k * bcol
            pre.append(dict(
                decay=decay, kq=jnp.concatenate([kb, q], axis=0).astype(BF16), kt=k.T.astype(BF16),
                rhs=jnp.concatenate([(v * bcol).astype(BF16), (kb * egcol).astype(BF16)], axis=1),
                qd=(q * egcol).astype(BF16), kdt=(k * ekdcol).T.astype(BF16)))
        kk0, kk1 = dot_pair(pre[0]["kq"], pre[1]["kq"], pre[0]["kt"], pre[1]["kt"])
        for d, kk in enumerate((kk0, kk1)):
            it = pre[d]
            it["a"] = jnp.where(strict, kk[:PAIR] * it["decay"], 0.0)
            it["qk"] = jnp.where(causal, kk[PAIR:] * it["decay"], 0.0).astype(BF16)
            insts.append(it)

    rs = [-it["a"] for it in insts]
    p16 = [it["a"].astype(BF16) for it in insts]
    for _ in range(CHUNK.bit_length() - 2):
        yield
        pws = []
        for i in range(0, len(insts), 2):
            pws.extend(dot_pair(p16[i], p16[i + 1], p16[i], p16[i + 1]))
        p16 = [pw.astype(BF16) for pw in pws]
        yield
        rps = []
        for i in range(0, len(insts), 2):
            rps.extend(dot_pair(rs[i].astype(BF16), rs[i + 1].astype(BF16), p16[i], p16[i + 1]))
        rs = [r + pw + rp for r, pw, rp in zip(rs, pws, rps)]
    yield
    for it, r in zip(insts, rs):
        uw = _dot((eye + r).astype(BF16), it["rhs"])
        it["u"] = uw[:, :DN_HEAD_DIM]
        it["wq"] = [jnp.concatenate([uw[c0:c0 + CHUNK, DN_HEAD_DIM:].astype(BF16), it["qd"][c0:c0 + CHUNK]], axis=0)
                    for c0 in range(0, PAIR, CHUNK)]
        it["qkk"] = [jnp.concatenate([it["qk"][c0:c0 + CHUNK], it["kdt"]], axis=0)
                     for c0 in range(0, PAIR, CHUNK)]

    s_cur = [state[h] for h in range(N_DN_HEADS)]
    o_rows = [[None] * (tt // CHUNK) for _ in range(N_DN_HEADS)]
    for d in range(n_pairs):
        for c in range(PAIR // CHUNK):
            yield
            c0 = c * CHUNK
            row = d * PAIR + c0
            for h0 in range(0, N_DN_HEADS, 2):
                its = [insts[h * n_pairs + d] for h in (h0, h0 + 1)]
                ss = [s_cur[h0], s_cur[h0 + 1]]
                ws = dot_pair(its[0]["wq"][c], its[1]["wq"][c], ss[0].astype(BF16), ss[1].astype(BF16))
                v_pads = []
                for it, wsq in zip(its, ws):
                    v_new = it["u"][c0:c0 + CHUNK] - wsq[:CHUNK]
                    parts = [zeros_chunk] * (PAIR // CHUNK)
                    parts[c] = v_new
                    v_pads.append(jnp.concatenate(parts, axis=0).astype(BF16))
                ov = dot_pair(its[0]["qkk"][c], its[1]["qkk"][c], v_pads[0], v_pads[1])
                for j, h in enumerate((h0, h0 + 1)):
                    o_rows[h][d * (PAIR // CHUNK) + c] = ws[j][CHUNK:] + ov[j][:CHUNK]
                    gt = egl[row:row + 1, A_LANE0 + h:A_LANE0 + h + 1]
                    s_cur[h] = ss[j] * gt + ov[j][CHUNK:]
    for h in range(N_DN_HEADS):
        state[h] = s_cur[h]

    yield
    dn_parts = []
    for h in range(N_DN_HEADS):
        o = jnp.concatenate(o_rows[h], axis=0)
        zh = z[:, h * LANES:(h + 1) * LANES]
        dn_parts.append(((o * _rms_scale(o) * dnw_ref[...]) * _silu(zh)).astype(BF16))

    pos = (t * tt + lax.broadcasted_iota(jnp.int32, (tt, 1), 0) + 1).astype(F32)
    pool_parts = []
    for gi, win in enumerate(POOL_WINDOWS):
        yield
        cols = slice(gi * POOL_GROUP_DIM, (gi + 1) * POOL_GROUP_DIM)
        cur = p_ext[gi, POOL_HALO:POOL_HALO + tt, :]
        acc = cur
        for sft in range(1, win):
            acc = acc + p_ext[gi, pl.ds(POOL_HALO - sft, tt), :]
        mixed = acc / jnp.minimum(pos, float(win)) - cur
        y = _dot(mixed.astype(BF16), poolw_ref[gi]) * pscale_ref[:, cols]
        pool_parts.append(y.astype(BF16))

    yield
    x1 = x +_dot(jnp.concatenate(dn_parts + pool_parts, axis=1), wout_ref[...])
    x1_ref[...] = x1

    qkv_ext[:, 0:CONV_HALO, :] = qkv_ext[:, tt:tt + CONV_HALO, :]
    p_ext[:, 0:POOL_HALO, :] = p_ext[:, tt:tt + POOL_HALO, :]

    hn = x1 * _rms_scale(x1) * fnw_ref[...]
    hn_hi, hn_lo = _hi_lo(hn)
    hh = _dot(hn_hi, wr_ref[...])
    logits = hh[:, :LANES] + (hh[:, LANES:] + _dot(hn_lo, wr_ref[:, :LANES])) + br_ref[...]
    yield
    lt = logits.T[:N_EXPERTS]
    eidx = lax.broadcasted_iota(jnp.int32, (N_EXPERTS, tt), 0).astype(F32)
    work = lt
    vals, idxs = [], []
    for _ in range(TOP_K):
        m = jnp.max(work, axis=0, keepdims=True)
        idx = jnp.min(jnp.where(work == m, eidx, float(LANES)), axis=0, keepdims=True)
        vals.append(m)
        idxs.append(idx)
        work = jnp.where(eidx == idx, -jnp.inf, work)
    exps = [jnp.exp(v - vals[0]) for v in vals]
    denom = exps[0] + exps[1] + exps[2] + exps[3]

    yield
    onehot = jnp.zeros((N_EXPERTS, tt), F32)
    for idx in idxs:
        onehot = onehot + jnp.where(eidx == idx, 1.0, 0.0)
    earlier = jnp.where(ri < ci, 1.0, 0.0).astype(BF16)
    within = _dot(onehot.astype(BF16), earlier)
    tile_cnt = jnp.sum(onehot, axis=1, keepdims=True)
    n_copies = jnp.floor((tile_cnt + float(COPY_ROWS - 1)) * (1.0 / COPY_ROWS))
    er = lax.broadcasted_iota(jnp.int32, (N_EXPERTS, N_EXPERTS), 0)
    ec = lax.broadcasted_iota(jnp.int32, (N_EXPERTS, N_EXPERTS), 1)
    lower_expert = jnp.where(ec < er, 1.0, 0.0).astype(BF16)
    run_start = float(COPY_ROWS) * _dot(
        lower_expert, jnp.broadcast_to(n_copies, (N_EXPERTS, LANES)).astype(BF16))[:, 0:1]
    slot_of = run_start + within
    srow = lax.broadcasted_iota(jnp.int32, (SUBLANES, tt), 0)
    route = jnp.zeros((SUBLANES, tt), F32)
    gates = jnp.zeros((SUBLANES, tt), F32)
    for kk in range(TOP_K):
        slot = jnp.sum(jnp.where(eidx == idxs[kk], slot_of, 0.0), axis=0, keepdims=True)
        route = jnp.where(srow == kk, slot * float(ROW_SUBLANES), route)
        gates = jnp.where(srow == kk, exps[kk] / denom, gates)
    route_ref[...] = route.astype(jnp.int32)
    gate_ref[...] = gates
    mlane = lax.broadcasted_iota(jnp.int32, (N_EXPERTS, LANES), 1)
    meta_ref[...] = jnp.where(mlane == 0, tile_cnt, jnp.where(mlane == 1, cnt[...], 0.0))
    cnt[...] = cnt[...] + tile_cnt


def _mix_kernel(x_ref, anw_ref, win_ref, convw_ref, alog_ref, dtb_ref, dnw_ref, poolw_ref, pscale_ref,
                wout_ref, fnw_ref, wr_ref, br_ref,
                x1_ref, route_ref, gate_ref, meta_ref, xs_ref,
                qkv_ext, p_ext, state, cnt, zero_buf, zero_sem, *, zero_per_step):
    b = pl.program_id(0)
    t = pl.program_id(1)
    first_step = jnp.logical_and(b == 0, t == 0)

    @pl.when(first_step)
    def _():
        zero_buf[...] = jnp.zeros(zero_buf.shape, F32)
        cnt[...] = jnp.zeros(cnt.shape, F32)
    fill_rows = zero_buf.shape[0]
    n_fill = xs_ref.shape[0] // fill_rows
    step = b * pl.num_programs(1) + t

    def zero_copy(j, go):
        idx = step * zero_per_step + j

        @pl.when(idx < n_fill)
        def _():
            go(pltpu.make_async_copy(
                zero_buf, xs_ref.at[pl.ds(pl.multiple_of(idx * fill_rows, SUBLANES), fill_rows)], zero_sem))
    for j in range(zero_per_step):
        zero_copy(j, lambda cp: cp.start())

    @pl.when(t == 0)
    def _():
        qkv_ext[:, :, 0:CONV_HALO, :] = jnp.zeros((SEQ_PER_STEP, 3 * N_DN_HEADS, CONV_HALO, LANES), F32)
        p_ext[:, :, 0:POOL_HALO, :] = jnp.zeros((SEQ_PER_STEP, N_POOL_GROUPS, POOL_HALO, LANES), F32)
        state[...] = jnp.zeros(state.shape, F32)

    tiles = [
        _mix_tile(t, x_ref.at[sq], anw_ref, win_ref, convw_ref, alog_ref, dtb_ref, dnw_ref, poolw_ref, pscale_ref,
                  wout_ref, fnw_ref, wr_ref, br_ref,
                  x1_ref.at[sq], route_ref.at[sq], gate_ref.at[sq], meta_ref.at[sq],
                  qkv_ext.at[sq], p_ext.at[sq], state.at[sq], cnt)
        for sq in range(SEQ_PER_STEP)]
    live = [True] * SEQ_PER_STEP
    tick = 0
    while any(live):
        for sq, gen in enumerate(tiles):
            if live[sq] and tick >= sq * PHASE_SKEW:
                live[sq] = next(gen, "done") != "done"
        tick += 1
    for j in range(zero_per_step):
        zero_copy(j, lambda cp: cp.wait())


def _mix_call(x, anw, win_r, convw, alog_p, dtb_p, dnw, poolw, pscale, wout, fnw, wr_p, br_p, xs_sublanes):
    bsz, seq, d = x.shape
    n_t = seq // TILE
    sps = SEQ_PER_STEP
    n_b = bsz // sps
    x = x.reshape(sps, n_b, seq, d)
    fill_rows = BLOCK_SUBLANES
    assert xs_sublanes % fill_rows == 0
    zero_per_step = -(-(xs_sublanes // fill_rows) // (n_b * n_t))
    const2 = lambda b, t: (0, 0)
    tok_map = lambda b, t: (0, b * n_t + t, 0)
    seq_map = lambda b, t: (0, b, t, 0)
    in_specs = [
        pl.BlockSpec((sps, None, TILE, d), seq_map),
        pl.BlockSpec(anw.shape, const2),
        pl.BlockSpec(win_r.shape, const2),
        pl.BlockSpec(convw.shape, const2),
        pl.BlockSpec(alog_p.shape, const2),
        pl.BlockSpec(dtb_p.shape, const2),
        pl.BlockSpec(dnw.shape, const2),
        pl.BlockSpec(poolw.shape, lambda b, t: (0, 0, 0)),
        pl.BlockSpec(pscale.shape, const2),
        pl.BlockSpec(wout.shape, const2),
        pl.BlockSpec(fnw.shape, const2),
        pl.BlockSpec(wr_p.shape, const2),
        pl.BlockSpec(br_p.shape, const2),
    ]
    out_shape = (
        jax.ShapeDtypeStruct((sps, n_b, seq, d), F32),
        jax.ShapeDtypeStruct((sps, n_b * n_t * SUBLANES, TILE), jnp.int32),
        jax.ShapeDtypeStruct((sps, n_b * n_t * SUBLANES, TILE), F32),
        jax.ShapeDtypeStruct((sps, n_b * n_t * N_EXPERTS, LANES), F32),
        jax.ShapeDtypeStruct((xs_sublanes, LANES), F32),
    )
    out_specs = (
        pl.BlockSpec((sps, None, TILE, d), seq_map),
        pl.BlockSpec((sps, SUBLANES, TILE), tok_map),
        pl.BlockSpec((sps, SUBLANES, TILE), tok_map),
        pl.BlockSpec((sps, N_EXPERTS, LANES), tok_map),
        pl.BlockSpec(memory_space=pl.ANY),
    )
    scratch = [
        pltpu.VMEM((sps, 3 * N_DN_HEADS, CONV_HALO + TILE, LANES), F32),
        pltpu.VMEM((sps, N_POOL_GROUPS, POOL_HALO + TILE, LANES), F32),
        pltpu.VMEM((sps, N_DN_HEADS, DN_HEAD_DIM, DN_HEAD_DIM), F32),
        pltpu.VMEM((N_EXPERTS, LANES), F32),
        pltpu.VMEM((fill_rows, LANES), F32),
        pltpu.SemaphoreType.DMA(()),
    ]
    return pl.pallas_call(
        functools.partial(_mix_kernel, zero_per_step=zero_per_step),
        grid=(n_b, n_t),
        in_specs=in_specs,
        out_specs=out_specs,
        out_shape=out_shape,
        scratch_shapes=scratch,
        compiler_params=pltpu.CompilerParams(
            dimension_semantics=("arbitrary", "arbitrary"), vmem_limit_bytes=VMEM_LIMIT_BYTES),
        name="mix",
    )(x, anw, win_r, convw, alog_p, dtb_p, dnw, poolw, pscale, wout, fnw, wr_p, br_p)


COPY_SUBLANES = COPY_ROWS * ROW_SUBLANES


COPY_TABLE = 256
assert SLOTS // COPY_ROWS < COPY_TABLE
LOOP_UNROLL = 8


def _for_count(n, fn):
    def main(i, carry):
        for u in range(LOOP_UNROLL):
            fn(i * LOOP_UNROLL + u)
        return carry
    n_main = lax.shift_right_logical(n, LOOP_UNROLL.bit_length() - 1)
    lax.fori_loop(0, n_main, main, 0)

    def tail(i, carry):
        fn(i)
        return carry
    lax.fori_loop(n_main * LOOP_UNROLL, n, tail, 0)


def _run_copies(dst_ref, local_buf, global_ref, sem, to_global):
    n = dst_ref[COPY_TABLE - 1]

    def start(c):
        loc = local_buf.at[pl.ds(pl.multiple_of(c * COPY_SUBLANES, COPY_SUBLANES), COPY_SUBLANES)]
        glo = global_ref.at[pl.ds(pl.multiple_of(dst_ref[c], SUBLANES), COPY_SUBLANES)]
        if to_global:
            pltpu.make_async_copy(loc, glo, sem).start()
        else:
            pltpu.make_async_copy(glo, loc, sem).start()
    _for_count(n, start)
    return n


def _wait_copies(n, local_buf, global_ref, sem):
    def wait(i):
        del i
        pltpu.make_async_copy(local_buf.at[pl.ds(0, COPY_SUBLANES)], global_ref.at[pl.ds(0, COPY_SUBLANES)], sem).wait()
    _for_count(n, wait)


def _dispatch_kernel(slot_ref, dst_ref, x1_ref, fnw_ref, xs_in_ref, xs_ref,
                     tok_tiles, sorted_buf, inflight, sem):
    del xs_in_ref
    step = pl.program_id(0)
    buf = sorted_buf.at[step % 2]
    x1 = x1_ref[...]
    hn = x1 * _rms_scale(x1) * fnw_ref[...]
    for s in range(ROW_SUBLANES):
        tok_tiles[pl.ds(s, TILE, stride=ROW_SUBLANES), :] = hn[:, s * LANES:(s + 1) * LANES]
    @pl.when(step < 2)
    def _():
        buf[...] = jnp.zeros(buf.shape, F32)

    def token_body(i, carry):
        tile = tok_tiles[pl.ds(pl.multiple_of(i * ROW_SUBLANES, ROW_SUBLANES), ROW_SUBLANES), :]
        for kk in range(TOP_K):
            slot_row = slot_ref[kk * TILE + i]
            buf[pl.ds(pl.multiple_of(slot_row, ROW_SUBLANES), ROW_SUBLANES), :] = tile
        return carry
    lax.fori_loop(0, TILE, token_body, 0, unroll=8)

    @pl.when(step > 0)
    def _():
        _wait_copies(inflight[0], buf, xs_ref, sem)
    inflight[0] = _run_copies(dst_ref, buf, xs_ref, sem, to_global=True)

    @pl.when(step == pl.num_programs(0) - 1)
    def _():
        _wait_copies(inflight[0], buf, xs_ref, sem)


def _smem_spec(n):
    return pl.BlockSpec((n,), lambda i: (i,), memory_space=pltpu.SMEM)


def _dispatch_call(slots, dst, x1_flat, fnw, xs_zero):
    n_tok, d = x1_flat.shape
    n_tiles = n_tok // TILE
    tile_of = lambda i: (i % SEQ_PER_STEP) * (n_tiles // SEQ_PER_STEP) + i // SEQ_PER_STEP
    return pl.pallas_call(
        _dispatch_kernel,
        grid=(n_tiles,),
        in_specs=[
            pl.BlockSpec((TILE * TOP_K,), lambda i: (tile_of(i),), memory_space=pltpu.SMEM),
            pl.BlockSpec((COPY_TABLE,), lambda i: (tile_of(i),), memory_space=pltpu.SMEM),
            pl.BlockSpec((TILE, d), lambda i: (tile_of(i), 0)),
            pl.BlockSpec(fnw.shape, lambda i: (0, 0)),
            pl.BlockSpec(memory_space=pl.ANY),
        ],
        out_specs=pl.BlockSpec(memory_space=pl.ANY),
        out_shape=jax.ShapeDtypeStruct(xs_zero.shape, F32),
        scratch_shapes=[pltpu.VMEM((TILE * ROW_SUBLANES, LANES), F32),
                        pltpu.VMEM((2, SLOTS * ROW_SUBLANES, LANES), F32),
                        pltpu.SMEM((1,), jnp.int32),
                        pltpu.SemaphoreType.DMA(())],
        input_output_aliases={4: 0},
        compiler_params=pltpu.CompilerParams(dimension_semantics=("arbitrary",)),
        name="dispatch",
    )(slots, dst, x1_flat, fnw, xs_zero)


def _combine_kernel(slot_ref, gatev_ref, dst_ref, dst_next_ref, x1_ref, fw_ref, ys_ref,
                    out_ref, sorted_buf, tok_tiles, inflight, sems):
    step = pl.program_id(0)
    cur = step % 2
    nxt = 1 - cur
    buf = sorted_buf.at[cur]

    @pl.when(step == 0)
    def _():
        inflight[0] = _run_copies(dst_ref, sorted_buf.at[0], ys_ref, sems.at[0], to_global=False)

    @pl.when(step + 1 < pl.num_programs(0))
    def _():
        inflight[nxt] = _run_copies(dst_next_ref, sorted_buf.at[nxt], ys_ref, sems.at[nxt], to_global=False)
    _wait_copies(inflight[cur], buf, ys_ref, sems.at[cur])

    def token_body(i, carry):
        acc = None
        for kk in range(TOP_K):
            slot_row = slot_ref[kk * TILE + i]
            row = buf[pl.ds(pl.multiple_of(slot_row, ROW_SUBLANES), ROW_SUBLANES), :]
            term = row * gatev_ref[kk * TILE + i]
            acc = term if acc is None else acc + term
        tok_tiles[pl.ds(pl.multiple_of(i * ROW_SUBLANES, ROW_SUBLANES), ROW_SUBLANES), :] = acc
        return carry
    lax.fori_loop(0, TILE, token_body, 0, unroll=8)

    x1 = x1_ref[...]
    cols = []
    ssq = None
    for s in range(ROW_SUBLANES):
        c = x1[:, s * LANES:(s + 1) * LANES] + tok_tiles[pl.ds(s, TILE, stride=ROW_SUBLANES), :]
        cols.append(c)
        sq = jnp.sum(c * c, axis=-1, keepdims=True)
        ssq = sq if ssq is None else ssq + sq
    scale = lax.rsqrt(ssq * (1.0 / D_MODEL) + RMS_EPS)
    for s in range(ROW_SUBLANES):
        out_ref[:, s * LANES:(s + 1) * LANES] = cols[s] * scale * fw_ref[:, s * LANES:(s + 1) * LANES]


def _combine_call(slots, gatev, dst, x1_flat, fw, ys):
    n_tok, d = x1_flat.shape
    n_steps = n_tok // TILE
    next_spec = pl.BlockSpec((COPY_TABLE,), lambda i: (jnp.minimum(i + 1, n_steps - 1),), memory_space=pltpu.SMEM)
    return pl.pallas_call(
        _combine_kernel,
        grid=(n_steps,),
        in_specs=[
            _smem_spec(TILE * TOP_K), _smem_spec(TILE * TOP_K), _smem_spec(COPY_TABLE), next_spec,
            pl.BlockSpec((TILE, d), lambda i: (i, 0)),
            pl.BlockSpec(fw.shape, lambda i: (0, 0)),
            pl.BlockSpec(memory_space=pl.ANY),
        ],
        out_specs=pl.BlockSpec((TILE, d), lambda i: (i, 0)),
        out_shape=jax.ShapeDtypeStruct((n_tok, d), F32),
        scratch_shapes=[pltpu.VMEM((2, SLOTS * ROW_SUBLANES, LANES), F32),
                        pltpu.VMEM((TILE * ROW_SUBLANES, LANES), F32),
                        pltpu.SMEM((2,), jnp.int32),
                        pltpu.SemaphoreType.DMA((2,))],
        compiler_params=pltpu.CompilerParams(dimension_semantics=("arbitrary",)),
        name="combine",
    )(slots, gatev, dst, dst, x1_flat, fw, ys)


BLOCK_SUBLANES = MOE_ROWS * ROW_SUBLANES


def _expert_mlp(x_tiles, y_tiles, wgu16, wd16, bgu_ref, bd_ref):
    xb = jnp.concatenate(
        [x_tiles[pl.ds(s, MOE_ROWS, stride=ROW_SUBLANES), :].astype(BF16) for s in range(ROW_SUBLANES)], axis=1)
    acc = None
    for j in range(D_EXPERT // FF_CHUNK):
        c0 = j * FF_CHUNK
        gate = _dot(xb, wgu16[:, c0:c0 + FF_CHUNK]) + bgu_ref[:, c0:c0 + FF_CHUNK]
        up = _dot(xb, wgu16[:, D_EXPERT + c0:D_EXPERT + c0 + FF_CHUNK]) + bgu_ref[:, D_EXPERT + c0:D_EXPERT + c0 + FF_CHUNK]
        gate = jnp.minimum(gate, SWIGLU_LIMIT)
        up = jnp.clip(up, -SWIGLU_LIMIT, SWIGLU_LIMIT)
        act = (up + 1.0) * gate * jax.nn.sigmoid(SWIGLU_ALPHA * gate)
        term = _dot(act.astype(BF16), wd16[c0:c0 + FF_CHUNK, :])
        acc = term if acc is None else acc + term
    y = acc + bd_ref[...]
    for s in range(ROW_SUBLANES):
        y_tiles[pl.ds(s, MOE_ROWS, stride=ROW_SUBLANES), :] = y[:, s * LANES:(s + 1) * LANES]


def _moe_kernel(be_ref, nu_ref, xs_ref, wgu_ref, bgu_ref, wd_ref, bd_ref, ys_ref, wgu16, wd16):
    i = pl.program_id(0)

    @pl.when(i < nu_ref[0])
    def _():
        prev = be_ref[jnp.maximum(i - 1, 0)]
        new_expert = jnp.logical_or(i == 0, be_ref[i] != prev)

        @pl.when(new_expert)
        def _():
            wgu16[...] = wgu_ref[...].astype(BF16)
            wd16[...] = wd_ref[...].astype(BF16)
        _expert_mlp(xs_ref, ys_ref, wgu16, wd16, bgu_ref, bd_ref)

    @pl.when(i >= nu_ref[0])
    def _():
        ys_ref[...] = jnp.zeros(ys_ref.shape, F32)


def _moe_call(block_e, n_used, xs, wgu, bgu, wd, bd):
    d = D_MODEL
    n_blocks = xs.shape[0] // BLOCK_SUBLANES
    blk = lambda i, be, nu: (jnp.minimum(i, nu[0] - 1), 0)
    exp3 = lambda i, be, nu: (be[jnp.minimum(i, nu[0] - 1)], 0, 0)
    grid_spec = pltpu.PrefetchScalarGridSpec(
        num_scalar_prefetch=2,
        grid=(n_blocks,),
        in_specs=[
            pl.BlockSpec((BLOCK_SUBLANES, LANES), blk),
            pl.BlockSpec((None, d, 2 * D_EXPERT), exp3),
            pl.BlockSpec((None, 1, 2 * D_EXPERT), exp3),
            pl.BlockSpec((None, D_EXPERT, d), exp3),
            pl.BlockSpec((None, 1, d), exp3),
        ],
        out_specs=pl.BlockSpec((BLOCK_SUBLANES, LANES), lambda i, be, nu: (i, 0)),
        scratch_shapes=[pltpu.VMEM((d, 2 * D_EXPERT), BF16), pltpu.VMEM((D_EXPERT, d), BF16)],
    )
    return pl.pallas_call(
        _moe_kernel,
        grid_spec=grid_spec,
        out_shape=jax.ShapeDtypeStruct(xs.shape, F32),
        compiler_params=pltpu.CompilerParams(
            dimension_semantics=("arbitrary",), vmem_limit_bytes=VMEM_LIMIT_BYTES),
        name="moe",
    )(block_e, n_used, xs, wgu, bgu, wd, bd)


def _pad_lanes(v, lane0):
    out = jnp.zeros((1, LANES), F32)
    return lax.dynamic_update_slice(out, v.reshape(1, -1).astype(F32), (0, lane0))


def kernel(x, attn_norm_w, w_in, conv_w, a_log, dt_bias, dn_norm_w, pool_w, pool_scale, w_out, ffn_norm_w,
           w_router, b_router, w_gate_up, b_gate_up, w_down, b_down, final_norm_w):
    bsz, seq, d = x.shape
    n_tok = bsz * seq
    assert w_in.shape[0] == 1, "the combine kernel fuses the final norm, so exactly one layer is supported"
    assert d == D_MODEL and seq % TILE == 0 and bsz % SEQ_PER_STEP == 0
    l = 0

    s_z, s_a = 4 * D_DN, 4 * D_DN + 2 * N_DN_HEADS
    ba_cols = jnp.pad(w_in[l][:, s_z:s_a], ((0, 0), (0, LANES - 2 * N_DN_HEADS)))
    win_r = jnp.concatenate([w_in[l][:, :s_z], w_in[l][:, s_a:], ba_cols], axis=1).astype(BF16)
    wr32 = jnp.pad(w_router[l].astype(F32), ((0, 0), (0, LANES - N_EXPERTS)))
    wr_hi = lax.bitcast_convert_type(
        lax.bitcast_convert_type(wr32, jnp.uint32) & jnp.uint32(0xFFFF0000), F32)
    wr_p = jnp.concatenate([wr_hi.astype(BF16), (wr32 - wr_hi).astype(BF16)], axis=1)
    br_p = _pad_lanes(b_router[l], 0)
    fnw = ffn_norm_w[l].reshape(1, d)

    n_blocks = (n_tok * TOP_K + N_EXPERTS * (MOE_ROWS + COPY_ROWS - 2)) // MOE_ROWS + 1
    n_rows = n_blocks * MOE_ROWS

    x1, route, gates, meta, xs_zero = _mix_call(
        x, attn_norm_w[l].reshape(1, d), win_r, conv_w[l], _pad_lanes(a_log[l], A_LANE0),
        _pad_lanes(dt_bias[l], A_LANE0), dn_norm_w[l].reshape(1, DN_HEAD_DIM), pool_w[l].astype(BF16),
        pool_scale[l].reshape(1, D_POOL), w_out[l].astype(BF16), fnw, wr_p, br_p, n_rows * ROW_SUBLANES)

    n_tiles = n_tok // TILE
    meta = meta.reshape(n_tiles, N_EXPERTS, LANES)[:, :, :2].astype(jnp.int32)
    tile_cnt, tile_before = meta[:, :, 0], meta[:, :, 1]
    total = jnp.sum(tile_cnt, axis=0)
    padded = jnp.where(total > 0, ((total + COPY_ROWS - 1 + MOE_ROWS - 1) // MOE_ROWS) * MOE_ROWS, 0)
    pad_end = jnp.cumsum(padded)
    pad_start = pad_end - padded
    run_base = pad_start[None, :] + tile_before
    ncopy = (tile_cnt + COPY_ROWS - 1) // COPY_ROWS
    copy_end = jnp.cumsum(ncopy, axis=1)
    copy_start = copy_end - ncopy
    c_idx = jnp.arange(COPY_TABLE, dtype=jnp.int32)[None, :, None]
    in_run = jnp.logical_and(c_idx >= copy_start[:, None, :], c_idx < copy_end[:, None, :])
    dst_row = jnp.sum(jnp.where(in_run, run_base[:, None, :] + COPY_ROWS * (c_idx - copy_start[:, None, :]), 0), axis=2)
    dst = (dst_row * ROW_SUBLANES).at[:, COPY_TABLE - 1].set(copy_end[:, -1]).reshape(-1)
    block_start = jnp.arange(n_blocks, dtype=jnp.int32) * MOE_ROWS
    block_e = jnp.minimum(jnp.sum(block_start[:, None] >= pad_end[None, :], axis=1), N_EXPERTS - 1).astype(jnp.int32)
    n_used = (pad_end[-1:] // MOE_ROWS).astype(jnp.int32)

    def per_tile_table(a):
        return a.reshape(n_tiles, SUBLANES, TILE)[:, :TOP_K].reshape(-1)
    slots = per_tile_table(route)
    gatev = per_tile_table(gates)

    x1_flat = x1.reshape(n_tok, d)
    xs = _dispatch_call(slots, dst, x1_flat, fnw, xs_zero)
    ys = _moe_call(block_e, n_used, xs, w_gate_up[l], b_gate_up[l].reshape(N_EXPERTS, 1, -1),
                   w_down[l], b_down[l].reshape(N_EXPERTS, 1, -1))
    out = _combine_call(slots, gatev, dst, x1_flat, final_norm_w.reshape(1, d), ys)
    return out.reshape(bsz, seq, d)
```

```python
import functools

import jax
import jax.numpy as jnp
from jax import lax
from jax.experimental import pallas as pl
from jax.experimental.pallas import tpu as pltpu

F32 = jnp.float32
BF16 = jnp.bfloat16

D_MODEL = 1024
N_DN_HEADS = 4
DN_HEAD_DIM = 128
D_DN = N_DN_HEADS * DN_HEAD_DIM
CONV_K = 4
CHUNK = 64
POOL_WINDOWS = (2, 4, 8, 16)
N_POOL_GROUPS = len(POOL_WINDOWS)
D_POOL = D_MODEL - D_DN
POOL_GROUP_DIM = D_POOL // N_POOL_GROUPS
N_EXPERTS = 32
TOP_K = 4
D_EXPERT = D_MODEL
SWIGLU_LIMIT = 7.0
SWIGLU_ALPHA = 1.702
RMS_EPS = 1e-6
L2_EPS = 1e-6

LANES = 128
SUBLANES = 8
ROW_SUBLANES = D_MODEL // LANES
VMEM_LIMIT_BYTES = 56 * 1024 * 1024

TILE = 256
PAIR = 2 * CHUNK
CONV_HALO = SUBLANES
POOL_HALO = 2 * SUBLANES
MOE_ROWS = 608
FF_CHUNK = 512
COPY_ROWS = 8
SEQ_PER_STEP = 2
assert MOE_ROWS % 16 == 0
SLOTS = TILE * TOP_K + N_EXPERTS * COPY_ROWS
assert TILE * TOP_K // COPY_ROWS + 1 <= 256

COL_QKV = 0
COL_Z = 3 * D_DN
COL_P = 4 * D_DN
COL_BA = 4 * D_DN + D_POOL
D_IN_R = COL_BA + LANES
A_LANE0 = N_DN_HEADS


def _dot(a, b):
    return jnp.dot(a, b, preferred_element_type=F32)


def _hi_lo(x):
    hi = x.astype(BF16)
    lo = (x - hi.astype(F32)).astype(BF16)
    return hi, lo


def _dot_exact_lhs(m_bf16, x):
    hi = x.astype(BF16)
    r = x - hi.astype(F32)
    mid = r.astype(BF16)
    lo = (r - mid.astype(F32)).astype(BF16)
    return _dot(m_bf16, hi) + (_dot(m_bf16, mid) + _dot(m_bf16, lo))


def _rms_scale(x):
    return lax.rsqrt(jnp.mean(x * x, axis=-1, keepdims=True) + RMS_EPS)


def _silu(x):
    return x * jax.nn.sigmoid(x)


def _softplus(x):
    return jnp.maximum(x, 0.0) + jnp.log1p(jnp.exp(-jnp.abs(x)))


def _mix_tile(t, x_ref, anw_ref, win_ref, convw_ref, alog_ref, dtb_ref, dnw_ref, poolw_ref, pscale_ref,
              wout_ref, fnw_ref, wr_ref, br_ref,
              x1_ref, hnt_ref, route_ref, gate_ref, meta_ref,
              qkv_ext, p_ext, state, cnt):
    tt = TILE

    x = x_ref[...]
    hb = (x * _rms_scale(x) * anw_ref[...]).astype(BF16)
    qkv_pre = _dot(hb, win_ref[:, COL_QKV:COL_Z])
    for c in range(3 * N_DN_HEADS):
        qkv_ext[c, CONV_HALO:CONV_HALO + tt, :] = qkv_pre[:, c * LANES:(c + 1) * LANES]
    yield
    z = _dot(hb, win_ref[:, COL_Z:COL_P])
    p_pre = _dot(hb, win_ref[:, COL_P:COL_BA])
    for c in range(N_POOL_GROUPS):
        p_ext[c, POOL_HALO:POOL_HALO + tt, :] = p_pre[:, c * LANES:(c + 1) * LANES]
    ba = _dot(hb, win_ref[:, COL_BA:D_IN_R])

    yield
    beta = jax.nn.sigmoid(ba)
    g = -jnp.exp(alog_ref[...]) * _softplus(ba + dtb_ref[...])
    ri = lax.broadcasted_iota(jnp.int32, (tt, tt), 0)
    ci = lax.broadcasted_iota(jnp.int32, (tt, tt), 1)
    same_chunk = (ri // CHUNK) == (ci // CHUNK)
    l_incl = jnp.where(jnp.logical_and(same_chunk, ri >= ci), 1.0, 0.0).astype(BF16)
    gc = _dot_exact_lhs(l_incl, g)
    gl = jnp.concatenate(
        [jnp.broadcast_to(gc[c * CHUNK + CHUNK - 1:c * CHUNK + CHUNK, :], (CHUNK, LANES)) for c in range(tt // CHUNK)],
        axis=0)
    eg = jnp.exp(gc)
    ekd = jnp.exp(gl - gc)
    egl = jnp.exp(gl)
    gct = gc.T

    pr = lax.broadcasted_iota(jnp.int32, (PAIR, PAIR), 0)
    pc = lax.broadcasted_iota(jnp.int32, (PAIR, PAIR), 1)
    pair_same = (pr // CHUNK) == (pc // CHUNK)
    causal = jnp.logical_and(pair_same, pr >= pc)
    strict = jnp.logical_and(pair_same, pr > pc)
    eye = jnp.where(pr == pc, 1.0, 0.0).astype(F32)
    zeros_chunk = jnp.zeros((CHUNK, DN_HEAD_DIM), F32)

    def conv_silu(col):
        acc = None
        for j in range(CONV_K):
            rows = pl.ds(CONV_HALO - (CONV_K - 1) + j, tt)
            term = qkv_ext[col, rows, :] * convw_ref[j:j + 1, col * LANES:(col + 1) * LANES]
            acc = term if acc is None else acc + term
        return _silu(acc)

    def l2n(v):
        return v * lax.rsqrt(jnp.sum(v * v, axis=-1, keepdims=True) + L2_EPS)

    def block_diag(b0, b1):
        zero = jnp.zeros_like(b0)
        return jnp.concatenate([jnp.concatenate([b0, zero], axis=1), jnp.concatenate([zero, b1], axis=1)], axis=0)

    def dot_pair(a0, a1, b0, b1):
        out = _dot(jnp.concatenate([a0, a1], axis=1), block_diag(b0, b1))
        return out[:, :LANES], out[:, LANES:]

    n_pairs = tt // PAIR
    assert n_pairs == 2 and N_DN_HEADS % 2 == 0
    insts = []
    for h in range(N_DN_HEADS):
        yield
        q_all = l2n(conv_silu(h)) * (DN_HEAD_DIM ** -0.5)
        k_all = l2n(conv_silu(N_DN_HEADS + h))
        v_all = conv_silu(2 * N_DN_HEADS + h)
        yield
        la = A_LANE0 + h
        pre = []
        for d in range(n_pairs):
            r0 = d * PAIR
            q = q_all[r0:r0 + PAIR]
            k = k_all[r0:r0 + PAIR]
            v = v_all[r0:r0 + PAIR]
            bcol = beta[r0:r0 + PAIR, h:h + 1]
            gcol = gc[r0:r0 + PAIR, la:la + 1]
            egcol = eg[r0:r0 + PAIR, la:la + 1]
            ekdcol = ekd[r0:r0 + PAIR, la:la + 1]
            grow = gct[la:la + 1, r0:r0 + PAIR]
            diff = gcol - grow
            decay = jnp.where(causal, jnp.exp(jnp.where(causal, diff, 0.0)), 0.0)
            kb = k * bcol
            pre.append(dict(
                decay=decay, kq=jnp.concatenate([kb, q], axis=0).astype(BF16), kt=k.T.astype(BF16),
                rhs=jnp.concatenate([(v * bcol).astype(BF16), (kb * egcol).astype(BF16)], axis=1),
                qd=(q * egcol).astype(BF16), kdt=(k * ekdcol).T.astype(BF16)))
        kk0, kk1 = dot_pair(pre[0]["kq"], pre[1]["kq"], pre[0]["kt"], pre[1]["kt"])
        for d, kk in enumerate((kk0, kk1)):
            it = pre[d]
            it["a"] = jnp.where(strict, kk[:PAIR] * it["decay"], 0.0)
            it["qk"] = jnp.where(causal, kk[PAIR:] * it["decay"], 0.0).astype(BF16)
            insts.append(it)

    rs = [-it["a"] for it in insts]
    p16 = [it["a"].astype(BF16) for it in insts]
    for _ in range(CHUNK.bit_length() - 2):
        yield
        pws = []
        for i in range(0, len(insts), 2):
            pws.extend(dot_pair(p16[i], p16[i + 1], p16[i], p16[i + 1]))
        p16 = [pw.astype(BF16) for pw in pws]
        yield
        rps = []
        for i in range(0, len(insts), 2):
            rps.extend(dot_pair(rs[i].astype(BF16), rs[i + 1].astype(BF16), p16[i], p16[i + 1]))
        rs = [r + pw + rp for r, pw, rp in zip(rs, pws, rps)]
    yield
    for it, r in zip(insts, rs):
        uw = _dot((eye + r).astype(BF16), it["rhs"])
        it["u"] = uw[:, :DN_HEAD_DIM]
        it["wq"] = [jnp.concatenate([uw[c0:c0 + CHUNK, DN_HEAD_DIM:].astype(BF16), it["qd"][c0:c0 + CHUNK]], axis=0)
                    for c0 in range(0, PAIR, CHUNK)]
        it["qkk"] = [jnp.concatenate([it["qk"][c0:c0 + CHUNK], it["kdt"]], axis=0)
                     for c0 in range(0, PAIR, CHUNK)]

    s_cur = [state[h] for h in range(N_DN_HEADS)]
    o_rows = [[None] * (tt // CHUNK) for _ in range(N_DN_HEADS)]
    for d in range(n_pairs):
        for c in range(PAIR // CHUNK):
            yield
            c0 = c * CHUNK
            row = d * PAIR + c0
            for h0 in range(0, N_DN_HEADS, 2):
                its = [insts[h * n_pairs + d] for h in (h0, h0 + 1)]
                ss = [s_cur[h0], s_cur[h0 + 1]]
                ws = dot_pair(its[0]["wq"][c], its[1]["wq"][c], ss[0].astype(BF16), ss[1].astype(BF16))
                v_pads = []
                for it, wsq in zip(its, ws):
                    v_new = it["u"][c0:c0 + CHUNK] - wsq[:CHUNK]
                    parts = [zeros_chunk] * (PAIR // CHUNK)
                    parts[c] = v_new
                    v_pads.append(jnp.concatenate(parts, axis=0).astype(BF16))
                ov = dot_pair(its[0]["qkk"][c], its[1]["qkk"][c], v_pads[0], v_pads[1])
                for j, h in enumerate((h0, h0 + 1)):
                    o_rows[h][d * (PAIR // CHUNK) + c] = ws[j][CHUNK:] + ov[j][:CHUNK]
                    gt = egl[row:row + 1, A_LANE0 + h:A_LANE0 + h + 1]
                    s_cur[h] = ss[j] * gt + ov[j][CHUNK:]
    for h in range(N_DN_HEADS):
        state[h] = s_cur[h]

    yield
    dn_parts = []
    for h in range(N_DN_HEADS):
        o = jnp.concatenate(o_rows[h], axis=0)
        zh = z[:, h * LANES:(h + 1) * LANES]
        dn_parts.append(((o * _rms_scale(o) * dnw_ref[...]) * _silu(zh)).astype(BF16))

    pos = (t * tt + lax.broadcasted_iota(jnp.int32, (tt, 1), 0) + 1).astype(F32)
    pool_parts = []
    for gi, win in enumerate(POOL_WINDOWS):
        yield
        cols = slice(gi * POOL_GROUP_DIM, (gi + 1) * POOL_GROUP_DIM)
        cur = p_ext[gi, POOL_HALO:POOL_HALO + tt, :]
        acc = cur
        for sft in range(1, win):
            acc = acc + p_ext[gi, pl.ds(POOL_HALO - sft, tt), :]
        mixed = acc / jnp.minimum(pos, float(win)) - cur
        y = _dot(mixed.astype(BF16), poolw_ref[gi]) * pscale_ref[:, cols]
        pool_parts.append(y.astype(BF16))

    yield
    x1 = x +_dot(jnp.concatenate(dn_parts + pool_parts, axis=1), wout_ref[...])
    x1_ref[...] = x1

    qkv_ext[:, 0:CONV_HALO, :] = qkv_ext[:, tt:tt + CONV_HALO, :]
    p_ext[:, 0:POOL_HALO, :] = p_ext[:, tt:tt + POOL_HALO, :]

    hn = x1 * _rms_scale(x1) * fnw_ref[...]
    for s in range(ROW_SUBLANES):
        hnt_ref[pl.ds(s, tt, stride=ROW_SUBLANES), :] = hn[:, s * LANES:(s + 1) * LANES]
    hn_hi, hn_lo = _hi_lo(hn)
    hh = _dot(hn_hi, wr_ref[...])
    logits = hh[:, :LANES] + (hh[:, LANES:] + _dot(hn_lo, wr_ref[:, :LANES])) + br_ref[...]
    yield
    lt = logits.T[:N_EXPERTS]
    eidx = lax.broadcasted_iota(jnp.int32, (N_EXPERTS, tt), 0).astype(F32)
    work = lt
    vals, idxs = [], []
    for _ in range(TOP_K):
        m = jnp.max(work, axis=0, keepdims=True)
        idx = jnp.min(jnp.where(work == m, eidx, float(LANES)), axis=0, keepdims=True)
        vals.append(m)
        idxs.append(idx)
        work = jnp.where(eidx == idx, -jnp.inf, work)
    exps = [jnp.exp(v - vals[0]) for v in vals]
    denom = exps[0] + exps[1] + exps[2] + exps[3]

    yield
    onehot = jnp.zeros((N_EXPERTS, tt), F32)
    for idx in idxs:
        onehot = onehot + jnp.where(eidx == idx, 1.0, 0.0)
    earlier = jnp.where(ri < ci, 1.0, 0.0).astype(BF16)
    within = _dot(onehot.astype(BF16), earlier)
    tile_cnt = jnp.sum(onehot, axis=1, keepdims=True)
    n_copies = jnp.floor((tile_cnt + float(COPY_ROWS - 1)) * (1.0 / COPY_ROWS))
    er = lax.broadcasted_iota(jnp.int32, (N_EXPERTS, N_EXPERTS), 0)
    ec = lax.broadcasted_iota(jnp.int32, (N_EXPERTS, N_EXPERTS), 1)
    lower_expert = jnp.where(ec < er, 1.0, 0.0).astype(BF16)
    run_start = float(COPY_ROWS) * _dot(
        lower_expert, jnp.broadcast_to(n_copies, (N_EXPERTS, LANES)).astype(BF16))[:, 0:1]
    slot_of = run_start + within
    srow = lax.broadcasted_iota(jnp.int32, (SUBLANES, tt), 0)
    route = jnp.zeros((SUBLANES, tt), F32)
    gates = jnp.zeros((SUBLANES, tt), F32)
    for kk in range(TOP_K):
        slot = jnp.sum(jnp.where(eidx == idxs[kk], slot_of, 0.0), axis=0, keepdims=True)
        route = jnp.where(srow == kk, slot * float(ROW_SUBLANES), route)
        gates = jnp.where(srow == kk, exps[kk] / denom, gates)
    route_ref[...] = route.astype(jnp.int32)
    gate_ref[...] = gates
    mlane = lax.broadcasted_iota(jnp.int32, (N_EXPERTS, LANES), 1)
    meta_ref[...] = jnp.where(mlane == 0, tile_cnt, jnp.where(mlane == 1, cnt[...], 0.0))
    cnt[...] = cnt[...] + tile_cnt


def _mix_kernel(x_ref, anw_ref, win_ref, convw_ref, alog_ref, dtb_ref, dnw_ref, poolw_ref, pscale_ref,
                wout_ref, fnw_ref, wr_ref, br_ref,
                x1_ref, hnt_ref, route_ref, gate_ref, meta_ref, xs_ref,
                qkv_ext, p_ext, state, cnt, zero_buf, zero_sem, *, zero_per_step):
    b = pl.program_id(0)
    t = pl.program_id(1)
    first_step = jnp.logical_and(b == 0, t == 0)

    @pl.when(first_step)
    def _():
        zero_buf[...] = jnp.zeros(zero_buf.shape, F32)
        cnt[...] = jnp.zeros(cnt.shape, F32)
    fill_rows = zero_buf.shape[0]
    n_fill = xs_ref.shape[0] // fill_rows
    step = b * pl.num_programs(1) + t

    def zero_copy(j, go):
        idx = step * zero_per_step + j

        @pl.when(idx < n_fill)
        def _():
            go(pltpu.make_async_copy(
                zero_buf, xs_ref.at[pl.ds(pl.multiple_of(idx * fill_rows, SUBLANES), fill_rows)], zero_sem))
    for j in range(zero_per_step):
        zero_copy(j, lambda cp: cp.start())

    @pl.when(t == 0)
    def _():
        qkv_ext[:, :, 0:CONV_HALO, :] = jnp.zeros((SEQ_PER_STEP, 3 * N_DN_HEADS, CONV_HALO, LANES), F32)
        p_ext[:, :, 0:POOL_HALO, :] = jnp.zeros((SEQ_PER_STEP, N_POOL_GROUPS, POOL_HALO, LANES), F32)
        state[...] = jnp.zeros(state.shape, F32)

    tiles = [
        _mix_tile(t, x_ref.at[sq], anw_ref, win_ref, convw_ref, alog_ref, dtb_ref, dnw_ref, poolw_ref, pscale_ref,
                  wout_ref, fnw_ref, wr_ref, br_ref,
                  x1_ref.at[sq], hnt_ref.at[sq], route_ref.at[sq], gate_ref.at[sq], meta_ref.at[sq],
                  qkv_ext.at[sq], p_ext.at[sq], state.at[sq], cnt)
        for sq in range(SEQ_PER_STEP)]
    live = [True] * SEQ_PER_STEP
    while any(live):
        for sq, gen in enumerate(tiles):
            if live[sq]:
                live[sq] = next(gen, "done") != "done"
    for j in range(zero_per_step):
        zero_copy(j, lambda cp: cp.wait())


def _mix_call(x, anw, win_r, convw, alog_p, dtb_p, dnw, poolw, pscale, wout, fnw, wr_p, br_p, xs_sublanes):
    bsz, seq, d = x.shape
    n_t = seq // TILE
    sps = SEQ_PER_STEP
    n_b = bsz // sps
    x = x.reshape(sps, n_b, seq, d)
    fill_rows = BLOCK_SUBLANES
    assert xs_sublanes % fill_rows == 0
    zero_per_step = -(-(xs_sublanes // fill_rows) // (n_b * n_t))
    const2 = lambda b, t: (0, 0)
    tok_map = lambda b, t: (0, b * n_t + t, 0)
    seq_map = lambda b, t: (0, b, t, 0)
    in_specs = [
        pl.BlockSpec((sps, None, TILE, d), seq_map),
        pl.BlockSpec(anw.shape, const2),
        pl.BlockSpec(win_r.shape, const2),
        pl.BlockSpec(convw.shape, const2),
        pl.BlockSpec(alog_p.shape, const2),
        pl.BlockSpec(dtb_p.shape, const2),
        pl.BlockSpec(dnw.shape, const2),
        pl.BlockSpec(poolw.shape, lambda b, t: (0, 0, 0)),
        pl.BlockSpec(pscale.shape, const2),
        pl.BlockSpec(wout.shape, const2),
        pl.BlockSpec(fnw.shape, const2),
        pl.BlockSpec(wr_p.shape, const2),
        pl.BlockSpec(br_p.shape, const2),
    ]
    out_shape = (
        jax.ShapeDtypeStruct((sps, n_b, seq, d), F32),
        jax.ShapeDtypeStruct((sps, n_b * seq * ROW_SUBLANES, LANES), F32),
        jax.ShapeDtypeStruct((sps, n_b * n_t * SUBLANES, TILE), jnp.int32),
        jax.ShapeDtypeStruct((sps, n_b * n_t * SUBLANES, TILE), F32),
        jax.ShapeDtypeStruct((sps, n_b * n_t * N_EXPERTS, LANES), F32),
        jax.ShapeDtypeStruct((xs_sublanes, LANES), F32),
    )
    out_specs = (
        pl.BlockSpec((sps, None, TILE, d), seq_map),
        pl.BlockSpec((sps, TILE * ROW_SUBLANES, LANES), tok_map),
        pl.BlockSpec((sps, SUBLANES, TILE), tok_map),
        pl.BlockSpec((sps, SUBLANES, TILE), tok_map),
        pl.BlockSpec((sps, N_EXPERTS, LANES), tok_map),
        pl.BlockSpec(memory_space=pl.ANY),
    )
    scratch = [
        pltpu.VMEM((sps, 3 * N_DN_HEADS, CONV_HALO + TILE, LANES), F32),
        pltpu.VMEM((sps, N_POOL_GROUPS, POOL_HALO + TILE, LANES), F32),
        pltpu.VMEM((sps, N_DN_HEADS, DN_HEAD_DIM, DN_HEAD_DIM), F32),
        pltpu.VMEM((N_EXPERTS, LANES), F32),
        pltpu.VMEM((fill_rows, LANES), F32),
        pltpu.SemaphoreType.DMA(()),
    ]
    return pl.pallas_call(
        functools.partial(_mix_kernel, zero_per_step=zero_per_step),
        grid=(n_b, n_t),
        in_specs=in_specs,
        out_specs=out_specs,
        out_shape=out_shape,
        scratch_shapes=scratch,
        compiler_params=pltpu.CompilerParams(
            dimension_semantics=("arbitrary", "arbitrary"), vmem_limit_bytes=VMEM_LIMIT_BYTES),
        name="mix",
    )(x, anw, win_r, convw, alog_p, dtb_p, dnw, poolw, pscale, wout, fnw, wr_p, br_p)


COPY_SUBLANES = COPY_ROWS * ROW_SUBLANES


COPY_TABLE = 256
assert SLOTS // COPY_ROWS < COPY_TABLE
LOOP_UNROLL = 8


def _for_count(n, fn):
    def main(i, carry):
        for u in range(LOOP_UNROLL):
            fn(i * LOOP_UNROLL + u)
        return carry
    n_main = lax.shift_right_logical(n, LOOP_UNROLL.bit_length() - 1)
    lax.fori_loop(0, n_main, main, 0)

    def tail(i, carry):
        fn(i)
        return carry
    lax.fori_loop(n_main * LOOP_UNROLL, n, tail, 0)


def _run_copies(dst_ref, local_buf, global_ref, sem, to_global):
    n = dst_ref[COPY_TABLE - 1]

    def start(c):
        loc = local_buf.at[pl.ds(pl.multiple_of(c * COPY_SUBLANES, COPY_SUBLANES), COPY_SUBLANES)]
        glo = global_ref.at[pl.ds(pl.multiple_of(dst_ref[c], SUBLANES), COPY_SUBLANES)]
        if to_global:
            pltpu.make_async_copy(loc, glo, sem).start()
        else:
            pltpu.make_async_copy(glo, loc, sem).start()
    _for_count(n, start)
    return n


def _wait_copies(n, local_buf, global_ref, sem):
    def wait(i):
        del i
        pltpu.make_async_copy(local_buf.at[pl.ds(0, COPY_SUBLANES)], global_ref.at[pl.ds(0, COPY_SUBLANES)], sem).wait()
    _for_count(n, wait)


def _dispatch_kernel(slot_ref, dst_ref, tok_tiles, xs_in_ref, xs_ref, sorted_buf, inflight, sem):
    del xs_in_ref
    step = pl.program_id(0)
    buf = sorted_buf.at[step % 2]
    @pl.when(step < 2)
    def _():
        buf[...] = jnp.zeros(buf.shape, F32)

    def token_body(i, carry):
        tile = tok_tiles[pl.ds(pl.multiple_of(i * ROW_SUBLANES, ROW_SUBLANES), ROW_SUBLANES), :]
        for kk in range(TOP_K):
            slot_row = slot_ref[kk * TILE + i]
            buf[pl.ds(pl.multiple_of(slot_row, ROW_SUBLANES), ROW_SUBLANES), :] = tile
        return carry
    lax.fori_loop(0, TILE, token_body, 0, unroll=8)

    @pl.when(step > 0)
    def _():
        _wait_copies(inflight[0], buf, xs_ref, sem)
    inflight[0] = _run_copies(dst_ref, buf, xs_ref, sem, to_global=True)

    @pl.when(step == pl.num_programs(0) - 1)
    def _():
        _wait_copies(inflight[0], buf, xs_ref, sem)


def _smem_spec(n):
    return pl.BlockSpec((n,), lambda i: (i,), memory_space=pltpu.SMEM)


def _dispatch_call(slots, dst, hn_tiles, xs_zero):
    n_tiles = hn_tiles.shape[0] // (TILE * ROW_SUBLANES)
    tile_of = lambda i: (i % SEQ_PER_STEP) * (n_tiles // SEQ_PER_STEP) + i // SEQ_PER_STEP
    return pl.pallas_call(
        _dispatch_kernel,
        grid=(n_tiles,),
        in_specs=[
            pl.BlockSpec((TILE * TOP_K,), lambda i: (tile_of(i),), memory_space=pltpu.SMEM),
            pl.BlockSpec((COPY_TABLE,), lambda i: (tile_of(i),), memory_space=pltpu.SMEM),
            pl.BlockSpec((TILE * ROW_SUBLANES, LANES), lambda i: (tile_of(i), 0)),
            pl.BlockSpec(memory_space=pl.ANY),
        ],
        out_specs=pl.BlockSpec(memory_space=pl.ANY),
        out_shape=jax.ShapeDtypeStruct(xs_zero.shape, F32),
        scratch_shapes=[pltpu.VMEM((2, SLOTS * ROW_SUBLANES, LANES), F32),
                        pltpu.SMEM((1,), jnp.int32),
                        pltpu.SemaphoreType.DMA(())],
        input_output_aliases={3: 0},
        compiler_params=pltpu.CompilerParams(dimension_semantics=("arbitrary",)),
        name="dispatch",
    )(slots, dst, hn_tiles, xs_zero)


def _combine_kernel(slot_ref, gatev_ref, dst_ref, dst_next_ref, x1_ref, fw_ref, ys_ref,
                    out_ref, sorted_buf, tok_tiles, inflight, sems):
    step = pl.program_id(0)
    cur = step % 2
    nxt = 1 - cur
    buf = sorted_buf.at[cur]

    @pl.when(step == 0)
    def _():
        inflight[0] = _run_copies(dst_ref, sorted_buf.at[0], ys_ref, sems.at[0], to_global=False)

    @pl.when(step + 1 < pl.num_programs(0))
    def _():
        inflight[nxt] = _run_copies(dst_next_ref, sorted_buf.at[nxt], ys_ref, sems.at[nxt], to_global=False)
    _wait_copies(inflight[cur], buf, ys_ref, sems.at[cur])

    def token_body(i, carry):
        acc = None
        for kk in range(TOP_K):
            slot_row = slot_ref[kk * TILE + i]
            row = buf[pl.ds(pl.multiple_of(slot_row, ROW_SUBLANES), ROW_SUBLANES), :]
            term = row * gatev_ref[kk * TILE + i]
            acc = term if acc is None else acc + term
        tok_tiles[pl.ds(pl.multiple_of(i * ROW_SUBLANES, ROW_SUBLANES), ROW_SUBLANES), :] = acc
        return carry
    lax.fori_loop(0, TILE, token_body, 0, unroll=8)

    x1 = x1_ref[...]
    cols = []
    ssq = None
    for s in range(ROW_SUBLANES):
        c = x1[:, s * LANES:(s + 1) * LANES] + tok_tiles[pl.ds(s, TILE, stride=ROW_SUBLANES), :]
        cols.append(c)
        sq = jnp.sum(c * c, axis=-1, keepdims=True)
        ssq = sq if ssq is None else ssq + sq
    scale = lax.rsqrt(ssq * (1.0 / D_MODEL) + RMS_EPS)
    for s in range(ROW_SUBLANES):
        out_ref[:, s * LANES:(s + 1) * LANES] = cols[s] * scale * fw_ref[:, s * LANES:(s + 1) * LANES]


def _combine_call(slots, gatev, dst, x1_flat, fw, ys):
    n_tok, d = x1_flat.shape
    n_steps = n_tok // TILE
    next_spec = pl.BlockSpec((COPY_TABLE,), lambda i: (jnp.minimum(i + 1, n_steps - 1),), memory_space=pltpu.SMEM)
    return pl.pallas_call(
        _combine_kernel,
        grid=(n_steps,),
        in_specs=[
            _smem_spec(TILE * TOP_K), _smem_spec(TILE * TOP_K), _smem_spec(COPY_TABLE), next_spec,
            pl.BlockSpec((TILE, d), lambda i: (i, 0)),
            pl.BlockSpec(fw.shape, lambda i: (0, 0)),
            pl.BlockSpec(memory_space=pl.ANY),
        ],
        out_specs=pl.BlockSpec((TILE, d), lambda i: (i, 0)),
        out_shape=jax.ShapeDtypeStruct((n_tok, d), F32),
        scratch_shapes=[pltpu.VMEM((2, SLOTS * ROW_SUBLANES, LANES), F32),
                        pltpu.VMEM((TILE * ROW_SUBLANES, LANES), F32),
                        pltpu.SMEM((2,), jnp.int32),
                        pltpu.SemaphoreType.DMA((2,))],
        compiler_params=pltpu.CompilerParams(dimension_semantics=("arbitrary",)),
        name="combine",
    )(slots, gatev, dst, dst, x1_flat, fw, ys)


BLOCK_SUBLANES = MOE_ROWS * ROW_SUBLANES


def _expert_mlp(x_tiles, y_tiles, wgu16, wd16, bgu_ref, bd_ref):
    xb = jnp.concatenate(
        [x_tiles[pl.ds(s, MOE_ROWS, stride=ROW_SUBLANES), :].astype(BF16) for s in range(ROW_SUBLANES)], axis=1)
    acc = None
    for j in range(D_EXPERT // FF_CHUNK):
        c0 = j * FF_CHUNK
        gate = _dot(xb, wgu16[:, c0:c0 + FF_CHUNK]) + bgu_ref[:, c0:c0 + FF_CHUNK]
        up = _dot(xb, wgu16[:, D_EXPERT + c0:D_EXPERT + c0 + FF_CHUNK]) + bgu_ref[:, D_EXPERT + c0:D_EXPERT + c0 + FF_CHUNK]
        gate = jnp.minimum(gate, SWIGLU_LIMIT)
        up = jnp.clip(up, -SWIGLU_LIMIT, SWIGLU_LIMIT)
        act = (up + 1.0) * gate * jax.nn.sigmoid(SWIGLU_ALPHA * gate)
        term = _dot(act.astype(BF16), wd16[c0:c0 + FF_CHUNK, :])
        acc = term if acc is None else acc + term
    y = acc + bd_ref[...]
    for s in range(ROW_SUBLANES):
        y_tiles[pl.ds(s, MOE_ROWS, stride=ROW_SUBLANES), :] = y[:, s * LANES:(s + 1) * LANES]


def _moe_kernel(be_ref, nu_ref, next_ref, xs_ref, wgu_hbm, bgu_ref, wd_hbm, bd_ref, ys_ref,
                wgu32, wd32, wgu16, wd16, group, wsem):
    i = pl.program_id(0)

    def fetch(e, slot):
        return (pltpu.make_async_copy(wgu_hbm.at[e], wgu32.at[slot], wsem.at[slot, 0]),
                pltpu.make_async_copy(wd_hbm.at[e], wd32.at[slot], wsem.at[slot, 1]))

    @pl.when(i < nu_ref[0])
    def _():
        e = be_ref[i]

        @pl.when(i == 0)
        def _():
            group[0] = 0
            for cp in fetch(e, 0):
                cp.start()

        @pl.when(jnp.logical_and(i > 0, e != be_ref[jnp.maximum(i - 1, 0)]))
        def _():
            group[0] = group[0] + 1

        @pl.when(jnp.logical_or(i == 0, e != be_ref[jnp.maximum(i - 1, 0)]))
        def _():
            slot = group[0] % 2
            for cp in fetch(e, slot):
                cp.wait()
            nxt = next_ref[e]

            @pl.when(nxt >= 0)
            def _():
                for cp in fetch(nxt, 1 - slot):
                    cp.start()
            wgu16[...] = wgu32[slot].astype(BF16)
            wd16[...] = wd32[slot].astype(BF16)
        _expert_mlp(xs_ref, ys_ref, wgu16, wd16, bgu_ref, bd_ref)

    @pl.when(i >= nu_ref[0])
    def _():
        ys_ref[...] = jnp.zeros(ys_ref.shape, F32)


def _moe_call(block_e, n_used, next_expert, xs, wgu, bgu, wd, bd):
    d = D_MODEL
    n_blocks = xs.shape[0] // BLOCK_SUBLANES
    blk = lambda i, be, nu, nx: (jnp.minimum(i, nu[0] - 1), 0)
    exp3 = lambda i, be, nu, nx: (be[jnp.minimum(i, nu[0] - 1)], 0, 0)
    grid_spec = pltpu.PrefetchScalarGridSpec(
        num_scalar_prefetch=3,
        grid=(n_blocks,),
        in_specs=[
            pl.BlockSpec((BLOCK_SUBLANES, LANES), blk),
            pl.BlockSpec(memory_space=pl.ANY),
            pl.BlockSpec((None, 1, 2 * D_EXPERT), exp3),
            pl.BlockSpec(memory_space=pl.ANY),
            pl.BlockSpec((None, 1, d), exp3),
        ],
        out_specs=pl.BlockSpec((BLOCK_SUBLANES, LANES), lambda i, be, nu, nx: (i, 0)),
        scratch_shapes=[pltpu.VMEM((2, d, 2 * D_EXPERT), F32), pltpu.VMEM((2, D_EXPERT, d), F32),
                        pltpu.VMEM((d, 2 * D_EXPERT), BF16), pltpu.VMEM((D_EXPERT, d), BF16),
                        pltpu.SMEM((1,), jnp.int32), pltpu.SemaphoreType.DMA((2, 2))],
    )
    return pl.pallas_call(
        _moe_kernel,
        grid_spec=grid_spec,
        out_shape=jax.ShapeDtypeStruct(xs.shape, F32),
        compiler_params=pltpu.CompilerParams(
            dimension_semantics=("arbitrary",), vmem_limit_bytes=VMEM_LIMIT_BYTES),
        name="moe",
    )(block_e, n_used, next_expert, xs, wgu, bgu, wd, bd)


def _pad_lanes(v, lane0):
    out = jnp.zeros((1, LANES), F32)
    return lax.dynamic_update_slice(out, v.reshape(1, -1).astype(F32), (0, lane0))


def kernel(x, attn_norm_w, w_in, conv_w, a_log, dt_bias, dn_norm_w, pool_w, pool_scale, w_out, ffn_norm_w,
           w_router, b_router, w_gate_up, b_gate_up, w_down, b_down, final_norm_w):
    bsz, seq, d = x.shape
    n_tok = bsz * seq
    assert w_in.shape[0] == 1, "the combine kernel fuses the final norm, so exactly one layer is supported"
    assert d == D_MODEL and seq % TILE == 0 and bsz % SEQ_PER_STEP == 0
    l = 0

    s_z, s_a = 4 * D_DN, 4 * D_DN + 2 * N_DN_HEADS
    ba_cols = jnp.pad(w_in[l][:, s_z:s_a], ((0, 0), (0, LANES - 2 * N_DN_HEADS)))
    win_r = jnp.concatenate([w_in[l][:, :s_z], w_in[l][:, s_a:], ba_cols], axis=1).astype(BF16)
    wr32 = jnp.pad(w_router[l].astype(F32), ((0, 0), (0, LANES - N_EXPERTS)))
    wr_hi = lax.bitcast_convert_type(
        lax.bitcast_convert_type(wr32, jnp.uint32) & jnp.uint32(0xFFFF0000), F32)
    wr_p = jnp.concatenate([wr_hi.astype(BF16), (wr32 - wr_hi).astype(BF16)], axis=1)
    br_p = _pad_lanes(b_router[l], 0)
    fnw = ffn_norm_w[l].reshape(1, d)

    n_blocks = (n_tok * TOP_K + N_EXPERTS * (MOE_ROWS + COPY_ROWS - 2)) // MOE_ROWS + 1
    n_rows = n_blocks * MOE_ROWS

    x1, hn_tiles, route, gates, meta, xs_zero = _mix_call(
        x, attn_norm_w[l].reshape(1, d), win_r, conv_w[l], _pad_lanes(a_log[l], A_LANE0),
        _pad_lanes(dt_bias[l], A_LANE0), dn_norm_w[l].reshape(1, DN_HEAD_DIM), pool_w[l].astype(BF16),
        pool_scale[l].reshape(1, D_POOL), w_out[l].astype(BF16), fnw, wr_p, br_p, n_rows * ROW_SUBLANES)

    n_tiles = n_tok // TILE
    meta = meta.reshape(n_tiles, N_EXPERTS, LANES)[:, :, :2].astype(jnp.int32)
    tile_cnt, tile_before = meta[:, :, 0], meta[:, :, 1]
    total = jnp.sum(tile_cnt, axis=0)
    padded = jnp.where(total > 0, ((total + COPY_ROWS - 1 + MOE_ROWS - 1) // MOE_ROWS) * MOE_ROWS, 0)
    pad_end = jnp.cumsum(padded)
    pad_start = pad_end - padded
    run_base = pad_start[None, :] + tile_before
    ncopy = (tile_cnt + COPY_ROWS - 1) // COPY_ROWS
    copy_end = jnp.cumsum(ncopy, axis=1)
    copy_start = copy_end - ncopy
    c_idx = jnp.arange(COPY_TABLE, dtype=jnp.int32)[None, :, None]
    in_run = jnp.logical_and(c_idx >= copy_start[:, None, :], c_idx < copy_end[:, None, :])
    dst_row = jnp.sum(jnp.where(in_run, run_base[:, None, :] + COPY_ROWS * (c_idx - copy_start[:, None, :]), 0), axis=2)
    dst = (dst_row * ROW_SUBLANES).at[:, COPY_TABLE - 1].set(copy_end[:, -1]).reshape(-1)
    block_start = jnp.arange(n_blocks, dtype=jnp.int32) * MOE_ROWS
    block_e = jnp.minimum(jnp.sum(block_start[:, None] >= pad_end[None, :], axis=1), N_EXPERTS - 1).astype(jnp.int32)
    n_used = (pad_end[-1:] // MOE_ROWS).astype(jnp.int32)

    def per_tile_table(a):
        return a.reshape(n_tiles, SUBLANES, TILE)[:, :TOP_K].reshape(-1)
    slots = per_tile_table(route)
    gatev = per_tile_table(gates)

    x1_flat = x1.reshape(n_tok, d)
    xs = _dispatch_call(slots, dst, hn_tiles.reshape(n_tok * ROW_SUBLANES, LANES), xs_zero)
    eidx = jnp.arange(N_EXPERTS, dtype=jnp.int32)
    later_owner = jnp.where(jnp.logical_and(padded[None, :] > 0, eidx[None, :] > eidx[:, None]), eidx[None, :], N_EXPERTS)
    next_expert = jnp.min(later_owner, axis=1)
    next_expert = jnp.where(next_expert < N_EXPERTS, next_expert, -1).astype(jnp.int32)
    ys = _moe_call(block_e, n_used, next_expert, xs, w_gate_up[l], b_gate_up[l].reshape(N_EXPERTS, 1, -1),
                   w_down[l], b_down[l].reshape(N_EXPERTS, 1, -1))
    out = _combine_call(slots, gatev, dst, x1_flat, final_norm_w.reshape(1, d), ys)
    return out.reshape(bsz, seq, d)
```

```python
import functools

import jax
import jax.numpy as jnp
from jax import lax
from jax.experimental import pallas as pl
from jax.experimental.pallas import tpu as pltpu

F32 = jnp.float32
BF16 = jnp.bfloat16

D_MODEL = 1024
N_DN_HEADS = 4
DN_HEAD_DIM = 128
D_DN = N_DN_HEADS * DN_HEAD_DIM
CONV_K = 4
CHUNK = 64
POOL_WINDOWS = (2, 4, 8, 16)
N_POOL_GROUPS = len(POOL_WINDOWS)
D_POOL = D_MODEL - D_DN
POOL_GROUP_DIM = D_POOL // N_POOL_GROUPS
N_EXPERTS = 32
TOP_K = 4
D_EXPERT = D_MODEL
SWIGLU_LIMIT = 7.0
SWIGLU_ALPHA = 1.702
RMS_EPS = 1e-6
L2_EPS = 1e-6

LANES = 128
SUBLANES = 8
ROW_SUBLANES = D_MODEL // LANES
VMEM_LIMIT_BYTES = 56 * 1024 * 1024

TILE = 256
PAIR = 2 * CHUNK
CONV_HALO = SUBLANES
POOL_HALO = 2 * SUBLANES
MOE_ROWS = 608
FF_CHUNK = 512
COPY_ROWS = 8
SEQ_PER_STEP = 2
assert MOE_ROWS % 16 == 0
SLOTS = TILE * TOP_K + N_EXPERTS * COPY_ROWS
assert TILE * TOP_K // COPY_ROWS + 1 <= 256

COL_QKV = 0
COL_Z = 3 * D_DN
COL_P = 4 * D_DN
COL_BA = 4 * D_DN + D_POOL
D_IN_R = COL_BA + LANES
A_LANE0 = N_DN_HEADS


def _dot(a, b):
    return jnp.dot(a, b, preferred_element_type=F32)


def _hi_lo(x):
    hi = x.astype(BF16)
    lo = (x - hi.astype(F32)).astype(BF16)
    return hi, lo


def _dot_exact_lhs(m_bf16, x):
    hi = x.astype(BF16)
    r = x - hi.astype(F32)
    mid = r.astype(BF16)
    lo = (r - mid.astype(F32)).astype(BF16)
    return _dot(m_bf16, hi) + (_dot(m_bf16, mid) + _dot(m_bf16, lo))


def _rms_scale(x):
    return lax.rsqrt(jnp.mean(x * x, axis=-1, keepdims=True) + RMS_EPS)


def _silu(x):
    return x * jax.nn.sigmoid(x)


def _softplus(x):
    return jnp.maximum(x, 0.0) + jnp.log1p(jnp.exp(-jnp.abs(x)))


def _mix_tile(t, x_ref, anw_ref, win_ref, convw_ref, alog_ref, dtb_ref, dnw_ref, poolw_ref, pscale_ref,
              wout_ref, fnw_ref, wr_ref, br_ref,
              x1_ref, hnt_ref, route_ref, gate_ref, meta_ref,
              qkv_ext, p_ext, state, cnt):
    tt = TILE

    x = x_ref[...]
    hb = (x * _rms_scale(x) * anw_ref[...]).astype(BF16)
    qkv_pre = _dot(hb, win_ref[:, COL_QKV:COL_Z])
    for c in range(3 * N_DN_HEADS):
        qkv_ext[c, CONV_HALO:CONV_HALO + tt, :] = qkv_pre[:, c * LANES:(c + 1) * LANES]
    yield
    z = _dot(hb, win_ref[:, COL_Z:COL_P])
    p_pre = _dot(hb, win_ref[:, COL_P:COL_BA])
    for c in range(N_POOL_GROUPS):
        p_ext[c, POOL_HALO:POOL_HALO + tt, :] = p_pre[:, c * LANES:(c + 1) * LANES]
    ba = _dot(hb, win_ref[:, COL_BA:D_IN_R])

    yield
    beta = jax.nn.sigmoid(ba)
    g = -jnp.exp(alog_ref[...]) * _softplus(ba + dtb_ref[...])
    ri = lax.broadcasted_iota(jnp.int32, (tt, tt), 0)
    ci = lax.broadcasted_iota(jnp.int32, (tt, tt), 1)
    same_chunk = (ri // CHUNK) == (ci // CHUNK)
    l_incl = jnp.where(jnp.logical_and(same_chunk, ri >= ci), 1.0, 0.0).astype(BF16)
    gc = _dot_exact_lhs(l_incl, g)
    gl = jnp.concatenate(
        [jnp.broadcast_to(gc[c * CHUNK + CHUNK - 1:c * CHUNK + CHUNK, :], (CHUNK, LANES)) for c in range(tt // CHUNK)],
        axis=0)
    eg = jnp.exp(gc)
    ekd = jnp.exp(gl - gc)
    egl = jnp.exp(gl)
    gct = gc.T

    pr = lax.broadcasted_iota(jnp.int32, (PAIR, PAIR), 0)
    pc = lax.broadcasted_iota(jnp.int32, (PAIR, PAIR), 1)
    pair_same = (pr // CHUNK) == (pc // CHUNK)
    causal = jnp.logical_and(pair_same, pr >= pc)
    strict = jnp.logical_and(pair_same, pr > pc)
    eye = jnp.where(pr == pc, 1.0, 0.0).astype(F32)
    zeros_chunk = jnp.zeros((CHUNK, DN_HEAD_DIM), F32)

    def conv_silu(col):
        acc = None
        for j in range(CONV_K):
            rows = pl.ds(CONV_HALO - (CONV_K - 1) + j, tt)
            term = qkv_ext[col, rows, :] * convw_ref[j:j + 1, col * LANES:(col + 1) * LANES]
            acc = term if acc is None else acc + term
        return _silu(acc)

    def l2n(v):
        return v * lax.rsqrt(jnp.sum(v * v, axis=-1, keepdims=True) + L2_EPS)

    def block_diag(b0, b1):
        zero = jnp.zeros_like(b0)
        return jnp.concatenate([jnp.concatenate([b0, zero], axis=1), jnp.concatenate([zero, b1], axis=1)], axis=0)

    def dot_pair(a0, a1, b0, b1):
        out = _dot(jnp.concatenate([a0, a1], axis=1), block_diag(b0, b1))
        return out[:, :LANES], out[:, LANES:]

    n_pairs = tt // PAIR
    assert n_pairs == 2 and N_DN_HEADS % 2 == 0
    insts = []
    for h in range(N_DN_HEADS):
        yield
        q_all = l2n(conv_silu(h)) * (DN_HEAD_DIM ** -0.5)
        k_all = l2n(conv_silu(N_DN_HEADS + h))
        v_all = conv_silu(2 * N_DN_HEADS + h)
        yield
        la = A_LANE0 + h
        pre = []
        for d in range(n_pairs):
            r0 = d * PAIR
            q = q_all[r0:r0 + PAIR]
            k = k_all[r0:r0 + PAIR]
            v = v_all[r0:r0 + PAIR]
            bcol = beta[r0:r0 + PAIR, h:h + 1]
            gcol = gc[r0:r0 + PAIR, la:la + 1]
            egcol = eg[r0:r0 + PAIR, la:la + 1]
            ekdcol = ekd[r0:r0 + PAIR, la:la + 1]
            grow = gct[la:la + 1, r0:r0 + PAIR]
            diff = gcol - grow
            decay = jnp.where(causal, jnp.exp(jnp.where(causal, diff, 0.0)), 0.0)
            kb = k * bcol
            pre.append(dict(
                decay=decay, kq=jnp.concatenate([kb, q], axis=0).astype(BF16), kt=k.T.astype(BF16),
                rhs=jnp.concatenate([(v * bcol).astype(BF16), (kb * egcol).astype(BF16)], axis=1),
                qd=(q * egcol).astype(BF16), kdt=(k * ekdcol).T.astype(BF16)))
        kk0, kk1 = dot_pair(pre[0]["kq"], pre[1]["kq"], pre[0]["kt"], pre[1]["kt"])
        for d, kk in enumerate((kk0, kk1)):
            it = pre[d]
            it["a"] = jnp.where(strict, kk[:PAIR] * it["decay"], 0.0)
            it["qk"] = jnp.where(causal, kk[PAIR:] * it["decay"], 0.0).astype(BF16)
            insts.append(it)

    rs = [-it["a"] for it in insts]
    p16 = [it["a"].astype(BF16) for it in insts]
    for _ in range(CHUNK.bit_length() - 2):
        yield
        pws = []
        for i in range(0, len(insts), 2):
            pws.extend(dot_pair(p16[i], p16[i + 1], p16[i], p16[i + 1]))
        p16 = [pw.astype(BF16) for pw in pws]
        yield
        rps = []
        for i in range(0, len(insts), 2):
            rps.extend(dot_pair(rs[i].astype(BF16), rs[i + 1].astype(BF16), p16[i], p16[i + 1]))
        rs = [r + pw + rp for r, pw, rp in zip(rs, pws, rps)]
    yield
    for it, r in zip(insts, rs):
        uw = _dot((eye + r).astype(BF16), it["rhs"])
        it["u"] = uw[:, :DN_HEAD_DIM]
        it["wq"] = [jnp.concatenate([uw[c0:c0 + CHUNK, DN_HEAD_DIM:].astype(BF16), it["qd"][c0:c0 + CHUNK]], axis=0)
                    for c0 in range(0, PAIR, CHUNK)]
        it["qkk"] = [jnp.concatenate([it["qk"][c0:c0 + CHUNK], it["kdt"]], axis=0)
                     for c0 in range(0, PAIR, CHUNK)]

    s_cur = [state[h] for h in range(N_DN_HEADS)]
    o_rows = [[None] * (tt // CHUNK) for _ in range(N_DN_HEADS)]
    for d in range(n_pairs):
        for c in range(PAIR // CHUNK):
            yield
            c0 = c * CHUNK
            row = d * PAIR + c0
            for h0 in range(0, N_DN_HEADS, 2):
                its = [insts[h * n_pairs + d] for h in (h0, h0 + 1)]
                ss = [s_cur[h0], s_cur[h0 + 1]]
                ws = dot_pair(its[0]["wq"][c], its[1]["wq"][c], ss[0].astype(BF16), ss[1].astype(BF16))
                v_pads = []
                for it, wsq in zip(its, ws):
                    v_new = it["u"][c0:c0 + CHUNK] - wsq[:CHUNK]
                    parts = [zeros_chunk] * (PAIR // CHUNK)
                    parts[c] = v_new
                    v_pads.append(jnp.concatenate(parts, axis=0).astype(BF16))
                ov = dot_pair(its[0]["qkk"][c], its[1]["qkk"][c], v_pads[0], v_pads[1])
                for j, h in enumerate((h0, h0 + 1)):
                    o_rows[h][d * (PAIR // CHUNK) + c] = ws[j][CHUNK:] + ov[j][:CHUNK]
                    gt = egl[row:row + 1, A_LANE0 + h:A_LANE0 + h + 1]
                    s_cur[h] = ss[j] * gt + ov[j][CHUNK:]
    for h in range(N_DN_HEADS):
        state[h] = s_cur[h]

    yield
    dn_parts = []
    for h in range(N_DN_HEADS):
        o = jnp.concatenate(o_rows[h], axis=0)
        zh = z[:, h * LANES:(h + 1) * LANES]
        dn_parts.append(((o * _rms_scale(o) * dnw_ref[...]) * _silu(zh)).astype(BF16))

    pos = (t * tt + lax.broadcasted_iota(jnp.int32, (tt, 1), 0) + 1).astype(F32)
    pool_parts = []
    for gi, win in enumerate(POOL_WINDOWS):
        yield
        cols = slice(gi * POOL_GROUP_DIM, (gi + 1) * POOL_GROUP_DIM)
        cur = p_ext[gi, POOL_HALO:POOL_HALO + tt, :]
        acc = cur
        for sft in range(1, win):
            acc = acc + p_ext[gi, pl.ds(POOL_HALO - sft, tt), :]
        mixed = acc / jnp.minimum(pos, float(win)) - cur
        y = _dot(mixed.astype(BF16), poolw_ref[gi]) * pscale_ref[:, cols]
        pool_parts.append(y.astype(BF16))

    yield
    x1 = x +_dot(jnp.concatenate(dn_parts + pool_parts, axis=1), wout_ref[...])
    x1_ref[...] = x1

    qkv_ext[:, 0:CONV_HALO, :] = qkv_ext[:, tt:tt + CONV_HALO, :]
    p_ext[:, 0:POOL_HALO, :] = p_ext[:, tt:tt + POOL_HALO, :]

    hn = x1 * _rms_scale(x1) * fnw_ref[...]
    for s in range(ROW_SUBLANES):
        hnt_ref[pl.ds(s, tt, stride=ROW_SUBLANES), :] = hn[:, s * LANES:(s + 1) * LANES]
    hn_hi, hn_lo = _hi_lo(hn)
    hh = _dot(hn_hi, wr_ref[...])
    logits = hh[:, :LANES] + (hh[:, LANES:] + _dot(hn_lo, wr_ref[:, :LANES])) + br_ref[...]
    yield
    lt = logits.T[:N_EXPERTS]
    eidx = lax.broadcasted_iota(jnp.int32, (N_EXPERTS, tt), 0).astype(F32)
    work = lt
    vals, idxs = [], []
    for _ in range(TOP_K):
        m = jnp.max(work, axis=0, keepdims=True)
        idx = jnp.min(jnp.where(work == m, eidx, float(LANES)), axis=0, keepdims=True)
        vals.append(m)
        idxs.append(idx)
        work = jnp.where(eidx == idx, -jnp.inf, work)
    exps = [jnp.exp(v - vals[0]) for v in vals]
    denom = exps[0] + exps[1] + exps[2] + exps[3]

    yield
    onehot = jnp.zeros((N_EXPERTS, tt), F32)
    for idx in idxs:
        onehot = onehot + jnp.where(eidx == idx, 1.0, 0.0)
    earlier = jnp.where(ri < ci, 1.0, 0.0).astype(BF16)
    within = _dot(onehot.astype(BF16), earlier)
    tile_cnt = jnp.sum(onehot, axis=1, keepdims=True)
    n_copies = jnp.floor((tile_cnt + float(COPY_ROWS - 1)) * (1.0 / COPY_ROWS))
    er = lax.broadcasted_iota(jnp.int32, (N_EXPERTS, N_EXPERTS), 0)
    ec = lax.broadcasted_iota(jnp.int32, (N_EXPERTS, N_EXPERTS), 1)
    lower_expert = jnp.where(ec < er, 1.0, 0.0).astype(BF16)
    run_start = float(COPY_ROWS) * _dot(
        lower_expert, jnp.broadcast_to(n_copies, (N_EXPERTS, LANES)).astype(BF16))[:, 0:1]
    slot_of = run_start + within
    srow = lax.broadcasted_iota(jnp.int32, (SUBLANES, tt), 0)
    route = jnp.zeros((SUBLANES, tt), F32)
    gates = jnp.zeros((SUBLANES, tt), F32)
    for kk in range(TOP_K):
        slot = jnp.sum(jnp.where(eidx == idxs[kk], slot_of, 0.0), axis=0, keepdims=True)
        route = jnp.where(srow == kk, slot * float(ROW_SUBLANES), route)
        gates = jnp.where(srow == kk, exps[kk] / denom, gates)
    route_ref[...] = route.astype(jnp.int32)
    gate_ref[...] = gates
    mlane = lax.broadcasted_iota(jnp.int32, (N_EXPERTS, LANES), 1)
    meta_ref[...] = jnp.where(mlane == 0, tile_cnt, jnp.where(mlane == 1, cnt[...], 0.0))
    cnt[...] = cnt[...] + tile_cnt


def _mix_kernel(x_ref, anw_ref, win_ref, convw_ref, alog_ref, dtb_ref, dnw_ref, poolw_ref, pscale_ref,
                wout_ref, fnw_ref, wr_ref, br_ref,
                x1_ref, hnt_ref, route_ref, gate_ref, meta_ref, xs_ref,
                qkv_ext, p_ext, state, cnt, zero_buf, zero_sem, *, zero_per_step):
    b = pl.program_id(0)
    t = pl.program_id(1)
    first_step = jnp.logical_and(b == 0, t == 0)

    @pl.when(first_step)
    def _():
        zero_buf[...] = jnp.zeros(zero_buf.shape, F32)
        cnt[...] = jnp.zeros(cnt.shape, F32)
    fill_rows = zero_buf.shape[0]
    n_fill = xs_ref.shape[0] // fill_rows
    step = b * pl.num_programs(1) + t

    def zero_copy(j, go):
        idx = step * zero_per_step + j

        @pl.when(idx < n_fill)
        def _():
            go(pltpu.make_async_copy(
                zero_buf, xs_ref.at[pl.ds(pl.multiple_of(idx * fill_rows, SUBLANES), fill_rows)], zero_sem))
    for j in range(zero_per_step):
        zero_copy(j, lambda cp: cp.start())

    @pl.when(t == 0)
    def _():
        qkv_ext[:, :, 0:CONV_HALO, :] = jnp.zeros((SEQ_PER_STEP, 3 * N_DN_HEADS, CONV_HALO, LANES), F32)
        p_ext[:, :, 0:POOL_HALO, :] = jnp.zeros((SEQ_PER_STEP, N_POOL_GROUPS, POOL_HALO, LANES), F32)
        state[...] = jnp.zeros(state.shape, F32)

    tiles = [
        _mix_tile(t, x_ref.at[sq], anw_ref, win_ref, convw_ref, alog_ref, dtb_ref, dnw_ref, poolw_ref, pscale_ref,
                  wout_ref, fnw_ref, wr_ref, br_ref,
                  x1_ref.at[sq], hnt_ref.at[sq], route_ref.at[sq], gate_ref.at[sq], meta_ref.at[sq],
                  qkv_ext.at[sq], p_ext.at[sq], state.at[sq], cnt)
        for sq in range(SEQ_PER_STEP)]
    live = [True] * SEQ_PER_STEP
    while any(live):
        for sq, gen in enumerate(tiles):
            if live[sq]:
                live[sq] = next(gen, "done") != "done"
    for j in range(zero_per_step):
        zero_copy(j, lambda cp: cp.wait())


def _mix_call(x, anw, win_r, convw, alog_p, dtb_p, dnw, poolw, pscale, wout, fnw, wr_p, br_p, xs_sublanes):
    bsz, seq, d = x.shape
    n_t = seq // TILE
    sps = SEQ_PER_STEP
    n_b = bsz // sps
    x = x.reshape(sps, n_b, seq, d)
    fill_rows = BLOCK_SUBLANES
    assert xs_sublanes % fill_rows == 0
    zero_per_step = -(-(xs_sublanes // fill_rows) // (n_b * n_t))
    const2 = lambda b, t: (0, 0)
    tok_map = lambda b, t: (0, b * n_t + t, 0)
    seq_map = lambda b, t: (0, b, t, 0)
    in_specs = [
        pl.BlockSpec((sps, None, TILE, d), seq_map),
        pl.BlockSpec(anw.shape, const2),
        pl.BlockSpec(win_r.shape, const2),
        pl.BlockSpec(convw.shape, const2),
        pl.BlockSpec(alog_p.shape, const2),
        pl.BlockSpec(dtb_p.shape, const2),
        pl.BlockSpec(dnw.shape, const2),
        pl.BlockSpec(poolw.shape, lambda b, t: (0, 0, 0)),
        pl.BlockSpec(pscale.shape, const2),
        pl.BlockSpec(wout.shape, const2),
        pl.BlockSpec(fnw.shape, const2),
        pl.BlockSpec(wr_p.shape, const2),
        pl.BlockSpec(br_p.shape, const2),
    ]
    out_shape = (
        jax.ShapeDtypeStruct((sps, n_b, seq, d), F32),
        jax.ShapeDtypeStruct((sps, n_b * seq * ROW_SUBLANES, LANES), F32),
        jax.ShapeDtypeStruct((sps, n_b * n_t * SUBLANES, TILE), jnp.int32),
        jax.ShapeDtypeStruct((sps, n_b * n_t * SUBLANES, TILE), F32),
        jax.ShapeDtypeStruct((sps, n_b * n_t * N_EXPERTS, LANES), F32),
        jax.ShapeDtypeStruct((xs_sublanes, LANES), F32),
    )
    out_specs = (
        pl.BlockSpec((sps, None, TILE, d), seq_map),
        pl.BlockSpec((sps, TILE * ROW_SUBLANES, LANES), tok_map),
        pl.BlockSpec((sps, SUBLANES, TILE), tok_map),
        pl.BlockSpec((sps, SUBLANES, TILE), tok_map),
        pl.BlockSpec((sps, N_EXPERTS, LANES), tok_map),
        pl.BlockSpec(memory_space=pl.ANY),
    )
    scratch = [
        pltpu.VMEM((sps, 3 * N_DN_HEADS, CONV_HALO + TILE, LANES), F32),
        pltpu.VMEM((sps, N_POOL_GROUPS, POOL_HALO + TILE, LANES), F32),
        pltpu.VMEM((sps, N_DN_HEADS, DN_HEAD_DIM, DN_HEAD_DIM), F32),
        pltpu.VMEM((N_EXPERTS, LANES), F32),
        pltpu.VMEM((fill_rows, LANES), F32),
        pltpu.SemaphoreType.DMA(()),
    ]
    return pl.pallas_call(
        functools.partial(_mix_kernel, zero_per_step=zero_per_step),
        grid=(n_b, n_t),
        in_specs=in_specs,
        out_specs=out_specs,
        out_shape=out_shape,
        scratch_shapes=scratch,
        compiler_params=pltpu.CompilerParams(
            dimension_semantics=("arbitrary", "arbitrary"), vmem_limit_bytes=VMEM_LIMIT_BYTES),
        name="mix",
    )(x, anw, win_r, convw, alog_p, dtb_p, dnw, poolw, pscale, wout, fnw, wr_p, br_p)


COPY_SUBLANES = COPY_ROWS * ROW_SUBLANES


COPY_TABLE = 256
assert SLOTS // COPY_ROWS < COPY_TABLE
LOOP_UNROLL = 8


def _for_count(n, fn):
    def main(i, carry):
        for u in range(LOOP_UNROLL):
            fn(i * LOOP_UNROLL + u, u)
        return carry
    n_main = lax.shift_right_logical(n, LOOP_UNROLL.bit_length() - 1)
    lax.fori_loop(0, n_main, main, 0)

    def tail(i, carry):
        fn(i, 0)
        return carry
    lax.fori_loop(n_main * LOOP_UNROLL, n, tail, 0)


def _run_copies(dst_ref, local_buf, global_ref, sem, to_global):
    n = dst_ref[COPY_TABLE - 1]

    def start(c, lane):
        loc = local_buf.at[pl.ds(pl.multiple_of(c * COPY_SUBLANES, COPY_SUBLANES), COPY_SUBLANES)]
        glo = global_ref.at[pl.ds(pl.multiple_of(dst_ref[c], SUBLANES), COPY_SUBLANES)]
        if to_global:
            pltpu.make_async_copy(loc, glo, sem).start(priority=lane % 2)
        else:
            pltpu.make_async_copy(glo, loc, sem).start(priority=lane % 2)
    _for_count(n, start)
    return n


def _wait_copies(n, local_buf, global_ref, sem):
    def wait(i, lane):
        del i, lane
        pltpu.make_async_copy(local_buf.at[pl.ds(0, COPY_SUBLANES)], global_ref.at[pl.ds(0, COPY_SUBLANES)], sem).wait()
    _for_count(n, wait)


def _dispatch_kernel(slot_ref, dst_ref, tok_tiles, xs_in_ref, xs_ref, sorted_buf, inflight, sem):
    del xs_in_ref
    step = pl.program_id(0)
    buf = sorted_buf.at[step % 2]
    @pl.when(step < 2)
    def _():
        buf[...] = jnp.zeros(buf.shape, F32)

    def token_body(i, carry):
        tile = tok_tiles[pl.ds(pl.multiple_of(i * ROW_SUBLANES, ROW_SUBLANES), ROW_SUBLANES), :]
        for kk in range(TOP_K):
            slot_row = slot_ref[kk * TILE + i]
            buf[pl.ds(pl.multiple_of(slot_row, ROW_SUBLANES), ROW_SUBLANES), :] = tile
        return carry
    lax.fori_loop(0, TILE, token_body, 0, unroll=8)

    @pl.when(step > 0)
    def _():
        _wait_copies(inflight[0], buf, xs_ref, sem)
    inflight[0] = _run_copies(dst_ref, buf, xs_ref, sem, to_global=True)

    @pl.when(step == pl.num_programs(0) - 1)
    def _():
        _wait_copies(inflight[0], buf, xs_ref, sem)


def _smem_spec(n):
    return pl.BlockSpec((n,), lambda i: (i,), memory_space=pltpu.SMEM)


def _dispatch_call(slots, dst, hn_tiles, xs_zero):
    n_tiles = hn_tiles.shape[0] // (TILE * ROW_SUBLANES)
    tile_of = lambda i: (i % SEQ_PER_STEP) * (n_tiles // SEQ_PER_STEP) + i // SEQ_PER_STEP
    return pl.pallas_call(
        _dispatch_kernel,
        grid=(n_tiles,),
        in_specs=[
            pl.BlockSpec((TILE * TOP_K,), lambda i: (tile_of(i),), memory_space=pltpu.SMEM),
            pl.BlockSpec((COPY_TABLE,), lambda i: (tile_of(i),), memory_space=pltpu.SMEM),
            pl.BlockSpec((TILE * ROW_SUBLANES, LANES), lambda i: (tile_of(i), 0)),
            pl.BlockSpec(memory_space=pl.ANY),
        ],
        out_specs=pl.BlockSpec(memory_space=pl.ANY),
        out_shape=jax.ShapeDtypeStruct(xs_zero.shape, F32),
        scratch_shapes=[pltpu.VMEM((2, SLOTS * ROW_SUBLANES, LANES), F32),
                        pltpu.SMEM((1,), jnp.int32),
                        pltpu.SemaphoreType.DMA(())],
        input_output_aliases={3: 0},
        compiler_params=pltpu.CompilerParams(dimension_semantics=("arbitrary",)),
        name="dispatch",
    )(slots, dst, hn_tiles, xs_zero)


def _combine_kernel(slot_ref, gatev_ref, dst_ref, dst_next_ref, x1_ref, fw_ref, ys_ref,
                    out_ref, sorted_buf, tok_tiles, inflight, sems):
    step = pl.program_id(0)
    cur = step % 2
    nxt = 1 - cur
    buf = sorted_buf.at[cur]

    @pl.when(step == 0)
    def _():
        inflight[0] = _run_copies(dst_ref, sorted_buf.at[0], ys_ref, sems.at[0], to_global=False)

    @pl.when(step + 1 < pl.num_programs(0))
    def _():
        inflight[nxt] = _run_copies(dst_next_ref, sorted_buf.at[nxt], ys_ref, sems.at[nxt], to_global=False)
    _wait_copies(inflight[cur], buf, ys_ref, sems.at[cur])

    def token_body(i, carry):
        acc = None
        for kk in range(TOP_K):
            slot_row = slot_ref[kk * TILE + i]
            row = buf[pl.ds(pl.multiple_of(slot_row, ROW_SUBLANES), ROW_SUBLANES), :]
            term = row * gatev_ref[kk * TILE + i]
            acc = term if acc is None else acc + term
        tok_tiles[pl.ds(pl.multiple_of(i * ROW_SUBLANES, ROW_SUBLANES), ROW_SUBLANES), :] = acc
        return carry
    lax.fori_loop(0, TILE, token_body, 0, unroll=8)

    x1 = x1_ref[...]
    cols = []
    ssq = None
    for s in range(ROW_SUBLANES):
        c = x1[:, s * LANES:(s + 1) * LANES] + tok_tiles[pl.ds(s, TILE, stride=ROW_SUBLANES), :]
        cols.append(c)
        sq = jnp.sum(c * c, axis=-1, keepdims=True)
        ssq = sq if ssq is None else ssq + sq
    scale = lax.rsqrt(ssq * (1.0 / D_MODEL) + RMS_EPS)
    for s in range(ROW_SUBLANES):
        out_ref[:, s * LANES:(s + 1) * LANES] = cols[s] * scale * fw_ref[:, s * LANES:(s + 1) * LANES]


def _combine_call(slots, gatev, dst, x1_flat, fw, ys):
    n_tok, d = x1_flat.shape
    n_steps = n_tok // TILE
    next_spec = pl.BlockSpec((COPY_TABLE,), lambda i: (jnp.minimum(i + 1, n_steps - 1),), memory_space=pltpu.SMEM)
    return pl.pallas_call(
        _combine_kernel,
        grid=(n_steps,),
        in_specs=[
            _smem_spec(TILE * TOP_K), _smem_spec(TILE * TOP_K), _smem_spec(COPY_TABLE), next_spec,
            pl.BlockSpec((TILE, d), lambda i: (i, 0)),
            pl.BlockSpec(fw.shape, lambda i: (0, 0)),
            pl.BlockSpec(memory_space=pl.ANY),
        ],
        out_specs=pl.BlockSpec((TILE, d), lambda i: (i, 0)),
        out_shape=jax.ShapeDtypeStruct((n_tok, d), F32),
        scratch_shapes=[pltpu.VMEM((2, SLOTS * ROW_SUBLANES, LANES), F32),
                        pltpu.VMEM((TILE * ROW_SUBLANES, LANES), F32),
                        pltpu.SMEM((2,), jnp.int32),
                        pltpu.SemaphoreType.DMA((2,))],
        compiler_params=pltpu.CompilerParams(dimension_semantics=("arbitrary",)),
        name="combine",
    )(slots, gatev, dst, dst, x1_flat, fw, ys)


BLOCK_SUBLANES = MOE_ROWS * ROW_SUBLANES


def _expert_mlp(x_tiles, y_tiles, wgu16, wd16, bgu_ref, bd_ref):
    xb = jnp.concatenate(
        [x_tiles[pl.ds(s, MOE_ROWS, stride=ROW_SUBLANES), :].astype(BF16) for s in range(ROW_SUBLANES)], axis=1)
    acc = None
    for j in range(D_EXPERT // FF_CHUNK):
        c0 = j * FF_CHUNK
        gate = _dot(xb, wgu16[:, c0:c0 + FF_CHUNK]) + bgu_ref[:, c0:c0 + FF_CHUNK]
        up = _dot(xb, wgu16[:, D_EXPERT + c0:D_EXPERT + c0 + FF_CHUNK]) + bgu_ref[:, D_EXPERT + c0:D_EXPERT + c0 + FF_CHUNK]
        gate = jnp.minimum(gate, SWIGLU_LIMIT)
        up = jnp.clip(up, -SWIGLU_LIMIT, SWIGLU_LIMIT)
        act = (up + 1.0) * gate * jax.nn.sigmoid(SWIGLU_ALPHA * gate)
        term = _dot(act.astype(BF16), wd16[c0:c0 + FF_CHUNK, :])
        acc = term if acc is None else acc + term
    y = acc + bd_ref[...]
    for s in range(ROW_SUBLANES):
        y_tiles[pl.ds(s, MOE_ROWS, stride=ROW_SUBLANES), :] = y[:, s * LANES:(s + 1) * LANES]


def _moe_kernel(be_ref, nu_ref, next_ref, xs_ref, wgu_hbm, bgu_ref, wd_hbm, bd_ref, ys_ref,
                wgu32, wd32, wgu16, wd16, group, wsem):
    i = pl.program_id(0)

    def fetch(e, slot):
        return (pltpu.make_async_copy(wgu_hbm.at[e], wgu32.at[slot], wsem.at[slot, 0]),
                pltpu.make_async_copy(wd_hbm.at[e], wd32.at[slot], wsem.at[slot, 1]))

    @pl.when(i < nu_ref[0])
    def _():
        e = be_ref[i]

        @pl.when(i == 0)
        def _():
            group[0] = 0
            for cp in fetch(e, 0):
                cp.start()

        @pl.when(jnp.logical_and(i > 0, e != be_ref[jnp.maximum(i - 1, 0)]))
        def _():
            group[0] = group[0] + 1

        @pl.when(jnp.logical_or(i == 0, e != be_ref[jnp.maximum(i - 1, 0)]))
        def _():
            slot = group[0] % 2
            for cp in fetch(e, slot):
                cp.wait()
            nxt = next_ref[e]

            @pl.when(nxt >= 0)
            def _():
                for cp in fetch(nxt, 1 - slot):
                    cp.start()
            wgu16[...] = wgu32[slot].astype(BF16)
            wd16[...] = wd32[slot].astype(BF16)
        _expert_mlp(xs_ref, ys_ref, wgu16, wd16, bgu_ref, bd_ref)

    @pl.when(i >= nu_ref[0])
    def _():
        ys_ref[...] = jnp.zeros(ys_ref.shape, F32)


def _moe_call(block_e, n_used, next_expert, xs, wgu, bgu, wd, bd):
    d = D_MODEL
    n_blocks = xs.shape[0] // BLOCK_SUBLANES
    blk = lambda i, be, nu, nx: (jnp.minimum(i, nu[0] - 1), 0)
    exp3 = lambda i, be, nu, nx: (be[jnp.minimum(i, nu[0] - 1)], 0, 0)
    grid_spec = pltpu.PrefetchScalarGridSpec(
        num_scalar_prefetch=3,
        grid=(n_blocks,),
        in_specs=[
            pl.BlockSpec((BLOCK_SUBLANES, LANES), blk),
            pl.BlockSpec(memory_space=pl.ANY),
            pl.BlockSpec((None, 1, 2 * D_EXPERT), exp3),
            pl.BlockSpec(memory_space=pl.ANY),
            pl.BlockSpec((None, 1, d), exp3),
        ],
        out_specs=pl.BlockSpec((BLOCK_SUBLANES, LANES), lambda i, be, nu, nx: (i, 0)),
        scratch_shapes=[pltpu.VMEM((2, d, 2 * D_EXPERT), F32), pltpu.VMEM((2, D_EXPERT, d), F32),
                        pltpu.VMEM((d, 2 * D_EXPERT), BF16), pltpu.VMEM((D_EXPERT, d), BF16),
                        pltpu.SMEM((1,), jnp.int32), pltpu.SemaphoreType.DMA((2, 2))],
    )
    return pl.pallas_call(
        _moe_kernel,
        grid_spec=grid_spec,
        out_shape=jax.ShapeDtypeStruct(xs.shape, F32),
        compiler_params=pltpu.CompilerParams(
            dimension_semantics=("arbitrary",), vmem_limit_bytes=VMEM_LIMIT_BYTES),
        name="moe",
    )(block_e, n_used, next_expert, xs, wgu, bgu, wd, bd)


def _pad_lanes(v, lane0):
    out = jnp.zeros((1, LANES), F32)
    return lax.dynamic_update_slice(out, v.reshape(1, -1).astype(F32), (0, lane0))


def kernel(x, attn_norm_w, w_in, conv_w, a_log, dt_bias, dn_norm_w, pool_w, pool_scale, w_out, ffn_norm_w,
           w_router, b_router, w_gate_up, b_gate_up, w_down, b_down, final_norm_w):
    bsz, seq, d = x.shape
    n_tok = bsz * seq
    assert w_in.shape[0] == 1, "the combine kernel fuses the final norm, so exactly one layer is supported"
    assert d == D_MODEL and seq % TILE == 0 and bsz % SEQ_PER_STEP == 0
    l = 0

    s_z, s_a = 4 * D_DN, 4 * D_DN + 2 * N_DN_HEADS
    ba_cols = jnp.pad(w_in[l][:, s_z:s_a], ((0, 0), (0, LANES - 2 * N_DN_HEADS)))
    win_r = jnp.concatenate([w_in[l][:, :s_z], w_in[l][:, s_a:], ba_cols], axis=1).astype(BF16)
    wr32 = jnp.pad(w_router[l].astype(F32), ((0, 0), (0, LANES - N_EXPERTS)))
    wr_hi = lax.bitcast_convert_type(
        lax.bitcast_convert_type(wr32, jnp.uint32) & jnp.uint32(0xFFFF0000), F32)
    wr_p = jnp.concatenate([wr_hi.astype(BF16), (wr32 - wr_hi).astype(BF16)], axis=1)
    br_p = _pad_lanes(b_router[l], 0)
    fnw = ffn_norm_w[l].reshape(1, d)

    n_blocks = (n_tok * TOP_K + N_EXPERTS * (MOE_ROWS + COPY_ROWS - 2)) // MOE_ROWS + 1
    n_rows = n_blocks * MOE_ROWS

    x1, hn_tiles, route, gates, meta, xs_zero = _mix_call(
        x, attn_norm_w[l].reshape(1, d), win_r, conv_w[l], _pad_lanes(a_log[l], A_LANE0),
        _pad_lanes(dt_bias[l], A_LANE0), dn_norm_w[l].reshape(1, DN_HEAD_DIM), pool_w[l].astype(BF16),
        pool_scale[l].reshape(1, D_POOL), w_out[l].astype(BF16), fnw, wr_p, br_p, n_rows * ROW_SUBLANES)

    n_tiles = n_tok // TILE
    meta = meta.reshape(n_tiles, N_EXPERTS, LANES)[:, :, :2].astype(jnp.int32)
    tile_cnt, tile_before = meta[:, :, 0], meta[:, :, 1]
    total = jnp.sum(tile_cnt, axis=0)
    padded = jnp.where(total > 0, ((total + COPY_ROWS - 1 + MOE_ROWS - 1) // MOE_ROWS) * MOE_ROWS, 0)
    pad_end = jnp.cumsum(padded)
    pad_start = pad_end - padded
    run_base = pad_start[None, :] + tile_before
    ncopy = (tile_cnt + COPY_ROWS - 1) // COPY_ROWS
    copy_end = jnp.cumsum(ncopy, axis=1)
    copy_start = copy_end - ncopy
    c_idx = jnp.arange(COPY_TABLE, dtype=jnp.int32)[None, :, None]
    in_run = jnp.logical_and(c_idx >= copy_start[:, None, :], c_idx < copy_end[:, None, :])
    dst_row = jnp.sum(jnp.where(in_run, run_base[:, None, :] + COPY_ROWS * (c_idx - copy_start[:, None, :]), 0), axis=2)
    dst = (dst_row * ROW_SUBLANES).at[:, COPY_TABLE - 1].set(copy_end[:, -1]).reshape(-1)
    block_start = jnp.arange(n_blocks, dtype=jnp.int32) * MOE_ROWS
    block_e = jnp.minimum(jnp.sum(block_start[:, None] >= pad_end[None, :], axis=1), N_EXPERTS - 1).astype(jnp.int32)
    n_used = (pad_end[-1:] // MOE_ROWS).astype(jnp.int32)

    def per_tile_table(a):
        return a.reshape(n_tiles, SUBLANES, TILE)[:, :TOP_K].reshape(-1)
    slots = per_tile_table(route)
    gatev = per_tile_table(gates)

    x1_flat = x1.reshape(n_tok, d)
    xs = _dispatch_call(slots, dst, hn_tiles.reshape(n_tok * ROW_SUBLANES, LANES), xs_zero)
    eidx = jnp.arange(N_EXPERTS, dtype=jnp.int32)
    later_owner = jnp.where(jnp.logical_and(padded[None, :] > 0, eidx[None, :] > eidx[:, None]), eidx[None, :], N_EXPERTS)
    next_expert = jnp.min(later_owner, axis=1)
    next_expert = jnp.where(next_expert < N_EXPERTS, next_expert, -1).astype(jnp.int32)
    ys = _moe_call(block_e, n_used, next_expert, xs, w_gate_up[l], b_gate_up[l].reshape(N_EXPERTS, 1, -1),
                   w_down[l], b_down[l].reshape(N_EXPERTS, 1, -1))
    out = _combine_call(slots, gatev, dst, x1_flat, final_norm_w.reshape(1, d), ys)
    return out.reshape(bsz, seq, d)
```
